```python
import math
import jax, jax.numpy as jnp
from jax import lax
import numpy as np

D_MODEL = 2048
BATCH = 8
SEQ = 2048
DEPTH = 1

HYENA_WIDTH = 1024
CONV_WIDTH = 3
FILTER_EMB_DIM = 33
FILTER_HIDDEN = 64
DECAY_TARGET = 1e-2
FAST_DECAY_PCT = 0.3
SLOW_DECAY_PCT = 1.5
DECAY_SHIFT = 0.05

N_HEADS = 8
QK_NOPE_DIM = 128
QK_ROPE_DIM = 64
V_HEAD_DIM = 128
Q_LORA_RANK = 512
KV_LORA_RANK = 256
ROPE_THETA = 10000.0
Q_BLOCK = 128

ATTN_WIDTH = N_HEADS * V_HEAD_DIM
MIX_WIDTH = HYENA_WIDTH + ATTN_WIDTH
IN_PROJ_WIDTH = 3 * HYENA_WIDTH + Q_LORA_RANK + KV_LORA_RANK + QK_ROPE_DIM

D_FF = 5632
NORM_EPS = 1e-6

kernel_name = "hybrid_hyena_mla_convffn_sandwich"


def _rmsnorm(x, gain):
    xf = x.astype(jnp.float32)
    y = xf * lax.rsqrt(jnp.mean(xf * xf, axis=-1, keepdims=True) + NORM_EPS)
    return (y * gain.astype(jnp.float32)).astype(x.dtype)


def _dwconv_centred(x, w, b):
    k = w.shape[0]
    half = k // 2
    s = x.shape[1]
    xp = jnp.pad(x, ((0, 0), (half, half), (0, 0)))
    return sum(xp[:, j:j + s] * w[j] for j in range(k)) + b


def _position_features(L):
    t = jnp.linspace(0.0, 1.0, L, dtype=jnp.float32)[:, None]
    bands = (FILTER_EMB_DIM - 1) // 2
    w = 2.0 * math.pi * jnp.arange(L, dtype=jnp.float32) / L
    f = jnp.linspace(1e-4, bands - 1, bands, dtype=jnp.float32)
    ang = w[:, None] * f[None, :]
    return t, jnp.concatenate([t, jnp.cos(ang), -jnp.sin(ang)], axis=-1)


def _implicit_filters(t, z, w1, b1, fr1, w2, b2, fr2, w3):
    f32 = jnp.float32
    L = t.shape[0]
    h = jnp.sin(fr1.astype(f32) * (z @ w1.astype(f32) + b1.astype(f32)))
    h = jnp.sin(fr2.astype(f32) * (h @ w2.astype(f32) + b2.astype(f32)))
    h = h @ w3.astype(f32)
    max_decay = math.log(DECAY_TARGET) / FAST_DECAY_PCT
    min_decay = math.log(DECAY_TARGET) / SLOW_DECAY_PCT
    deltas = jnp.abs(jnp.linspace(min_decay, max_decay, HYENA_WIDTH, dtype=f32))
    window = jnp.exp(-t * deltas[None, :]) + DECAY_SHIFT
    h = h.reshape(L, 2, HYENA_WIDTH) * window[:, None, :]
    h_fwd, h_bwd = h[:, 0], h[:, 1]
    return jnp.concatenate([h_fwd, jnp.zeros((1, HYENA_WIDTH), f32), h_bwd[1:][::-1]], axis=0)


def _bidir_long_conv(u, k2, bias):
    L = u.shape[1]
    n = 2 * L
    uf = jnp.fft.rfft(u.astype(jnp.float32), n=n, axis=1)
    kf = jnp.fft.rfft(k2, n=n, axis=0)
    y = jnp.fft.irfft(uf * kf[None], n=n, axis=1)[:, :L]
    return (y + u.astype(jnp.float32) * bias.astype(jnp.float32)).astype(u.dtype)


def _rope(x, cos, sin):
    x1, x2 = jnp.split(x, 2, axis=-1)
    return x * cos + jnp.concatenate([-x2, x1], axis=-1) * sin


def _mla(c_q, c_kv, k_pe, q_norm_gain, w_uq, kv_norm_gain, w_ukv, cos, sin):
    B, S, _ = c_q.shape
    dqk = QK_NOPE_DIM + QK_ROPE_DIM
    q = (_rmsnorm(c_q, q_norm_gain) @ w_uq).reshape(B, S, N_HEADS, dqk)
    q_nope, q_pe = q[..., :QK_NOPE_DIM], q[..., QK_NOPE_DIM:]
    kv = (_rmsnorm(c_kv, kv_norm_gain) @ w_ukv).reshape(B, S, N_HEADS, QK_NOPE_DIM + V_HEAD_DIM)
    k_nope, v = kv[..., :QK_NOPE_DIM], kv[..., QK_NOPE_DIM:]
    q_pe = _rope(q_pe, cos[:, None, :], sin[:, None, :])
    k_pe = _rope(k_pe, cos, sin)
    q = jnp.concatenate([q_nope, q_pe], axis=-1)
    k = jnp.concatenate([k_nope, jnp.broadcast_to(k_pe[:, :, None, :], (B, S, N_HEADS, QK_ROPE_DIM))], axis=-1)
    scale = dqk ** -0.5
    nb = S // Q_BLOCK
    qb = q.reshape(B, nb, Q_BLOCK, N_HEADS, dqk).transpose(1, 0, 2, 3, 4)

    def attend(q_blk):
        s = jnp.einsum('bqhd,bkhd->bhqk', q_blk, k).astype(jnp.float32) * scale
        p = jax.nn.softmax(s, axis=-1).astype(v.dtype)
        return jnp.einsum('bhqk,bkhd->bqhd', p, v)

    o = lax.map(attend, qb)
    return o.transpose(1, 0, 2, 3, 4).reshape(B, S, ATTN_WIDTH)


def setup_inputs(seed: int = 0) -> dict:
    key = jax.random.key(seed)
    ks = jax.random.split(key, 32)
    f32 = jnp.float32

    def nrm(k, shape, scale):
        return jax.random.normal(k, shape, f32) * scale

    def gain(k, n):
        return 1.0 + 0.05 * jax.random.normal(k, (DEPTH, n), f32)

    return {
        "x": nrm(ks[0], (BATCH, SEQ, D_MODEL), 1.0),
        "pre_mix_gain": gain(ks[1], D_MODEL),
        "w_in": nrm(ks[2], (DEPTH, D_MODEL, IN_PROJ_WIDTH), D_MODEL ** -0.5),
        "hyena_conv_w": nrm(ks[3], (DEPTH, CONV_WIDTH, 3 * HYENA_WIDTH), CONV_WIDTH ** -0.5),
        "hyena_conv_b": nrm(ks[4], (DEPTH, 3 * HYENA_WIDTH), 0.02),
        "filt_w1": nrm(ks[5], (DEPTH, FILTER_EMB_DIM, FILTER_HIDDEN), FILTER_EMB_DIM ** -0.5),
        "filt_b1": nrm(ks[6], (DEPTH, FILTER_HIDDEN), 0.02),
        "filt_freq1": 1.0 + 0.1 * jax.random.normal(ks[7], (DEPTH, FILTER_HIDDEN), f32),
        "filt_w2": nrm(ks[8], (DEPTH, FILTER_HIDDEN, FILTER_HIDDEN), FILTER_HIDDEN ** -0.5),
        "filt_b2": nrm(ks[9], (DEPTH, FILTER_HIDDEN), 0.02),
        "filt_freq2": 1.0 + 0.1 * jax.random.normal(ks[10], (DEPTH, FILTER_HIDDEN), f32),
        "filt_w3": nrm(ks[11], (DEPTH, FILTER_HIDDEN, 2 * HYENA_WIDTH), FILTER_HIDDEN ** -0.5),
        "hyena_bias": nrm(ks[12], (DEPTH, HYENA_WIDTH), 0.5),
        "q_norm_gain": gain(ks[13], Q_LORA_RANK),
        "w_uq": nrm(ks[14], (DEPTH, Q_LORA_RANK, N_HEADS * (QK_NOPE_DIM + QK_ROPE_DIM)), Q_LORA_RANK ** -0.5),
        "kv_norm_gain": gain(ks[15], KV_LORA_RANK),
        "w_ukv": nrm(ks[16], (DEPTH, KV_LORA_RANK, N_HEADS * (QK_NOPE_DIM + V_HEAD_DIM)), KV_LORA_RANK ** -0.5),
        "hyena_out_gain": gain(ks[17], HYENA_WIDTH),
        "attn_out_gain": gain(ks[18], ATTN_WIDTH),
        "w_out": nrm(ks[19], (DEPTH, MIX_WIDTH, D_MODEL), MIX_WIDTH ** -0.5),
        "post_mix_gain": gain(ks[20], D_MODEL),
        "pre_ffn_gain": gain(ks[21], D_MODEL),
        "w_up": nrm(ks[22], (DEPTH, D_MODEL, 2 * D_FF), D_MODEL ** -0.5),
        "ffn_conv_w": nrm(ks[23], (DEPTH, CONV_WIDTH, D_FF), CONV_WIDTH ** -0.5),
        "ffn_conv_b": nrm(ks[24], (DEPTH, D_FF), 0.02),
        "w_down": nrm(ks[25], (DEPTH, D_FF, D_MODEL), D_FF ** -0.5),
        "post_ffn_gain": gain(ks[26], D_MODEL),
    }


def reference(x, pre_mix_gain, w_in, hyena_conv_w, hyena_conv_b, filt_w1, filt_b1, filt_freq1,
              filt_w2, filt_b2, filt_freq2, filt_w3, hyena_bias, q_norm_gain, w_uq, kv_norm_gain,
              w_ukv, hyena_out_gain, attn_out_gain, w_out, post_mix_gain, pre_ffn_gain, w_up,
              ffn_conv_w, ffn_conv_b, w_down, post_ffn_gain):
    B, S, _ = x.shape
    pos = jnp.arange(S, dtype=jnp.float32)
    inv_freq = 1.0 / (ROPE_THETA ** (jnp.arange(0, QK_ROPE_DIM, 2, dtype=jnp.float32) / QK_ROPE_DIM))
    ang = pos[:, None] * inv_freq[None, :]
    ang = jnp.concatenate([ang, ang], axis=-1)
    cos = jnp.cos(ang).astype(x.dtype)
    sin = jnp.sin(ang).astype(x.dtype)
    t, z = _position_features(S)

    h = x
    for l in range(DEPTH):
        xn = _rmsnorm(h, pre_mix_gain[l])
        proj = xn @ w_in[l]
        o1 = 3 * HYENA_WIDTH
        o2 = o1 + Q_LORA_RANK
        o3 = o2 + KV_LORA_RANK
        xh, c_q, c_kv, k_pe = proj[..., :o1], proj[..., o1:o2], proj[..., o2:o3], proj[..., o3:]

        xh = _dwconv_centred(xh, hyena_conv_w[l], hyena_conv_b[l])
        x0 = xh[..., :HYENA_WIDTH]
        x1 = xh[..., HYENA_WIDTH:2 * HYENA_WIDTH]
        v = xh[..., 2 * HYENA_WIDTH:]
        k2 = _implicit_filters(t, z, filt_w1[l], filt_b1[l], filt_freq1[l], filt_w2[l], filt_b2[l],
                               filt_freq2[l], filt_w3[l])
        y_hyena = x0 * _bidir_long_conv(x1 * v, k2, hyena_bias[l])

        y_attn = _mla(c_q, c_kv, k_pe, q_norm_gain[l], w_uq[l], kv_norm_gain[l], w_ukv[l], cos, sin)

        mixed = jnp.concatenate([_rmsnorm(y_hyena, hyena_out_gain[l]),
                                 _rmsnorm(y_attn, attn_out_gain[l])], axis=-1) @ w_out[l]
        h = h + _rmsnorm(mixed, post_mix_gain[l])

        hn = _rmsnorm(h, pre_ffn_gain[l])
        gu = hn @ w_up[l]
        g = _dwconv_centred(gu[..., :D_FF], ffn_conv_w[l], ffn_conv_b[l])
        u = gu[..., D_FF:]
        f = (jax.nn.gelu(g, approximate=True) * u) @ w_down[l]
        h = h + _rmsnorm(f, post_ffn_gain[l])
    return h
```

```python
import functools
import math

import jax
import jax.numpy as jnp
from jax import lax
from jax.experimental import pallas as pl
from jax.experimental.pallas import tpu as pltpu

F32 = jnp.float32
BF16 = jnp.bfloat16

D_MODEL = 2048
HYENA_WIDTH = 1024
CONV_WIDTH = 3
FILTER_EMB_DIM = 33
FILTER_HIDDEN = 64
DECAY_TARGET = 1e-2
FAST_DECAY_PCT = 0.3
SLOW_DECAY_PCT = 1.5
DECAY_SHIFT = 0.05
N_HEADS = 8
QK_NOPE_DIM = 128
QK_ROPE_DIM = 64
V_HEAD_DIM = 128
Q_LORA_RANK = 512
KV_LORA_RANK = 256
ROPE_THETA = 10000.0
ATTN_WIDTH = N_HEADS * V_HEAD_DIM
D_FF = 5632
NORM_EPS = 1e-6

V7X_VMEM_BYTES = 64 * 1024 * 1024
V7X_LANES = 128
V7X_MXU_DIM = 256

QK_PAD_DIM = V7X_MXU_DIM
PROJ_WIDTH = 4096
COL_CQ = 3 * HYENA_WIDTH
COL_CKV = COL_CQ + Q_LORA_RANK
COL_KPE = COL_CKV + KV_LORA_RANK
COL_KPE_ROT = COL_KPE + V7X_LANES


def _params(semantics, vmem_bytes):
    return pltpu.CompilerParams(dimension_semantics=semantics,
                                vmem_limit_bytes=min(int(vmem_bytes), V7X_VMEM_BYTES - (4 << 20)))


def _rms(x, gain):
    return x * lax.rsqrt(jnp.mean(x * x, axis=-1, keepdims=True) + NORM_EPS) * gain


def _bdot(a, b):
    return jnp.dot(a, b, preferred_element_type=F32)


def _seq_conv3(x, w, b):
    t = x.shape[0]
    row = lax.broadcasted_iota(jnp.int32, x.shape, 0)
    prev = jnp.where(row == 0, 0.0, pltpu.roll(x, 1, 0))
    nxt = jnp.where(row == t - 1, 0.0, pltpu.roll(x, t - 1, 0))
    return prev * w[0:1] + x * w[1:2] + nxt * w[2:3] + b


def _inproj_body(x_ref, g_ref, w_ref, o_ref, xn_ref):
    @pl.when(pl.program_id(1) == 0)
    def _():
        xn_ref[...] = _rms(x_ref[...], g_ref[...]).astype(BF16)

    o_ref[...] = _bdot(xn_ref[...], w_ref[...]).astype(o_ref.dtype)


def _inproj(x2d, gain, w_ext):
    m, d = x2d.shape
    n = w_ext.shape[1]
    tm, tn = 1024, 1024
    vmem = 2 * tm * d * 4 + tm * d * 2 + 2 * d * tn * 2 + 2 * tm * tn * 2 + tm * tn * 4 + (8 << 20)
    return pl.pallas_call(
        _inproj_body,
        grid=(m // tm, n // tn),
        in_specs=[pl.BlockSpec((tm, d), lambda i, j: (i, 0)),
                  pl.BlockSpec((1, d), lambda i, j: (0, 0)),
                  pl.BlockSpec((d, tn), lambda i, j: (0, j))],
        out_specs=pl.BlockSpec((tm, tn), lambda i, j: (i, j)),
        out_shape=jax.ShapeDtypeStruct((m, n), BF16),
        scratch_shapes=[pltpu.VMEM((tm, d), BF16)],
        compiler_params=_params(("parallel", "arbitrary"), vmem),
        name="inproj",
    )(x2d, gain, w_ext)


def _dft_mats(p):
    idx = jnp.arange(p, dtype=jnp.int32)
    k = (idx[:, None] * idx[None, :]) % (2 * p)
    ang = k.astype(F32) * (math.pi / p)
    c = jnp.cos(ang)
    s = -jnp.sin(ang)
    alt = jnp.where(idx % 2 == 0, 1.0, -1.0).astype(F32)
    s_fwd = jnp.where(idx[:, None] == 0, alt[None, :], s)
    s_inv = jnp.where(idx[None, :] == 0, alt[:, None], s)
    return c.astype(BF16), s_fwd.astype(BF16), s_inv.astype(BF16)


def _filter_body(z_ref, t_ref, w1_ref, b1_ref, f1_ref, w2_ref, b2_ref, f2_ref, w3f_ref, w3b_ref,
                 dl_ref, c_ref, s_ref, o_ref, *, p):
    hp = lax.Precision.HIGHEST
    h = jnp.sin(f1_ref[...] * (jnp.dot(z_ref[...], w1_ref[...], precision=hp,
                                       preferred_element_type=F32) + b1_ref[...]))
    h = jnp.sin(f2_ref[...] * (jnp.dot(h, w2_ref[...], precision=hp,
                                       preferred_element_type=F32) + b2_ref[...]))
    win = jnp.exp(-t_ref[...] * dl_ref[...]) + DECAY_SHIFT
    hf = jnp.dot(h, w3f_ref[...], precision=hp, preferred_element_type=F32) * win
    hb = jnp.dot(h, w3b_ref[...], precision=hp, preferred_element_type=F32) * win
    lrow = lax.broadcasted_iota(jnp.int32, hb.shape, 0)
    hb = jnp.where(lrow == 0, 0.0, hb)

    cmat = c_ref[...]
    smat = s_ref[...]

    def fwd(x):
        hi = x.astype(BF16)
        lo = (x - hi.astype(F32)).astype(BF16)
        return (_bdot(cmat, hi) + _bdot(cmat, lo), _bdot(smat, hi) + _bdot(smat, lo))

    row = lax.broadcasted_iota(jnp.int32, (p, hf.shape[1]), 0)
    row0 = row == 0
    sigma = jnp.where(row % 2 == 1, -1.0, 1.0)

    def conj(a):
        return a[0], jnp.where(row0, a[1], -a[1])

    af0, af1 = fwd(hf[:p]), fwd(hf[p:])
    ab0, ab1 = fwd(hb[:p]), fwd(hb[p:])
    cb0 = conj(ab0)
    k0 = (af0[0] + cb0[0], af0[1] + cb0[1])
    k1 = (af1[0] + sigma * af0[0], af1[1] + sigma * af0[1])
    km1 = conj((ab1[0] + sigma * ab0[0], ab1[1] + sigma * ab0[1]))
    scale = jnp.where(row0, 0.5 / p, 1.0 / p)
    for i, a in enumerate((k0, k1, km1)):
        o_ref[2 * i] = a[0] * scale
        o_ref[2 * i + 1] = a[1] * scale


def _filter_spectra(z, t, w1, b1, f1, w2, b2, f2, w3, deltas, cmat, smat, p):
    l = z.shape[0]
    c = HYENA_WIDTH
    tc = 256
    nb = c // tc
    full = lambda a: pl.BlockSpec(a.shape, lambda j: (0,) * a.ndim)
    return pl.pallas_call(
        functools.partial(_filter_body, p=p),
        grid=(nb,),
        in_specs=[full(z), full(t), full(w1), full(b1), full(f1), full(w2), full(b2), full(f2),
                  pl.BlockSpec((FILTER_HIDDEN, tc), lambda j: (0, j)),
                  pl.BlockSpec((FILTER_HIDDEN, tc), lambda j: (0, j + nb)),
                  pl.BlockSpec((1, tc), lambda j: (0, j)),
                  full(cmat), full(smat)],
        out_specs=pl.BlockSpec((6, p, tc), lambda j: (0, 0, j)),
        out_shape=jax.ShapeDtypeStruct((6, p, c), F32),
        compiler_params=_params(("parallel",), 40 << 20),
        name="filt",
    )(z, t, w1, b1, f1, w2, b2, f2, w3, w3, deltas, cmat, smat)


def _hyena_body(x0_ref, x1_ref, v_ref, cw_ref, cb_ref, ks_ref, hb_ref, c_ref, sf_ref, si_ref, o_ref, *, p):
    x0 = _seq_conv3(x0_ref[...].astype(F32), cw_ref[0], cb_ref[0])
    x1 = _seq_conv3(x1_ref[...].astype(F32), cw_ref[1], cb_ref[1])
    v = _seq_conv3(v_ref[...].astype(F32), cw_ref[2], cb_ref[2])
    u = x1 * v
    ub = u.astype(BF16)
    cmat = c_ref[...]
    sfwd = sf_ref[...]
    sinv = si_ref[...]
    row0 = lax.broadcasted_iota(jnp.int32, (p, u.shape[1]), 0) == 0

    spec = [(_bdot(cmat, ub[j * p:(j + 1) * p]), _bdot(sfwd, ub[j * p:(j + 1) * p])) for j in range(2)]

    def cmul(i, uu):
        kre, kim = ks_ref[2 * i], ks_ref[2 * i + 1]
        ii = kim * uu[1]
        return (kre * uu[0] - jnp.where(row0, 0.0, ii),
                jnp.where(row0, ii, kre * uu[1] + kim * uu[0]))

    def inv(a, b):
        return _bdot(cmat, (a[0] + b[0]).astype(BF16)) + _bdot(sinv, (a[1] + b[1]).astype(BF16))

    y_lo = inv(cmul(0, spec[0]), cmul(2, spec[1]))
    y_hi = inv(cmul(1, spec[0]), cmul(0, spec[1]))
    y = jnp.concatenate([y_lo, y_hi], axis=0)
    o_ref[...] = (x0 * (y + u * hb_ref[...])).astype(o_ref.dtype)


def _hyena(proj, conv_w, conv_b, kspec, hbias, cmat, sfwd, sinv, batch, seq, p):
    c = HYENA_WIDTH
    tc = 256
    nb = c // tc
    xspec = lambda off: pl.BlockSpec((seq, tc), lambda j, b: (b, j + off * nb))
    full = lambda a: pl.BlockSpec(a.shape, lambda j, b: (0,) * a.ndim)
    vmem = (6 * seq * tc * 2 + 2 * 6 * p * tc * 4 + 6 * p * p * 2 + 2 * seq * tc * 2
            + 14 * seq * tc * 4 + (6 << 20))
    return pl.pallas_call(
        functools.partial(_hyena_body, p=p),
        grid=(nb, batch),
        in_specs=[xspec(0), xspec(1), xspec(2),
                  pl.BlockSpec((3, CONV_WIDTH, tc), lambda j, b: (0, 0, j)),
                  pl.BlockSpec((3, 1, tc), lambda j, b: (0, 0, j)),
                  pl.BlockSpec((6, p, tc), lambda j, b: (0, 0, j)),
                  pl.BlockSpec((1, tc), lambda j, b: (0, j)),
                  full(cmat), full(sfwd), full(sinv)],
        out_specs=pl.BlockSpec((seq, tc), lambda j, b: (b, j)),
        out_shape=jax.ShapeDtypeStruct((batch * seq, c), BF16),
        compiler_params=_params(("parallel", "parallel"), vmem),
        name="hyena",
    )(proj, proj, proj, conv_w, conv_b, kspec, hbias, cmat, sfwd, sinv)


def _qkv_body(cq_ref, ckv_ref, ka_ref, kb_ref, cos_ref, sin_ref, gq_ref, gkv_ref, wa_ref, wb_ref, wkv_ref,
              q_ref, k_ref, v_ref):
    scale = (QK_NOPE_DIM + QK_ROPE_DIM) ** -0.5
    cqn = _rms(cq_ref[...].astype(F32), gq_ref[...]).astype(BF16)
    ckvn = _rms(ckv_ref[...].astype(F32), gkv_ref[...]).astype(BF16)
    cos = cos_ref[...]
    sin = sin_ref[...]
    qa = _bdot(cqn, wa_ref[...])
    qb = _bdot(cqn, wb_ref[...])
    kv = _bdot(ckvn, wkv_ref[...])
    kpe = (ka_ref[...].astype(F32) * cos + kb_ref[...].astype(F32) * sin).astype(BF16)
    for h in range(N_HEADS):
        a = h * QK_PAD_DIM
        r = h * V7X_LANES
        q_ref[0, h, :, 0:128] = (qa[:, a:a + 128] * scale).astype(BF16)
        q_ref[0, h, :, 128:256] = ((qa[:, a + 128:a + 256] * cos + qb[:, r:r + 128] * sin) * scale).astype(BF16)
        k_ref[0, h, :, 0:128] = kv[:, a:a + 128].astype(BF16)
        k_ref[0, h, :, 128:256] = kpe
        v_ref[0, h] = kv[:, a + 128:a + 256].astype(BF16)


def _qkv(proj, cos_t, sin_t, gq, gkv, wa, wb, wkv, batch, seq):
    tm = 512
    ns = seq // tm
    full = lambda a: pl.BlockSpec(a.shape, lambda b, i: (0,) * a.ndim)
    col = lambda width, off: pl.BlockSpec((tm, width), lambda b, i: (b * ns + i, off // width))
    hd = lambda w: pl.BlockSpec((1, N_HEADS, tm, w), lambda b, i: (b, 0, i, 0))
    return pl.pallas_call(
        _qkv_body,
        grid=(batch, ns),
        in_specs=[col(Q_LORA_RANK, COL_CQ), col(KV_LORA_RANK, COL_CKV),
                  col(V7X_LANES, COL_KPE), col(V7X_LANES, COL_KPE_ROT),
                  pl.BlockSpec((tm, V7X_LANES), lambda b, i: (i, 0)),
                  pl.BlockSpec((tm, V7X_LANES), lambda b, i: (i, 0)),
                  full(gq), full(gkv), full(wa), full(wb), full(wkv)],
        out_specs=[hd(QK_PAD_DIM), hd(QK_PAD_DIM), hd(V_HEAD_DIM)],
        out_shape=[jax.ShapeDtypeStruct((batch, N_HEADS, seq, QK_PAD_DIM), BF16),
                   jax.ShapeDtypeStruct((batch, N_HEADS, seq, QK_PAD_DIM), BF16),
                   jax.ShapeDtypeStruct((batch, N_HEADS, seq, V_HEAD_DIM), BF16)],
        compiler_params=_params(("parallel", "parallel"), 48 << 20),
        name="qkv",
    )(proj, proj, proj, proj, cos_t, sin_t, gq, gkv, wa, wb, wkv)


def _attn_body(q_ref, k_ref, v_ref, o_ref):
    s = lax.dot_general(q_ref[0, 0], k_ref[0, 0], (((1,), (1,)), ((), ())), preferred_element_type=F32)
    m = jnp.max(s, axis=-1, keepdims=True)
    e = jnp.exp(s - m)
    l = jnp.sum(e, axis=-1, keepdims=True)
    o = _bdot(e.astype(BF16), v_ref[0, 0])
    o_ref[0] = (o / l).astype(o_ref.dtype)


def _attn(q, k, v):
    batch, heads, seq, _ = q.shape
    tq = 512
    return pl.pallas_call(
        _attn_body,
        grid=(batch, heads, seq // tq),
        in_specs=[pl.BlockSpec((1, 1, tq, QK_PAD_DIM), lambda b, h, i: (b, h, i, 0)),
                  pl.BlockSpec((1, 1, seq, QK_PAD_DIM), lambda b, h, i: (b, h, 0, 0)),
                  pl.BlockSpec((1, 1, seq, V_HEAD_DIM), lambda b, h, i: (b, h, 0, 0))],
        out_specs=pl.BlockSpec((1, tq, V_HEAD_DIM), lambda b, h, i: (b, i, h)),
        out_shape=jax.ShapeDtypeStruct((batch, seq, heads * V_HEAD_DIM), BF16),
        compiler_params=_params(("parallel", "parallel", "arbitrary"), 40 << 20),
        name="attn",
    )(q, k, v)


def _outproj_body(yh_ref, ya_ref, gh_ref, ga_ref, w_ref, x_ref, gpm_ref, gpf_ref, h_ref, hn_ref):
    a = _rms(yh_ref[...].astype(F32), gh_ref[...]).astype(BF16)
    b = _rms(ya_ref[...].astype(F32), ga_ref[...]).astype(BF16)
    mixed = _bdot(a, w_ref[0:HYENA_WIDTH, :]) + _bdot(b, w_ref[HYENA_WIDTH:, :])
    h = x_ref[...] + _rms(mixed, gpm_ref[...])
    h_ref[...] = h
    hn_ref[...] = _rms(h, gpf_ref[...]).astype(BF16)


def _outproj(yh, ya, gh, ga, w_out, x2d, gpm, gpf):
    m, d = x2d.shape
    tm = 512
    full = lambda a: pl.BlockSpec(a.shape, lambda i: (0,) * a.ndim)
    rows = lambda w: pl.BlockSpec((tm, w), lambda i: (i, 0))
    vmem = 2 * 2 * tm * HYENA_WIDTH * 2 + 2 * d * d * 2 + 2 * tm * d * (4 + 4 + 2) + 4 * tm * d * 4 + (6 << 20)
    return pl.pallas_call(
        _outproj_body,
        grid=(m // tm,),
        in_specs=[rows(HYENA_WIDTH), rows(ATTN_WIDTH), full(gh), full(ga), full(w_out), rows(d),
                  full(gpm), full(gpf)],
        out_specs=[rows(d), rows(d)],
        out_shape=[jax.ShapeDtypeStruct((m, d), F32), jax.ShapeDtypeStruct((m, d), BF16)],
        compiler_params=_params(("parallel",), vmem),
        name="outproj",
    )(yh, ya, gh, ga, w_out, x2d, gpm, gpf)


def _ffn_up_body(hn_ref, wg_ref, wu_ref, cw_ref, cb_ref, o_ref):
    hn = hn_ref[...]
    g = _seq_conv3(_bdot(hn, wg_ref[...]), cw_ref[...], cb_ref[...])
    u = _bdot(hn, wu_ref[...])
    gelu = 0.5 * g * (1.0 + jnp.tanh(math.sqrt(2.0 / math.pi) * (g + 0.044715 * (g * g * g))))
    o_ref[...] = (gelu * u).astype(o_ref.dtype)


def _ffn_up(hn, w_up, conv_w, conv_b, batch, seq):
    d = hn.shape[1]
    tn = 256
    nb = D_FF // tn
    vmem = 2 * seq * d * 2 + 2 * 2 * d * tn * 2 + 2 * seq * tn * 2 + 8 * seq * tn * 4 + (6 << 20)
    return pl.pallas_call(
        _ffn_up_body,
        grid=(batch, nb),
        in_specs=[pl.BlockSpec((seq, d), lambda b, j: (b, 0)),
                  pl.BlockSpec((d, tn), lambda b, j: (0, j)),
                  pl.BlockSpec((d, tn), lambda b, j: (0, j + nb)),
                  pl.BlockSpec((CONV_WIDTH, tn), lambda b, j: (0, j)),
                  pl.BlockSpec((1, tn), lambda b, j: (0, j))],
        out_specs=pl.BlockSpec((seq, tn), lambda b, j: (b, j)),
        out_shape=jax.ShapeDtypeStruct((batch * seq, D_FF), BF16),
        compiler_params=_params(("parallel", "arbitrary"), vmem),
        name="ffn_up",
    )(hn, w_up, w_up, conv_w, conv_b)


def _ffn_down_body(a_ref, w_ref, h_ref, g_ref, o_ref, acc_ref):
    k = pl.program_id(1)

    @pl.when(k == 0)
    def _():
        acc_ref[...] = jnp.zeros_like(acc_ref)

    acc_ref[...] += _bdot(a_ref[...], w_ref[...])

    @pl.when(k == pl.num_programs(1) - 1)
    def _():
        o_ref[...] = h_ref[...] + _rms(acc_ref[...], g_ref[...])


def _ffn_down(act, w_down, h, gain):
    m, d = h.shape
    tm, tk = 1024, 512
    vmem = 2 * tm * tk * 2 + 2 * tk * d * 2 + 2 * 2 * tm * d * 4 + 2 * tm * d * 4 + (4 << 20)
    return pl.pallas_call(
        _ffn_down_body,
        grid=(m // tm, D_FF // tk),
        in_specs=[pl.BlockSpec((tm, tk), lambda i, k: (i, k)),
                  pl.BlockSpec((tk, d), lambda i, k: (k, 0)),
                  pl.BlockSpec((tm, d), lambda i, k: (i, 0)),
                  pl.BlockSpec((1, d), lambda i, k: (0, 0))],
        out_specs=pl.BlockSpec((tm, d), lambda i, k: (i, 0)),
        out_shape=jax.ShapeDtypeStruct((m, d), F32),
        scratch_shapes=[pltpu.VMEM((tm, d), F32)],
        compiler_params=_params(("parallel", "arbitrary"), vmem),
        name="ffn_down",
    )(act, w_down, h, gain)


def _rotate_half_cols(w):
    half = w.shape[1] // 2
    return jnp.concatenate([-w[:, half:], w[:, :half]], axis=1)


def _position_features(l):
    t = jnp.linspace(0.0, 1.0, l, dtype=F32)[:, None]
    bands = (FILTER_EMB_DIM - 1) // 2
    w = 2.0 * math.pi * jnp.arange(l, dtype=F32) / l
    f = jnp.linspace(1e-4, bands - 1, bands, dtype=F32)
    ang = w[:, None] * f[None, :]
    return t, jnp.concatenate([t, jnp.cos(ang), -jnp.sin(ang)], axis=-1)


def _layer(h, l, seq, prm):
    (pre_mix_gain, w_in, hyena_conv_w, hyena_conv_b, filt_w1, filt_b1, filt_freq1, filt_w2, filt_b2,
     filt_freq2, filt_w3, hyena_bias, q_norm_gain, w_uq, kv_norm_gain, w_ukv, hyena_out_gain, attn_out_gain,
     w_out, post_mix_gain, pre_ffn_gain, w_up, ffn_conv_w, ffn_conv_b, w_down, post_ffn_gain) = prm
    batch = h.shape[0]
    p = seq // 2
    x2d = h.reshape(batch * seq, D_MODEL)
    row = lambda a: a[l][None, :].astype(F32)

    w = w_in[l]
    w_kpe = w[:, COL_KPE:COL_KPE + QK_ROPE_DIM]
    zpad = jnp.zeros((D_MODEL, V7X_LANES - QK_ROPE_DIM), w.dtype)
    w_ext = jnp.concatenate([w, zpad, _rotate_half_cols(w_kpe), zpad], axis=1).astype(BF16)
    proj = _inproj(x2d, row(pre_mix_gain), w_ext)

    cmat, sfwd, sinv = _dft_mats(p)
    t, z = _position_features(seq)
    zp = jnp.pad(z, ((0, 0), (0, V7X_LANES - FILTER_EMB_DIM)))
    w1p = jnp.pad(filt_w1[l], ((0, V7X_LANES - FILTER_EMB_DIM), (0, 0)))
    max_decay = math.log(DECAY_TARGET) / FAST_DECAY_PCT
    min_decay = math.log(DECAY_TARGET) / SLOW_DECAY_PCT
    deltas = jnp.abs(jnp.linspace(min_decay, max_decay, HYENA_WIDTH, dtype=F32))[None, :]
    kspec = _filter_spectra(zp, t, w1p, row(filt_b1), row(filt_freq1), filt_w2[l], row(filt_b2),
                            row(filt_freq2), filt_w3[l], deltas, cmat, sfwd, p)
    conv_w = hyena_conv_w[l].reshape(CONV_WIDTH, 3, HYENA_WIDTH).transpose(1, 0, 2)
    conv_b = hyena_conv_b[l].reshape(3, 1, HYENA_WIDTH)
    y_hyena = _hyena(proj, conv_w, conv_b, kspec, row(hyena_bias), cmat, sfwd, sinv, batch, seq, p)

    pos = jnp.arange(seq, dtype=F32)
    inv_freq = 1.0 / (ROPE_THETA ** (jnp.arange(0, QK_ROPE_DIM, 2, dtype=F32) / QK_ROPE_DIM))
    ang = pos[:, None] * inv_freq[None, :]
    ang = jnp.concatenate([ang, ang], axis=-1)
    lpad = ((0, 0), (0, V7X_LANES - QK_ROPE_DIM))
    cos_t = jnp.pad(jnp.cos(ang), lpad)
    sin_t = jnp.pad(jnp.sin(ang), lpad)
    dqk = QK_NOPE_DIM + QK_ROPE_DIM
    wq = w_uq[l].reshape(Q_LORA_RANK, N_HEADS, dqk)
    wa = jnp.pad(wq, ((0, 0), (0, 0), (0, QK_PAD_DIM - dqk))).reshape(Q_LORA_RANK, N_HEADS * QK_PAD_DIM)
    wq_pe = wq[:, :, QK_NOPE_DIM:]
    wq_rot = jnp.concatenate([-wq_pe[..., QK_ROPE_DIM // 2:], wq_pe[..., :QK_ROPE_DIM // 2]], axis=-1)
    wb = jnp.pad(wq_rot, ((0, 0), (0, 0), (0, V7X_LANES - QK_ROPE_DIM))).reshape(Q_LORA_RANK, N_HEADS * V7X_LANES)
    q, k, v = _qkv(proj, cos_t, sin_t, row(q_norm_gain), row(kv_norm_gain), wa.astype(BF16), wb.astype(BF16),
                   w_ukv[l].astype(BF16), batch, seq)
    y_attn = _attn(q, k, v).reshape(batch * seq, ATTN_WIDTH)

    h2d, hn = _outproj(y_hyena, y_attn, row(hyena_out_gain), row(attn_out_gain), w_out[l].astype(BF16), x2d,
                       row(post_mix_gain), row(pre_ffn_gain))

    act = _ffn_up(hn, w_up[l].astype(BF16), ffn_conv_w[l], row(ffn_conv_b), batch, seq)
    out = _ffn_down(act, w_down[l].astype(BF16), h2d, row(post_ffn_gain))
    return out.reshape(batch, seq, D_MODEL)


def kernel(x, pre_mix_gain, w_in, hyena_conv_w, hyena_conv_b, filt_w1, filt_b1, filt_freq1, filt_w2, filt_b2,
           filt_freq2, filt_w3, hyena_bias, q_norm_gain, w_uq, kv_norm_gain, w_ukv, hyena_out_gain,
           attn_out_gain, w_out, post_mix_gain, pre_ffn_gain, w_up, ffn_conv_w, ffn_conv_b, w_down,
           post_ffn_gain):
    prm = (pre_mix_gain, w_in, hyena_conv_w, hyena_conv_b, filt_w1, filt_b1, filt_freq1, filt_w2, filt_b2,
           filt_freq2, filt_w3, hyena_bias, q_norm_gain, w_uq, kv_norm_gain, w_ukv, hyena_out_gain,
           attn_out_gain, w_out, post_mix_gain, pre_ffn_gain, w_up, ffn_conv_w, ffn_conv_b, w_down,
           post_ffn_gain)
    seq = x.shape[1]
    h = x
    for l in range(w_in.shape[0]):
        h = _layer(h, l, seq, prm)
    return h
```

```python
import functools
import math

import jax
import jax.numpy as jnp
from jax import lax
from jax.experimental import pallas as pl
from jax.experimental.pallas import tpu as pltpu

F32 = jnp.float32
BF16 = jnp.bfloat16

D_MODEL = 2048
HYENA_WIDTH = 1024
CONV_WIDTH = 3
FILTER_EMB_DIM = 33
FILTER_HIDDEN = 64
DECAY_TARGET = 1e-2
FAST_DECAY_PCT = 0.3
SLOW_DECAY_PCT = 1.5
DECAY_SHIFT = 0.05
N_HEADS = 8
QK_NOPE_DIM = 128
QK_ROPE_DIM = 64
V_HEAD_DIM = 128
Q_LORA_RANK = 512
KV_LORA_RANK = 256
ROPE_THETA = 10000.0
ATTN_WIDTH = N_HEADS * V_HEAD_DIM
D_FF = 5632
NORM_EPS = 1e-6

V7X_VMEM_BYTES = 64 * 1024 * 1024
V7X_LANES = 128
V7X_MXU_DIM = 256
V7X_SUBLANES = 8
HALO = V7X_SUBLANES

QK_PAD_DIM = V7X_MXU_DIM
PROJ_WIDTH = 4096
COL_CQ = 3 * HYENA_WIDTH
COL_CKV = COL_CQ + Q_LORA_RANK
COL_KPE = COL_CKV + KV_LORA_RANK
COL_KPE_ROT = COL_KPE + V7X_LANES


def _params(semantics, vmem_bytes, flags=None):
    return pltpu.CompilerParams(dimension_semantics=semantics, flags=flags,
                                vmem_limit_bytes=min(int(vmem_bytes), V7X_VMEM_BYTES - (4 << 20)))


def _rms(x, gain):
    return x * lax.rsqrt(jnp.mean(x * x, axis=-1, keepdims=True) + NORM_EPS) * gain


def _bdot(a, b):
    return jnp.dot(a, b, preferred_element_type=F32)


def _seq_conv3(x, w, b):
    t = x.shape[0]
    row = lax.broadcasted_iota(jnp.int32, x.shape, 0)
    prev = jnp.where(row == 0, 0.0, pltpu.roll(x, 1, 0))
    nxt = jnp.where(row == t - 1, 0.0, pltpu.roll(x, t - 1, 0))
    return prev * w[0:1] + x * w[1:2] + nxt * w[2:3] + b


def _inproj_body(x_ref, g_ref, wm_ref, wt_ref, o_ref, xn_ref, *, n_main):
    j = pl.program_id(1)

    @pl.when(j == 0)
    def _():
        xn_ref[...] = _rms(x_ref[...], g_ref[...]).astype(BF16)

    @pl.when(j < n_main)
    def _():
        o_ref[...] = _bdot(xn_ref[...], wm_ref[...]).astype(o_ref.dtype)

    @pl.when(j == n_main)
    def _():
        o_ref[...] = _bdot(xn_ref[...], wt_ref[...]).astype(o_ref.dtype)


def _inproj(x2d, gain, w_main, w_tail):
    m, d = x2d.shape
    tm, tn = 1024, w_tail.shape[1]
    n_main = w_main.shape[1] // tn
    vmem = 2 * tm * d * 4 + tm * d * 2 + 4 * d * tn * 2 + 2 * tm * tn * 2 + tm * tn * 4 + (8 << 20)
    return pl.pallas_call(
        functools.partial(_inproj_body, n_main=n_main),
        grid=(m // tm, n_main + 1),
        in_specs=[pl.BlockSpec((tm, d), lambda i, j: (i, 0)),
                  pl.BlockSpec((1, d), lambda i, j: (0, 0)),
                  pl.BlockSpec((d, tn), lambda i, j: (0, jnp.minimum(j, n_main - 1))),
                  pl.BlockSpec((d, tn), lambda i, j: (0, 0))],
        out_specs=pl.BlockSpec((tm, tn), lambda i, j: (i, j)),
        out_shape=jax.ShapeDtypeStruct((m, (n_main + 1) * tn), BF16),
        scratch_shapes=[pltpu.VMEM((tm, d), BF16)],
        compiler_params=_params(("parallel", "arbitrary"), vmem),
        name="inproj",
    )(x2d, gain, w_main, w_tail)


def _dft_mats(p):
    r = 32
    assert p == r * r
    idx = jnp.arange(p, dtype=jnp.int32)
    sub = jnp.arange(r, dtype=jnp.int32)
    ang_hi = ((idx[:, None] * (r * sub)[None, :]) % (2 * p)).astype(F32) * (math.pi / p)
    ang_lo = ((idx[:, None] * sub[None, :]) % (2 * p)).astype(F32) * (math.pi / p)
    ch, sh = jnp.cos(ang_hi)[:, :, None], jnp.sin(ang_hi)[:, :, None]
    cl, sl = jnp.cos(ang_lo)[:, None, :], jnp.sin(ang_lo)[:, None, :]
    c = (ch * cl - sh * sl).reshape(p, p)
    s = -(sh * cl + ch * sl).reshape(p, p)
    alt = jnp.where(idx % 2 == 0, 1.0, -1.0).astype(F32)
    s_fwd = jnp.where(idx[:, None] == 0, alt[None, :], s)
    s_inv = jnp.where(idx[None, :] == 0, alt[:, None], s)
    return c.astype(BF16), s_fwd.astype(BF16), s_inv.astype(BF16)


def _filter_body(z_ref, t_ref, w1_ref, b1_ref, f1_ref, w2_ref, b2_ref, f2_ref, w3f_ref, w3b_ref,
                 dl_ref, c_ref, s_ref, o_ref, hid_ref, *, p):
    hp = lax.Precision.HIGHEST

    @pl.when(pl.program_id(0) == 0)
    def _():
        h1 = jnp.sin(f1_ref[...] * (jnp.dot(z_ref[...], w1_ref[...], precision=hp,
                                            preferred_element_type=F32) + b1_ref[...]))
        hid_ref[...] = jnp.sin(f2_ref[...] * (jnp.dot(h1, w2_ref[...], precision=hp,
                                                      preferred_element_type=F32) + b2_ref[...]))

    h = hid_ref[...]
    win =jnp.exp(-t_ref[...] * dl_ref[...]) + DECAY_SHIFT
    hf = jnp.dot(h, w3f_ref[...], precision=hp, preferred_element_type=F32) * win
    hb = jnp.dot(h, w3b_ref[...], precision=hp, preferred_element_type=F32) * win
    lrow = lax.broadcasted_iota(jnp.int32, hb.shape, 0)
    hb = jnp.where(lrow == 0, 0.0, hb)

    cmat = c_ref[...]
    smat = s_ref[...]

    def fwd(x):
        hi = x.astype(BF16)
        lo = (x - hi.astype(F32)).astype(BF16)
        return (_bdot(cmat, hi) + _bdot(cmat, lo), _bdot(smat, hi) + _bdot(smat, lo))

    row = lax.broadcasted_iota(jnp.int32, (p, hf.shape[1]), 0)
    row0 = row == 0
    sigma = jnp.where(row % 2 == 1, -1.0, 1.0)

    def conj(a):
        return a[0], jnp.where(row0, a[1], -a[1])

    af0, af1 = fwd(hf[:p]), fwd(hf[p:])
    ab0, ab1 = fwd(hb[:p]), fwd(hb[p:])
    cb0 = conj(ab0)
    k0 = (af0[0] + cb0[0], af0[1] + cb0[1])
    k1 = (af1[0] + sigma * af0[0], af1[1] + sigma * af0[1])
    km1 = conj((ab1[0] + sigma * ab0[0], ab1[1] + sigma * ab0[1]))
    scale = jnp.where(row0, 0.5 / p, 1.0 / p)
    for i, a in enumerate((k0, k1, km1)):
        o_ref[2 * i] = a[0] * scale
        o_ref[2 * i + 1] = a[1] * scale


def _filter_spectra(z, t, w1, b1, f1, w2, b2, f2, w3, deltas, cmat, smat, p):
    l = z.shape[0]
    c = HYENA_WIDTH
    tc = 256
    nb = c // tc
    full = lambda a: pl.BlockSpec(a.shape, lambda j: (0,) * a.ndim)
    return pl.pallas_call(
        functools.partial(_filter_body, p=p),
        grid=(nb,),
        in_specs=[full(z), full(t), full(w1), full(b1), full(f1), full(w2), full(b2), full(f2),
                  pl.BlockSpec((FILTER_HIDDEN, tc), lambda j: (0, j)),
                  pl.BlockSpec((FILTER_HIDDEN, tc), lambda j: (0, j + nb)),
                  pl.BlockSpec((1, tc), lambda j: (0, j)),
                  full(cmat), full(smat)],
        out_specs=pl.BlockSpec((6, p, tc), lambda j: (0, 0, j)),
        out_shape=jax.ShapeDtypeStruct((6, p, c), F32),
        scratch_shapes=[pltpu.VMEM((l, FILTER_HIDDEN), F32)],
        compiler_params=_params(("arbitrary",), 40 << 20),
        name="filt",
    )(z, t, w1, b1, f1, w2, b2, f2, w3, w3, deltas, cmat, smat)


def _hyena_body(x0_ref, x1_ref, v_ref, cw_ref, cb_ref, ks_ref, hb_ref, c_ref, sf_ref, si_ref, o_ref, *, p):
    x0 = _seq_conv3(x0_ref[...].astype(F32), cw_ref[0], cb_ref[0])
    x1 = _seq_conv3(x1_ref[...].astype(F32), cw_ref[1], cb_ref[1])
    v = _seq_conv3(v_ref[...].astype(F32), cw_ref[2], cb_ref[2])
    u = x1 * v
    ub = u.astype(BF16)
    cmat = c_ref[...]
    sfwd = sf_ref[...]
    sinv = si_ref[...]
    row0 = lax.broadcasted_iota(jnp.int32, (p, u.shape[1]), 0) == 0

    spec = [(_bdot(cmat, ub[j * p:(j + 1) * p]), _bdot(sfwd, ub[j * p:(j + 1) * p])) for j in range(2)]

    def cmul(i, uu):
        kre, kim = ks_ref[2 * i], ks_ref[2 * i + 1]
        ii = kim * uu[1]
        return (kre * uu[0] - jnp.where(row0, 0.0, ii),
                jnp.where(row0, ii, kre * uu[1] + kim * uu[0]))

    def inv(a, b):
        return _bdot(cmat, (a[0] + b[0]).astype(BF16)) + _bdot(sinv, (a[1] + b[1]).astype(BF16))

    y_lo = inv(cmul(0, spec[0]), cmul(2, spec[1]))
    y_hi = inv(cmul(1, spec[0]), cmul(0, spec[1]))
    y = jnp.concatenate([y_lo, y_hi], axis=0)
    o_ref[...] = (x0 * (y + u * hb_ref[...])).astype(o_ref.dtype)


def _hyena(proj, conv_w, conv_b, kspec, hbias, cmat, sfwd, sinv, batch, seq, p):
    c = HYENA_WIDTH
    tc = 256
    nb = c // tc
    xspec = lambda off: pl.BlockSpec((seq, tc), lambda j, b: (b, j + off * nb))
    full = lambda a: pl.BlockSpec(a.shape, lambda j, b: (0,) * a.ndim)
    vmem = (6 * seq * tc * 2 + 2 * 6 * p * tc * 4 + 6 * p * p * 2 + 2 * seq * tc * 2
            + 14 * seq * tc * 4 + (6 << 20))
    return pl.pallas_call(
        functools.partial(_hyena_body, p=p),
        grid=(nb, batch),
        in_specs=[xspec(0), xspec(1), xspec(2),
                  pl.BlockSpec((3, CONV_WIDTH, tc), lambda j, b: (0, 0, j)),
                  pl.BlockSpec((3, 1, tc), lambda j, b: (0, 0, j)),
                  pl.BlockSpec((6, p, tc), lambda j, b: (0, 0, j)),
                  pl.BlockSpec((1, tc), lambda j, b: (0, j)),
                  full(cmat), full(sfwd), full(sinv)],
        out_specs=pl.BlockSpec((seq, tc), lambda j, b: (b, j)),
        out_shape=jax.ShapeDtypeStruct((batch * seq, c), BF16),
        compiler_params=_params(("parallel", "parallel"), vmem),
        name="hyena",
    )(proj, proj, proj, conv_w, conv_b, kspec, hbias, cmat, sfwd, sinv)


def _qkv_body(cq_ref, ckv_ref, ka_ref, kb_ref, cos_ref, sin_ref, gq_ref, gkv_ref, wa_ref, wb_ref, wkv_ref,
              q_ref, k_ref, v_ref):
    scale = (QK_NOPE_DIM + QK_ROPE_DIM) ** -0.5
    cqn = _rms(cq_ref[...].astype(F32), gq_ref[...]).astype(BF16)
    ckvn = _rms(ckv_ref[...].astype(F32), gkv_ref[...]).astype(BF16)
    cos = cos_ref[...]
    sin = sin_ref[...]
    qa = _bdot(cqn, wa_ref[...])
    qb = _bdot(cqn, wb_ref[...])
    kv = _bdot(ckvn, wkv_ref[...])
    kpe = (ka_ref[...].astype(F32) * cos + kb_ref[...].astype(F32) * sin).astype(BF16)
    for h in range(N_HEADS):
        a = h * QK_PAD_DIM
        r = h * V7X_LANES
        q_ref[0, h, :, 0:128] = (qa[:, a:a + 128] * scale).astype(BF16)
        q_ref[0, h, :, 128:256] = ((qa[:, a + 128:a + 256] * cos + qb[:, r:r + 128] * sin) * scale).astype(BF16)
        k_ref[0, h, :, 0:128] = kv[:, a:a + 128].astype(BF16)
        k_ref[0, h, :, 128:256] = kpe
        v_ref[0, h] = kv[:, a + 128:a + 256].astype(BF16)


def _qkv(proj, cos_t, sin_t, gq, gkv, wa, wb, wkv, batch, seq):
    tm = 512
    ns = seq // tm
    full = lambda a: pl.BlockSpec(a.shape, lambda b, i: (0,) * a.ndim)
    col = lambda width, off: pl.BlockSpec((tm, width), lambda b, i: (b * ns + i, off // width))
    hd = lambda w: pl.BlockSpec((1, N_HEADS, tm, w), lambda b, i: (b, 0, i, 0))
    return pl.pallas_call(
        _qkv_body,
        grid=(batch, ns),
        in_specs=[col(Q_LORA_RANK, COL_CQ), col(KV_LORA_RANK, COL_CKV),
                  col(V7X_LANES, COL_KPE), col(V7X_LANES, COL_KPE_ROT),
                  pl.BlockSpec((tm, V7X_LANES), lambda b, i: (i, 0)),
                  pl.BlockSpec((tm, V7X_LANES), lambda b, i: (i, 0)),
                  full(gq), full(gkv), full(wa), full(wb), full(wkv)],
        out_specs=[hd(QK_PAD_DIM), hd(QK_PAD_DIM), hd(V_HEAD_DIM)],
        out_shape=[jax.ShapeDtypeStruct((batch, N_HEADS, seq, QK_PAD_DIM), BF16),
                   jax.ShapeDtypeStruct((batch, N_HEADS, seq, QK_PAD_DIM), BF16),
                   jax.ShapeDtypeStruct((batch, N_HEADS, seq, V_HEAD_DIM), BF16)],
        compiler_params=_params(("parallel", "parallel"), 48 << 20),
        name="qkv",
    )(proj, proj, proj, proj, cos_t, sin_t, gq, gkv, wa, wb, wkv)


def _attn_body(q_ref, k_ref, v_ref, o_ref, *, tq):
    k = k_ref[0, 0]
    v = v_ref[0, 0]
    for c in range(q_ref.shape[2] // tq):
        rows = slice(c * tq, (c + 1) * tq)
        s = lax.dot_general(q_ref[0, 0, rows, :], k, (((1,), (1,)), ((), ())), preferred_element_type=F32)
        m = jnp.max(s, axis=-1, keepdims=True)
        e = jnp.exp(s - m)
        l = jnp.sum(e, axis=-1, keepdims=True)
        o = _bdot(e.astype(BF16), v)
        o_ref[0, rows, :] = (o / l).astype(o_ref.dtype)


def _attn(q, k, v):
    batch, heads, seq, _ = q.shape
    tq = 256
    hspec =lambda w: pl.BlockSpec((1, 1, seq, w), lambda b, h: (b, h, 0, 0))
    return pl.pallas_call(
        functools.partial(_attn_body, tq=tq),
        grid=(batch, heads),
        in_specs=[hspec(QK_PAD_DIM), hspec(QK_PAD_DIM), hspec(V_HEAD_DIM)],
        out_specs=pl.BlockSpec((1, seq, V_HEAD_DIM), lambda b, h: (b, 0, h)),
        out_shape=jax.ShapeDtypeStruct((batch, seq, heads * V_HEAD_DIM), BF16),
        compiler_params=_params(("parallel", "parallel"), 48 << 20),
        name="attn",
    )(q, k, v)


def _outproj_body(yh_ref, ya_ref, gh_ref, ga_ref, w_ref, x_ref, gpm_ref, gpf_ref, h_ref, hn_ref):
    a = _rms(yh_ref[...].astype(F32), gh_ref[...]).astype(BF16)
    b = _rms(ya_ref[...].astype(F32), ga_ref[...]).astype(BF16)
    mixed = _bdot(a, w_ref[0:HYENA_WIDTH, :]) + _bdot(b, w_ref[HYENA_WIDTH:, :])
    h = x_ref[...] + _rms(mixed, gpm_ref[...])
    h_ref[...] = h
    hn_ref[...] = _rms(h, gpf_ref[...]).astype(BF16)


def _outproj(yh, ya, gh, ga, w_out, x2d, gpm, gpf):
    m, d = x2d.shape
    tm = 512
    full = lambda a: pl.BlockSpec(a.shape, lambda i: (0,) * a.ndim)
    rows = lambda w: pl.BlockSpec((tm, w), lambda i: (i, 0))
    vmem = 2 * 2 * tm * HYENA_WIDTH * 2 + 2 * d * d * 2 + 2 * tm * d * (4 + 4 + 2) + 4 * tm * d * 4 + (6 << 20)
    return pl.pallas_call(
        _outproj_body,
        grid=(m // tm,),
        in_specs=[rows(HYENA_WIDTH), rows(ATTN_WIDTH), full(gh), full(ga), full(w_out), rows(d),
                  full(gpm), full(gpf)],
        out_specs=[rows(d), rows(d)],
        out_shape=[jax.ShapeDtypeStruct((m, d), F32), jax.ShapeDtypeStruct((m, d), BF16)],
        compiler_params=_params(("parallel",), vmem),
        name="outproj",
    )(yh, ya, gh, ga, w_out, x2d, gpm, gpf)


def _ffn_up_body(hn_ref, wg_ref, wu_ref, cw_ref, cb_ref, o_ref):
    hn = hn_ref[...]
    g = _seq_conv3(_bdot(hn, wg_ref[...]), cw_ref[...], cb_ref[...])
    u = _bdot(hn, wu_ref[...])
    gelu = 0.5 * g * (1.0 + jnp.tanh(math.sqrt(2.0 / math.pi) * (g + 0.044715 * (g * g * g))))
    o_ref[...] = (gelu * u).astype(o_ref.dtype)


def _ffn_up(hn, w_up, conv_w, conv_b, batch, seq):
    d = hn.shape[1]
    tn = 256
    nb = D_FF // tn
    vmem = 2 * seq * d * 2 + 2 * 2 * d * tn * 2 + 2 * seq * tn * 2 + 8 * seq * tn * 4 + (6 << 20)
    return pl.pallas_call(
        _ffn_up_body,
        grid=(batch, nb),
        in_specs=[pl.BlockSpec((seq, d), lambda b, j: (b, 0)),
                  pl.BlockSpec((d, tn), lambda b, j: (0, j)),
                  pl.BlockSpec((d, tn), lambda b, j: (0, j + nb)),
                  pl.BlockSpec((CONV_WIDTH, tn), lambda b, j: (0, j)),
                  pl.BlockSpec((1, tn), lambda b, j: (0, j))],
        out_specs=pl.BlockSpec((seq, tn), lambda b, j: (b, j)),
        out_shape=jax.ShapeDtypeStruct((batch * seq, D_FF), BF16),
        compiler_params=_params(("parallel", "arbitrary"), vmem),
        name="ffn_up",
    )(hn, w_up, w_up, conv_w, conv_b)


def _ffn_down_body(a_ref, w_ref, h_ref, g_ref, o_ref, acc_ref):
    k = pl.program_id(1)

    @pl.when(k == 0)
    def _():
        acc_ref[...] = jnp.zeros_like(acc_ref)

    acc_ref[...] += _bdot(a_ref[...], w_ref[...])

    @pl.when(k == pl.num_programs(1) - 1)
    def _():
        o_ref[...] = h_ref[...] + _rms(acc_ref[...], g_ref[...])


def _ffn_down(act, w_down, h, gain):
    m, d = h.shape
    tm, tk = 1024, 512
    vmem = 2 * tm * tk * 2 + 2 * tk * d * 2 + 2 * 2 * tm * d * 4 + 2 * tm * d * 4 + (4 << 20)
    return pl.pallas_call(
        _ffn_down_body,
        grid=(m // tm, D_FF // tk),
        in_specs=[pl.BlockSpec((tm, tk), lambda i, k: (i, k)),
                  pl.BlockSpec((tk, d), lambda i, k: (k, 0)),
                  pl.BlockSpec((tm, d), lambda i, k: (i, 0)),
                  pl.BlockSpec((1, d), lambda i, k: (0, 0))],
        out_specs=pl.BlockSpec((tm, d), lambda i, k: (i, 0)),
        out_shape=jax.ShapeDtypeStruct((m, d), F32),
        scratch_shapes=[pltpu.VMEM((tm, d), F32)],
        compiler_params=_params(("parallel", "arbitrary"), vmem),
        name="ffn_down",
    )(act, w_down, h, gain)


def _rotate_half_cols(w):
    half = w.shape[1] // 2
    return jnp.concatenate([-w[:, half:], w[:, :half]], axis=1)


def _position_features(l):
    t = jnp.linspace(0.0, 1.0, l, dtype=F32)[:, None]
    bands = (FILTER_EMB_DIM - 1) // 2
    w = 2.0 * math.pi * jnp.arange(l, dtype=F32) / l
    f = jnp.linspace(1e-4, bands - 1, bands, dtype=F32)
    ang = w[:, None] * f[None, :]
    return t, jnp.concatenate([t, jnp.cos(ang), -jnp.sin(ang)], axis=-1)


def _layer(h, l, seq, prm):
    (pre_mix_gain, w_in, hyena_conv_w, hyena_conv_b, filt_w1, filt_b1, filt_freq1, filt_w2, filt_b2,
     filt_freq2, filt_w3, hyena_bias, q_norm_gain, w_uq, kv_norm_gain, w_ukv, hyena_out_gain, attn_out_gain,
     w_out, post_mix_gain, pre_ffn_gain, w_up, ffn_conv_w, ffn_conv_b, w_down, post_ffn_gain) = prm
    batch = h.shape[0]
    p = seq // 2
    x2d = h.reshape(batch * seq, D_MODEL)
    row = lambda a: a[l][None, :].astype(F32)

    w = w_in[l]
    w_kpe = w[:, COL_KPE:COL_KPE + QK_ROPE_DIM]
    zpad = jnp.zeros((D_MODEL, V7X_LANES - QK_ROPE_DIM), w.dtype)
    w_tail = jnp.concatenate([w[:, COL_CQ:], zpad, _rotate_half_cols(w_kpe), zpad], axis=1).astype(BF16)
    assert w_tail.shape[1] == PROJ_WIDTH - COL_CQ
    proj = _inproj(x2d, row(pre_mix_gain), w[:, :COL_CQ].astype(BF16), w_tail)

    cmat, sfwd, sinv = _dft_mats(p)
    t, z = _position_features(seq)
    zp = jnp.pad(z, ((0, 0), (0, V7X_LANES - FILTER_EMB_DIM)))
    w1p = jnp.pad(filt_w1[l], ((0, V7X_LANES - FILTER_EMB_DIM), (0, 0)))
    max_decay = math.log(DECAY_TARGET) / FAST_DECAY_PCT
    min_decay = math.log(DECAY_TARGET) / SLOW_DECAY_PCT
    deltas = jnp.abs(jnp.linspace(min_decay, max_decay, HYENA_WIDTH, dtype=F32))[None, :]
    kspec = _filter_spectra(zp, t, w1p, row(filt_b1), row(filt_freq1), filt_w2[l], row(filt_b2),
                            row(filt_freq2), filt_w3[l], deltas, cmat, sfwd, p)
    conv_w = hyena_conv_w[l].reshape(CONV_WIDTH, 3, HYENA_WIDTH).transpose(1, 0, 2)
    conv_b = hyena_conv_b[l].reshape(3, 1, HYENA_WIDTH)
    y_hyena = _hyena(proj, conv_w, conv_b, kspec, row(hyena_bias), cmat, sfwd, sinv, batch, seq, p)

    pos = jnp.arange(seq, dtype=F32)
    inv_freq = 1.0 / (ROPE_THETA ** (jnp.arange(0, QK_ROPE_DIM, 2, dtype=F32) / QK_ROPE_DIM))
    ang = pos[:, None] * inv_freq[None, :]
    ang = jnp.concatenate([ang, ang], axis=-1)
    lpad = ((0, 0), (0, V7X_LANES - QK_ROPE_DIM))
    cos_t = jnp.pad(jnp.cos(ang), lpad)
    sin_t = jnp.pad(jnp.sin(ang), lpad)
    dqk = QK_NOPE_DIM + QK_ROPE_DIM
    wq = w_uq[l].reshape(Q_LORA_RANK, N_HEADS, dqk)
    wa = jnp.pad(wq, ((0, 0), (0, 0), (0, QK_PAD_DIM - dqk))).reshape(Q_LORA_RANK, N_HEADS * QK_PAD_DIM)
    wq_pe = wq[:, :, QK_NOPE_DIM:]
    wq_rot = jnp.concatenate([-wq_pe[..., QK_ROPE_DIM // 2:], wq_pe[..., :QK_ROPE_DIM // 2]], axis=-1)
    wb = jnp.pad(wq_rot, ((0, 0), (0, 0), (0, V7X_LANES - QK_ROPE_DIM))).reshape(Q_LORA_RANK, N_HEADS * V7X_LANES)
    q, k, v = _qkv(proj, cos_t, sin_t, row(q_norm_gain), row(kv_norm_gain), wa.astype(BF16), wb.astype(BF16),
                   w_ukv[l].astype(BF16), batch, seq)
    y_attn = _attn(q, k, v).reshape(batch * seq, ATTN_WIDTH)

    h2d, hn = _outproj(y_hyena, y_attn, row(hyena_out_gain), row(attn_out_gain), w_out[l].astype(BF16), x2d,
                       row(post_mix_gain), row(pre_ffn_gain))

    act = _ffn_up(hn, w_up[l].astype(BF16), ffn_conv_w[l], row(ffn_conv_b), batch, seq)
    out = _ffn_down(act, w_down[l].astype(BF16), h2d, row(post_ffn_gain))
    return out.reshape(batch, seq, D_MODEL)


def kernel(x, pre_mix_gain, w_in, hyena_conv_w, hyena_conv_b, filt_w1, filt_b1, filt_freq1, filt_w2, filt_b2,
           filt_freq2, filt_w3, hyena_bias, q_norm_gain, w_uq, kv_norm_gain, w_ukv, hyena_out_gain,
           attn_out_gain, w_out, post_mix_gain, pre_ffn_gain, w_up, ffn_conv_w, ffn_conv_b, w_down,
           post_ffn_gain):
    prm = (pre_mix_gain, w_in, hyena_conv_w, hyena_conv_b, filt_w1, filt_b1, filt_freq1, filt_w2, filt_b2,
           filt_freq2, filt_w3, hyena_bias, q_norm_gain, w_uq, kv_norm_gain, w_ukv, hyena_out_gain,
           attn_out_gain, w_out, post_mix_gain, pre_ffn_gain, w_up, ffn_conv_w, ffn_conv_b, w_down,
           post_ffn_gain)
    seq = x.shape[1]
    h = x
    for l in range(w_in.shape[0]):
        h = _layer(h, l, seq, prm)
    return h
```

```python
import functools
import math

import jax
import jax.numpy as jnp
from jax import lax
from jax.experimental import pallas as pl
from jax.experimental.pallas import tpu as pltpu

F32 = jnp.float32
BF16 = jnp.bfloat16

D_MODEL = 2048
HYENA_WIDTH = 1024
CONV_WIDTH = 3
FILTER_EMB_DIM = 33
FILTER_HIDDEN = 64
DECAY_TARGET = 1e-2
FAST_DECAY_PCT = 0.3
SLOW_DECAY_PCT = 1.5
DECAY_SHIFT = 0.05
N_HEADS = 8
QK_NOPE_DIM = 128
QK_ROPE_DIM = 64
V_HEAD_DIM = 128
Q_LORA_RANK = 512
KV_LORA_RANK = 256
ROPE_THETA = 10000.0
ATTN_WIDTH = N_HEADS * V_HEAD_DIM
D_FF = 5632
NORM_EPS = 1e-6

V7X_VMEM_BYTES = 64 * 1024 * 1024
V7X_LANES = 128
V7X_MXU_DIM = 256
V7X_SUBLANES = 8
HALO = V7X_SUBLANES

QK_PAD_DIM = V7X_MXU_DIM
PROJ_WIDTH = 4096
COL_CQ = 3 * HYENA_WIDTH
COL_CKV = COL_CQ + Q_LORA_RANK
COL_KPE = COL_CKV + KV_LORA_RANK


def _params(semantics, vmem_bytes, flags=None):
    return pltpu.CompilerParams(dimension_semantics=semantics, flags=flags,
                                vmem_limit_bytes=min(int(vmem_bytes), V7X_VMEM_BYTES - (4 << 20)))


def _rms(x, gain):
    return x * lax.rsqrt(jnp.mean(x * x, axis=-1, keepdims=True) + NORM_EPS) * gain


def _bdot(a, b):
    return jnp.dot(a, b, preferred_element_type=F32)


def _seq_conv3(x, w, b):
    t = x.shape[0]
    row = lax.broadcasted_iota(jnp.int32, x.shape, 0)
    prev = jnp.where(row == 0, 0.0, pltpu.roll(x, 1, 0))
    nxt = jnp.where(row == t - 1, 0.0, pltpu.roll(x, t - 1, 0))
    return prev * w[0:1] + x * w[1:2] + nxt * w[2:3] + b


def _inproj_body(x_ref, g_ref, wm_ref, wt_ref, o_ref, xn_ref, *, n_main):
    j = pl.program_id(1)

    @pl.when(j == 0)
    def _():
        xn_ref[...] = _rms(x_ref[...], g_ref[...]).astype(BF16)

    @pl.when(j < n_main)
    def _():
        o_ref[...] = _bdot(xn_ref[...], wm_ref[...]).astype(o_ref.dtype)

    @pl.when(j == n_main)
    def _():
        o_ref[...] = _bdot(xn_ref[...], wt_ref[...]).astype(o_ref.dtype)


def _inproj(x2d, gain, w_main, w_tail):
    m, d = x2d.shape
    tm, tn = 1024, w_tail.shape[1]
    n_main = w_main.shape[1] // tn
    vmem = 2 * tm * d * 4 + tm * d * 2 + 4 * d * tn * 2 + 2 * tm * tn * 2 + tm * tn * 4 + (8 << 20)
    return pl.pallas_call(
        functools.partial(_inproj_body, n_main=n_main),
        grid=(m // tm, n_main + 1),
        in_specs=[pl.BlockSpec((tm, d), lambda i, j: (i, 0)),
                  pl.BlockSpec((1, d), lambda i, j: (0, 0)),
                  pl.BlockSpec((d, tn), lambda i, j: (0, jnp.minimum(j, n_main - 1))),
                  pl.BlockSpec((d, tn), lambda i, j: (0, 0))],
        out_specs=pl.BlockSpec((tm, tn), lambda i, j: (i, j)),
        out_shape=jax.ShapeDtypeStruct((m, (n_main + 1) * tn), BF16),
        scratch_shapes=[pltpu.VMEM((tm, d), BF16)],
        compiler_params=_params(("parallel", "arbitrary"), vmem),
        name="inproj",
    )(x2d, gain, w_main, w_tail)


def _dft_mats(p):
    r = 32
    assert p == r * r
    idx = jnp.arange(p, dtype=jnp.int32)
    sub = jnp.arange(r, dtype=jnp.int32)
    ang_hi = ((idx[:, None] * (r * sub)[None, :]) % (2 * p)).astype(F32) * (math.pi / p)
    ang_lo = ((idx[:, None] * sub[None, :]) % (2 * p)).astype(F32) * (math.pi / p)
    ch, sh = jnp.cos(ang_hi)[:, :, None], jnp.sin(ang_hi)[:, :, None]
    cl, sl = jnp.cos(ang_lo)[:, None, :], jnp.sin(ang_lo)[:, None, :]
    c = (ch * cl - sh * sl).reshape(p, p)
    s = -(sh * cl + ch * sl).reshape(p, p)
    alt = jnp.where(idx % 2 == 0, 1.0, -1.0).astype(F32)
    s_fwd = jnp.where(idx[:, None] == 0, alt[None, :], s)
    s_inv = jnp.where(idx[None, :] == 0, alt[:, None], s)
    return c.astype(BF16), s_fwd.astype(BF16), s_inv.astype(BF16)


def _filter_body(z_ref, t_ref, w1_ref, b1_ref, f1_ref, w2_ref, b2_ref, f2_ref, w3f_ref, w3b_ref,
                 dl_ref, c_ref, s_ref, o_ref, hid_ref, *, p):
    hp = lax.Precision.HIGHEST

    @pl.when(pl.program_id(0) == 0)
    def _():
        h1 = jnp.sin(f1_ref[...] * (jnp.dot(z_ref[...], w1_ref[...], precision=hp,
                                            preferred_element_type=F32) + b1_ref[...]))
        hid_ref[...] = jnp.sin(f2_ref[...] * (jnp.dot(h1, w2_ref[...], precision=hp,
                                                      preferred_element_type=F32) + b2_ref[...]))

    h = hid_ref[...]
    win =jnp.exp(-t_ref[...] * dl_ref[...]) + DECAY_SHIFT
    hf = jnp.dot(h, w3f_ref[...], precision=hp, preferred_element_type=F32) * win
    hb = jnp.dot(h, w3b_ref[...], precision=hp, preferred_element_type=F32) * win
    lrow = lax.broadcasted_iota(jnp.int32, hb.shape, 0)
    hb = jnp.where(lrow == 0, 0.0, hb)

    cmat = c_ref[...]
    smat = s_ref[...]

    def fwd(x):
        hi = x.astype(BF16)
        lo = (x - hi.astype(F32)).astype(BF16)
        return (_bdot(cmat, hi) + _bdot(cmat, lo), _bdot(smat, hi) + _bdot(smat, lo))

    row = lax.broadcasted_iota(jnp.int32, (p, hf.shape[1]), 0)
    row0 = row == 0
    sigma = jnp.where(row % 2 == 1, -1.0, 1.0)

    def conj(a):
        return a[0], jnp.where(row0, a[1], -a[1])

    af0, af1 = fwd(hf[:p]), fwd(hf[p:])
    ab0, ab1 = fwd(hb[:p]), fwd(hb[p:])
    cb0 = conj(ab0)
    k0 = (af0[0] + cb0[0], af0[1] + cb0[1])
    k1 = (af1[0] + sigma * af0[0], af1[1] + sigma * af0[1])
    km1 = conj((ab1[0] + sigma * ab0[0], ab1[1] + sigma * ab0[1]))
    scale = jnp.where(row0, 0.5 / p, 1.0 / p)
    for i, a in enumerate((k0, k1, km1)):
        o_ref[2 * i] = a[0] * scale
        o_ref[2 * i + 1] = a[1] * scale


def _filter_spectra(z, t, w1, b1, f1, w2, b2, f2, w3, deltas, cmat, smat, p):
    l = z.shape[0]
    c = HYENA_WIDTH
    tc = 256
    nb = c // tc
    full = lambda a: pl.BlockSpec(a.shape, lambda j: (0,) * a.ndim)
    return pl.pallas_call(
        functools.partial(_filter_body, p=p),
        grid=(nb,),
        in_specs=[full(z), full(t), full(w1), full(b1), full(f1), full(w2), full(b2), full(f2),
                  pl.BlockSpec((FILTER_HIDDEN, tc), lambda j: (0, j)),
                  pl.BlockSpec((FILTER_HIDDEN, tc), lambda j: (0, j + nb)),
                  pl.BlockSpec((1, tc), lambda j: (0, j)),
                  full(cmat), full(smat)],
        out_specs=pl.BlockSpec((6, p, tc), lambda j: (0, 0, j)),
        out_shape=jax.ShapeDtypeStruct((6, p, c), F32),
        scratch_shapes=[pltpu.VMEM((l, FILTER_HIDDEN), F32)],
        compiler_params=_params(("arbitrary",), 40 << 20),
        name="filt",
    )(z, t, w1, b1, f1, w2, b2, f2, w3, w3, deltas, cmat, smat)


def _hyena_body(x0_ref, x1_ref, v_ref, cw_ref, cb_ref, ks_ref, hb_ref, c_ref, sf_ref, si_ref, o_ref, *, p):
    cmat = c_ref[...]
    sfwd = sf_ref[...]
    sinv = si_ref[...]
    seq = 2 * p
    row0 = lax.broadcasted_iota(jnp.int32, (p, o_ref.shape[1]), 0) == 0

    def cmul(i, uu):
        kre, kim = ks_ref[2 * i], ks_ref[2 * i + 1]
        ii = kim * uu[1]
        return (kre * uu[0] - jnp.where(row0, 0.0, ii),
                jnp.where(row0, ii, kre * uu[1] + kim * uu[0]))

    def inv(a, b):
        return _bdot(cmat, (a[0] + b[0]).astype(BF16)) + _bdot(sinv, (a[1] + b[1]).astype(BF16))

    for r0 in range(0, o_ref.shape[0], seq):
        rs = slice(r0, r0 + seq)
        x0 = _seq_conv3(x0_ref[rs, :].astype(F32), cw_ref[0], cb_ref[0])
        x1 = _seq_conv3(x1_ref[rs, :].astype(F32), cw_ref[1], cb_ref[1])
        v = _seq_conv3(v_ref[rs, :].astype(F32), cw_ref[2], cb_ref[2])
        u = x1 * v
        ub = u.astype(BF16)
        spec = [(_bdot(cmat, ub[j * p:(j + 1) * p]), _bdot(sfwd, ub[j * p:(j + 1) * p])) for j in range(2)]
        y_lo = inv(cmul(0, spec[0]), cmul(2, spec[1]))
        y_hi = inv(cmul(1, spec[0]), cmul(0, spec[1]))
        y = jnp.concatenate([y_lo, y_hi], axis=0)
        o_ref[rs, :] = (x0 * (y + u * hb_ref[...])).astype(o_ref.dtype)


def _hyena(proj, conv_w, conv_b, kspec, hbias, cmat, sfwd, sinv, batch, seq, p):
    c = HYENA_WIDTH
    tc = 256
    nb = c // tc
    nseq = 1
    xspec = lambda off: pl.BlockSpec((nseq * seq, tc), lambda j, b: (b, j + off * nb))
    full = lambda a: pl.BlockSpec(a.shape, lambda j, b: (0,) * a.ndim)
    vmem = (nseq * 6 * seq * tc * 2 + 2 * 6 * p * tc * 4 + 6 * p * p * 2 + nseq * 2 * seq * tc * 2
            + nseq * 6 * seq * tc * 4 + (6 << 20))
    return pl.pallas_call(
        functools.partial(_hyena_body, p=p),
        grid=(nb, batch // nseq),
        in_specs=[xspec(0), xspec(1), xspec(2),
                  pl.BlockSpec((3, CONV_WIDTH, tc), lambda j, b: (0, 0, j)),
                  pl.BlockSpec((3, 1, tc), lambda j, b: (0, 0, j)),
                  pl.BlockSpec((6, p, tc), lambda j, b: (0, 0, j)),
                  pl.BlockSpec((1, tc), lambda j, b: (0, j)),
                  full(cmat), full(sfwd), full(sinv)],
        out_specs=pl.BlockSpec((nseq * seq, tc), lambda j, b: (b, j)),
        out_shape=jax.ShapeDtypeStruct((batch * seq, c), BF16),
        compiler_params=_params(("parallel", "parallel"), vmem),
        name="hyena",
    )(proj, proj, proj, conv_w, conv_b, kspec, hbias, cmat, sfwd, sinv)


def _qkv_body(cq_ref, ckv_ref, kpe_ref, rot_ref, cos_ref, sin_ref, gq_ref, gkv_ref, wa_ref, wb_ref, wkv_ref,
              q_ref, k_ref, v_ref, *, rows):
    scale = (QK_NOPE_DIM + QK_ROPE_DIM) ** -0.5 * math.log2(math.e)
    for r0 in range(0, cq_ref.shape[0], rows):
        rs = slice(r0, r0 + rows)
        cqn = _rms(cq_ref[rs, :].astype(F32), gq_ref[...]).astype(BF16)
        ckvn = _rms(ckv_ref[rs, :].astype(F32), gkv_ref[...]).astype(BF16)
        cos = cos_ref[rs, :]
        sin = sin_ref[rs, :]
        qa = _bdot(cqn, wa_ref[...])
        qb = _bdot(cqn, wb_ref[...])
        kv = _bdot(ckvn, wkv_ref[...])
        kpe = kpe_ref[rs, :]
        kpe = (kpe.astype(F32) * cos + _bdot(kpe, rot_ref[...]) * sin).astype(BF16)
        for h in range(N_HEADS):
            a = h * QK_PAD_DIM
            r = h * V7X_LANES
            q_ref[0, h, rs, 0:128] = (qa[:, a:a + 128] * scale).astype(BF16)
            q_ref[0, h, rs, 128:256] = ((qa[:, a + 128:a + 256] * cos + qb[:, r:r + 128] * sin) * scale).astype(BF16)
            k_ref[0, h, rs, 0:128] = kv[:, a:a + 128].astype(BF16)
            k_ref[0, h, rs, 128:256] = kpe
            v_ref[0, h, rs, :] = kv[:, a + 128:a + 256].astype(BF16)


def _qkv(proj, rot, cos_t, sin_t, gq, gkv, wa, wb, wkv, batch, seq):
    tm = 512
    ns = seq // tm
    full = lambda a: pl.BlockSpec(a.shape, lambda b, i: (0,) * a.ndim)
    col = lambda width, off: pl.BlockSpec((tm, width), lambda b, i: (b * ns + i, off // width))
    hd = lambda w: pl.BlockSpec((1, N_HEADS, tm, w), lambda b, i: (b, 0, i, 0))
    return pl.pallas_call(
        functools.partial(_qkv_body, rows=tm // 2),
        grid=(batch, ns),
        in_specs=[col(Q_LORA_RANK, COL_CQ), col(KV_LORA_RANK, COL_CKV),
                  col(V7X_LANES, COL_KPE), full(rot),
                  pl.BlockSpec((tm, V7X_LANES), lambda b, i: (i, 0)),
                  pl.BlockSpec((tm, V7X_LANES), lambda b, i: (i, 0)),
                  full(gq), full(gkv), full(wa), full(wb), full(wkv)],
        out_specs=[hd(QK_PAD_DIM), hd(QK_PAD_DIM), hd(V_HEAD_DIM)],
        out_shape=[jax.ShapeDtypeStruct((batch, N_HEADS, seq, QK_PAD_DIM), BF16),
                   jax.ShapeDtypeStruct((batch, N_HEADS, seq, QK_PAD_DIM), BF16),
                   jax.ShapeDtypeStruct((batch, N_HEADS, seq, V_HEAD_DIM), BF16)],
        compiler_params=_params(("parallel", "parallel"), 48 << 20),
        name="qkv",
    )(proj, proj, proj, rot, cos_t, sin_t, gq, gkv, wa, wb, wkv)


def _attn_body(q_ref, k_ref, v_ref, o_ref, *, tq):
    k = k_ref[0, 0]
    v = v_ref[0, 0]
    for c in range(q_ref.shape[2] // tq):
        rows = slice(c * tq, (c + 1) * tq)
        s = lax.dot_general(q_ref[0, 0, rows, :], k, (((1,), (1,)), ((), ())), preferred_element_type=F32)
        m = jnp.max(s, axis=-1, keepdims=True)
        e = jnp.exp2(s - m)
        l = jnp.sum(e, axis=-1, keepdims=True)
        o = _bdot(e.astype(BF16), v)
        o_ref[0, rows, :] = (o / l).astype(o_ref.dtype)


def _attn(q, k, v):
    batch, heads, seq, _ = q.shape
    tq = 256
    hspec =lambda w: pl.BlockSpec((1, 1, seq, w), lambda b, h: (b, h, 0, 0))
    return pl.pallas_call(
        functools.partial(_attn_body, tq=tq),
        grid=(batch, heads),
        in_specs=[hspec(QK_PAD_DIM), hspec(QK_PAD_DIM), hspec(V_HEAD_DIM)],
        out_specs=pl.BlockSpec((1, seq, V_HEAD_DIM), lambda b, h: (b, 0, h)),
        out_shape=jax.ShapeDtypeStruct((batch, seq, heads * V_HEAD_DIM), BF16),
        compiler_params=_params(("parallel", "parallel"), 48 << 20),
        name="attn",
    )(q, k, v)


def _outproj_body(yh_ref, ya_ref, gh_ref, ga_ref, w_ref, x_ref, gpm_ref, gpf_ref, h_ref, hn_ref, *, rows):
    for r0 in range(0, x_ref.shape[0], rows):
        rs = slice(r0, r0 + rows)
        a = _rms(yh_ref[rs, :].astype(F32), gh_ref[...]).astype(BF16)
        b = _rms(ya_ref[rs, :].astype(F32), ga_ref[...]).astype(BF16)
        mixed = _bdot(a, w_ref[0:HYENA_WIDTH, :]) + _bdot(b, w_ref[HYENA_WIDTH:, :])
        h = x_ref[rs, :] + _rms(mixed, gpm_ref[...])
        h_ref[rs, :] = h
        hn_ref[rs, :] = _rms(h, gpf_ref[...]).astype(BF16)


def _outproj(yh, ya, gh, ga, w_out, x2d, gpm, gpf):
    m, d = x2d.shape
    tm = 512
    full = lambda a: pl.BlockSpec(a.shape, lambda i: (0,) * a.ndim)
    rows = lambda w: pl.BlockSpec((tm, w), lambda i: (i, 0))
    vmem = 2 * 2 * tm * HYENA_WIDTH * 2 + 2 * d * d * 2 + 2 * tm * d * (4 + 4 + 2) + 4 * tm * d * 4 + (6 << 20)
    return pl.pallas_call(
        functools.partial(_outproj_body, rows=tm // 2),
        grid=(m // tm,),
        in_specs=[rows(HYENA_WIDTH), rows(ATTN_WIDTH), full(gh), full(ga), full(w_out), rows(d),
                  full(gpm), full(gpf)],
        out_specs=[rows(d), rows(d)],
        out_shape=[jax.ShapeDtypeStruct((m, d), F32), jax.ShapeDtypeStruct((m, d), BF16)],
        compiler_params=_params(("parallel",), vmem),
        name="outproj",
    )(yh, ya, gh, ga, w_out, x2d, gpm, gpf)


def _ffn_up_body(hn_ref, wg_ref, wu_ref, cw_ref, cb_ref, o_ref):
    hn = hn_ref[...]
    g = _seq_conv3(_bdot(hn, wg_ref[...]), cw_ref[...], cb_ref[...])
    u = _bdot(hn, wu_ref[...])
    gelu = 0.5 * g * (1.0 + jnp.tanh(math.sqrt(2.0 / math.pi) * (g + 0.044715 * (g * g * g))))
    o_ref[...] = (gelu * u).astype(o_ref.dtype)


def _ffn_up(hn, w_up, conv_w, conv_b, batch, seq):
    d = hn.shape[1]
    tn = 256
    nb = D_FF // tn
    vmem = 2 * seq * d * 2 + 2 * 2 * d * tn * 2 + 2 * seq * tn * 2 + 8 * seq * tn * 4 + (6 << 20)
    return pl.pallas_call(
        _ffn_up_body,
        grid=(batch, nb),
        in_specs=[pl.BlockSpec((seq, d), lambda b, j: (b, 0)),
                  pl.BlockSpec((d, tn), lambda b, j: (0, j)),
                  pl.BlockSpec((d, tn), lambda b, j: (0, j + nb)),
                  pl.BlockSpec((CONV_WIDTH, tn), lambda b, j: (0, j)),
                  pl.BlockSpec((1, tn), lambda b, j: (0, j))],
        out_specs=pl.BlockSpec((seq, tn), lambda b, j: (b, j)),
        out_shape=jax.ShapeDtypeStruct((batch * seq, D_FF), BF16),
        compiler_params=_params(("parallel", "arbitrary"), vmem),
        name="ffn_up",
    )(hn, w_up, w_up, conv_w, conv_b)


def _ffn_down_body(a_ref, w_ref, h_ref, g_ref, o_ref, acc_ref):
    k = pl.program_id(1)

    @pl.when(k == 0)
    def _():
        acc_ref[...] = jnp.zeros_like(acc_ref)

    acc_ref[...] += _bdot(a_ref[...], w_ref[...])

    @pl.when(k == pl.num_programs(1) - 1)
    def _():
        o_ref[...] = h_ref[...] + _rms(acc_ref[...], g_ref[...])


def _ffn_down(act, w_down, h, gain):
    m, d = h.shape
    tm, tk = 1024, 512
    vmem = 2 * tm * tk * 2 + 2 * tk * d * 2 + 2 * 2 * tm * d * 4 + 2 * tm * d * 4 + (4 << 20)
    return pl.pallas_call(
        _ffn_down_body,
        grid=(m // tm, D_FF // tk),
        in_specs=[pl.BlockSpec((tm, tk), lambda i, k: (i, k)),
                  pl.BlockSpec((tk, d), lambda i, k: (k, 0)),
                  pl.BlockSpec((tm, d), lambda i, k: (i, 0)),
                  pl.BlockSpec((1, d), lambda i, k: (0, 0))],
        out_specs=pl.BlockSpec((tm, d), lambda i, k: (i, 0)),
        out_shape=jax.ShapeDtypeStruct((m, d), F32),
        scratch_shapes=[pltpu.VMEM((tm, d), F32)],
        compiler_params=_params(("parallel", "arbitrary"), vmem),
        name="ffn_down",
    )(act, w_down, h, gain)


def _position_features(l):
    t = jnp.linspace(0.0, 1.0, l, dtype=F32)[:, None]
    bands = (FILTER_EMB_DIM - 1) // 2
    w = 2.0 * math.pi * jnp.arange(l, dtype=F32) / l
    f = jnp.linspace(1e-4, bands - 1, bands, dtype=F32)
    ang = w[:, None] * f[None, :]
    return t, jnp.concatenate([t, jnp.cos(ang), -jnp.sin(ang)], axis=-1)


def _layer(h, l, seq, prm):
    (pre_mix_gain, w_in, hyena_conv_w, hyena_conv_b, filt_w1, filt_b1, filt_freq1, filt_w2, filt_b2,
     filt_freq2, filt_w3, hyena_bias, q_norm_gain, w_uq, kv_norm_gain, w_ukv, hyena_out_gain, attn_out_gain,
     w_out, post_mix_gain, pre_ffn_gain, w_up, ffn_conv_w, ffn_conv_b, w_down, post_ffn_gain) = prm
    batch = h.shape[0]
    p = seq // 2
    x2d = h.reshape(batch * seq, D_MODEL)
    row = lambda a: a[l][None, :].astype(F32)

    w = w_in[l]
    w_tail = jnp.pad(w[:, COL_CQ:].astype(BF16), ((0, 0), (0, PROJ_WIDTH - w.shape[1])))
    proj = _inproj(x2d, row(pre_mix_gain), w[:, :COL_CQ].astype(BF16), w_tail)

    cmat, sfwd, sinv = _dft_mats(p)
    t, z = _position_features(seq)
    zp = jnp.pad(z, ((0, 0), (0, V7X_LANES - FILTER_EMB_DIM)))
    w1p = jnp.pad(filt_w1[l], ((0, V7X_LANES - FILTER_EMB_DIM), (0, 0)))
    max_decay = math.log(DECAY_TARGET) / FAST_DECAY_PCT
    min_decay = math.log(DECAY_TARGET) / SLOW_DECAY_PCT
    deltas = jnp.abs(jnp.linspace(min_decay, max_decay, HYENA_WIDTH, dtype=F32))[None, :]
    kspec = _filter_spectra(zp, t, w1p, row(filt_b1), row(filt_freq1), filt_w2[l], row(filt_b2),
                            row(filt_freq2), filt_w3[l], deltas, cmat, sfwd, p)
    conv_w = hyena_conv_w[l].reshape(CONV_WIDTH, 3, HYENA_WIDTH).transpose(1, 0, 2)
    conv_b = hyena_conv_b[l].reshape(3, 1, HYENA_WIDTH)
    y_hyena = _hyena(proj, conv_w, conv_b, kspec, row(hyena_bias), cmat, sfwd, sinv, batch, seq, p)

    pos = jnp.arange(seq, dtype=F32)
    inv_freq = 1.0 / (ROPE_THETA ** (jnp.arange(0, QK_ROPE_DIM, 2, dtype=F32) / QK_ROPE_DIM))
    ang = pos[:, None] * inv_freq[None, :]
    ang = jnp.concatenate([ang, ang], axis=-1)
    lpad = ((0, 0), (0, V7X_LANES - QK_ROPE_DIM))
    cos_t = jnp.pad(jnp.cos(ang), lpad)
    sin_t = jnp.pad(jnp.sin(ang), lpad)
    dqk = QK_NOPE_DIM + QK_ROPE_DIM
    wq = w_uq[l].reshape(Q_LORA_RANK, N_HEADS, dqk)
    wa = jnp.pad(wq, ((0, 0), (0, 0), (0, QK_PAD_DIM - dqk))).reshape(Q_LORA_RANK, N_HEADS * QK_PAD_DIM)
    wq_pe = wq[:, :, QK_NOPE_DIM:]
    wq_rot = jnp.concatenate([-wq_pe[..., QK_ROPE_DIM // 2:], wq_pe[..., :QK_ROPE_DIM // 2]], axis=-1)
    wb = jnp.pad(wq_rot, ((0, 0), (0, 0), (0, V7X_LANES - QK_ROPE_DIM))).reshape(Q_LORA_RANK, N_HEADS * V7X_LANES)
    ri = jnp.arange(V7X_LANES, dtype=jnp.int32)[:, None]
    rj = jnp.arange(V7X_LANES, dtype=jnp.int32)[None, :]
    half = QK_ROPE_DIM // 2
    rot = (jnp.where((ri == rj + half) & (rj < half), -1.0, 0.0)
           + jnp.where((ri == rj - half) & (rj >= half) & (rj < QK_ROPE_DIM), 1.0, 0.0)).astype(BF16)
    q, k, v = _qkv(proj, rot, cos_t, sin_t, row(q_norm_gain), row(kv_norm_gain), wa.astype(BF16),
                   wb.astype(BF16), w_ukv[l].astype(BF16), batch, seq)
    y_attn = _attn(q, k, v).reshape(batch * seq, ATTN_WIDTH)

    h2d, hn = _outproj(y_hyena, y_attn, row(hyena_out_gain), row(attn_out_gain), w_out[l].astype(BF16), x2d,
                       row(post_mix_gain), row(pre_ffn_gain))

    act = _ffn_up(hn, w_up[l].astype(BF16), ffn_conv_w[l], row(ffn_conv_b), batch, seq)
    out = _ffn_down(act, w_down[l].astype(BF16), h2d, row(post_ffn_gain))
    return out.reshape(batch, seq, D_MODEL)


def kernel(x, pre_mix_gain, w_in, hyena_conv_w, hyena_conv_b, filt_w1, filt_b1, filt_freq1, filt_w2, filt_b2,
           filt_freq2, filt_w3, hyena_bias, q_norm_gain, w_uq, kv_norm_gain, w_ukv, hyena_out_gain,
           attn_out_gain, w_out, post_mix_gain, pre_ffn_gain, w_up, ffn_conv_w, ffn_conv_b, w_down,
           post_ffn_gain):
    prm = (pre_mix_gain, w_in, hyena_conv_w, hyena_conv_b, filt_w1, filt_b1, filt_freq1, filt_w2, filt_b2,
           filt_freq2, filt_w3, hyena_bias, q_norm_gain, w_uq, kv_norm_gain, w_ukv, hyena_out_gain,
           attn_out_gain, w_out, post_mix_gain, pre_ffn_gain, w_up, ffn_conv_w, ffn_conv_b, w_down,
           post_ffn_gain)
    seq = x.shape[1]
    h = x
    for l in range(w_in.shape[0]):
        h = _layer(h, l, seq, prm)
    return h
```

```python
import functools
import math

import jax
import jax.numpy as jnp
from jax import lax
from jax.experimental import pallas as pl
from jax.experimental.pallas import tpu as pltpu

F32 = jnp.float32
BF16 = jnp.bfloat16

D_MODEL = 2048
HYENA_WIDTH = 1024
CONV_WIDTH = 3
FILTER_EMB_DIM = 33
FILTER_HIDDEN = 64
DECAY_TARGET = 1e-2
FAST_DECAY_PCT = 0.3
SLOW_DECAY_PCT = 1.5
DECAY_SHIFT = 0.05
N_HEADS = 8
QK_NOPE_DIM = 128
QK_ROPE_DIM = 64
V_HEAD_DIM = 128
Q_LORA_RANK = 512
KV_LORA_RANK = 256
ROPE_THETA = 10000.0
ATTN_WIDTH = N_HEADS * V_HEAD_DIM
D_FF = 5632
NORM_EPS = 1e-6

V7X_VMEM_BYTES = 64 * 1024 * 1024
V7X_LANES = 128
V7X_MXU_DIM = 256
V7X_SUBLANES = 8
HALO = V7X_SUBLANES

QK_PAD_DIM = V7X_MXU_DIM
PROJ_WIDTH = 4096
COL_CQ = 3 * HYENA_WIDTH
COL_CKV = COL_CQ + Q_LORA_RANK
COL_KPE = COL_CKV + KV_LORA_RANK


def _params(semantics, vmem_bytes, flags=None):
    return pltpu.CompilerParams(dimension_semantics=semantics, flags=flags,
                                vmem_limit_bytes=min(int(vmem_bytes), V7X_VMEM_BYTES - (4 << 20)))


def _rms(x, gain):
    return x * lax.rsqrt(jnp.mean(x * x, axis=-1, keepdims=True) + NORM_EPS) * gain


def _bdot(a, b):
    return jnp.dot(a, b, preferred_element_type=F32)


def _seq_conv3(x, w, b):
    t = x.shape[0]
    row = lax.broadcasted_iota(jnp.int32, x.shape, 0)
    prev = jnp.where(row == 0, 0.0, pltpu.roll(x, 1, 0))
    nxt = jnp.where(row == t - 1, 0.0, pltpu.roll(x, t - 1, 0))
    return prev * w[0:1] + x * w[1:2] + nxt * w[2:3] + b


def _inproj_body(x_ref, g_ref, wm_ref, wt_ref, o_ref, xn_ref, *, n_main):
    j = pl.program_id(1)

    @pl.when(j == 0)
    def _():
        xn_ref[...] = _rms(x_ref[...], g_ref[...]).astype(BF16)

    def project(w_ref):
        o_ref[...] = lax.dot_general(xn_ref[...], w_ref[...], (((1,), (1,)), ((), ())),
                                     preferred_element_type=F32).astype(o_ref.dtype)

    @pl.when(j < n_main)
    def _():
        project(wm_ref)

    @pl.when(j == n_main)
    def _():
        project(wt_ref)


def _inproj(x2d, gain, w_main, w_tail):
    m, d = x2d.shape
    tm, tn = 1024, w_tail.shape[0]
    n_main = w_main.shape[0] // tn
    vmem = 2 * tm * d * 4 + tm * d * 2 + 4 * d * tn * 2 + 2 * tm * tn * 2 + tm * tn * 4 + (8 << 20)
    return pl.pallas_call(
        functools.partial(_inproj_body, n_main=n_main),
        grid=(m // tm, n_main + 1),
        in_specs=[pl.BlockSpec((tm, d), lambda i, j: (i, 0)),
                  pl.BlockSpec((1, d), lambda i, j: (0, 0)),
                  pl.BlockSpec((tn, d), lambda i, j: (jnp.minimum(j, n_main - 1), 0)),
                  pl.BlockSpec((tn, d), lambda i, j: (0, 0))],
        out_specs=pl.BlockSpec((tm, tn), lambda i, j: (i, j)),
        out_shape=jax.ShapeDtypeStruct((m, (n_main + 1) * tn), BF16),
        scratch_shapes=[pltpu.VMEM((tm, d), BF16)],
        compiler_params=_params(("parallel", "arbitrary"), vmem),
        name="inproj",
    )(x2d, gain, w_main, w_tail)


def _dft_mats(p):
    r = 32
    assert p == r * r
    idx = jnp.arange(p, dtype=jnp.int32)
    sub = jnp.arange(r, dtype=jnp.int32)
    ang_hi = ((idx[:, None] * (r * sub)[None, :]) % (2 * p)).astype(F32) * (math.pi / p)
    ang_lo = ((idx[:, None] * sub[None, :]) % (2 * p)).astype(F32) * (math.pi / p)
    ch, sh = jnp.cos(ang_hi)[:, :, None], jnp.sin(ang_hi)[:, :, None]
    cl, sl = jnp.cos(ang_lo)[:, None, :], jnp.sin(ang_lo)[:, None, :]
    c = (ch * cl - sh * sl).reshape(p, p)
    s = -(sh * cl + ch * sl).reshape(p, p)
    alt = jnp.where(idx % 2 == 0, 1.0, -1.0).astype(F32)
    s_fwd = jnp.where(idx[:, None] == 0, alt[None, :], s)
    s_inv = jnp.where(idx[None, :] == 0, alt[:, None], s)
    return c.astype(BF16), s_fwd.astype(BF16), s_inv.astype(BF16)


def _filter_body(z_ref, t_ref, w1_ref, b1_ref, f1_ref, w2_ref, b2_ref, f2_ref, w3f_ref, w3b_ref,
                 dl_ref, c_ref, s_ref, o_ref, hid_ref, *, p):
    hp = lax.Precision.HIGHEST

    @pl.when(pl.program_id(0) == 0)
    def _():
        h1 = jnp.sin(f1_ref[...] * (jnp.dot(z_ref[...], w1_ref[...], precision=hp,
                                            preferred_element_type=F32) + b1_ref[...]))
        hid_ref[...] = jnp.sin(f2_ref[...] * (jnp.dot(h1, w2_ref[...], precision=hp,
                                                      preferred_element_type=F32) + b2_ref[...]))

    h = hid_ref[...]
    win =jnp.exp(-t_ref[...] * dl_ref[...]) + DECAY_SHIFT
    hf = jnp.dot(h, w3f_ref[...], precision=hp, preferred_element_type=F32) * win
    hb = jnp.dot(h, w3b_ref[...], precision=hp, preferred_element_type=F32) * win
    lrow = lax.broadcasted_iota(jnp.int32, hb.shape, 0)
    hb = jnp.where(lrow == 0, 0.0, hb)

    cmat = c_ref[...]
    smat = s_ref[...]

    def fwd(x):
        hi = x.astype(BF16)
        lo = (x - hi.astype(F32)).astype(BF16)
        return (_bdot(cmat, hi) + _bdot(cmat, lo), _bdot(smat, hi) + _bdot(smat, lo))

    row = lax.broadcasted_iota(jnp.int32, (p, hf.shape[1]), 0)
    row0 = row == 0
    sigma = jnp.where(row % 2 == 1, -1.0, 1.0)

    def conj(a):
        return a[0], jnp.where(row0, a[1], -a[1])

    af0, af1 = fwd(hf[:p]), fwd(hf[p:])
    ab0, ab1 = fwd(hb[:p]), fwd(hb[p:])
    cb0 = conj(ab0)
    k0 = (af0[0] + cb0[0], af0[1] + cb0[1])
    k1 = (af1[0] + sigma * af0[0], af1[1] + sigma * af0[1])
    km1 = conj((ab1[0] + sigma * ab0[0], ab1[1] + sigma * ab0[1]))
    scale = jnp.where(row0, 0.5 / p, 1.0 / p)
    for i, a in enumerate((k0, k1, km1)):
        o_ref[2 * i] = a[0] * scale
        o_ref[2 * i + 1] = a[1] * scale


def _filter_spectra(z, t, w1, b1, f1, w2, b2, f2, w3, deltas, cmat, smat, p):
    l = z.shape[0]
    c = HYENA_WIDTH
    tc = 256
    nb = c // tc
    full = lambda a: pl.BlockSpec(a.shape, lambda j: (0,) * a.ndim)
    return pl.pallas_call(
        functools.partial(_filter_body, p=p),
        grid=(nb,),
        in_specs=[full(z), full(t), full(w1), full(b1), full(f1), full(w2), full(b2), full(f2),
                  pl.BlockSpec((FILTER_HIDDEN, tc), lambda j: (0, j)),
                  pl.BlockSpec((FILTER_HIDDEN, tc), lambda j: (0, j + nb)),
                  pl.BlockSpec((1, tc), lambda j: (0, j)),
                  full(cmat), full(smat)],
        out_specs=pl.BlockSpec((6, p, tc), lambda j: (0, 0, j)),
        out_shape=jax.ShapeDtypeStruct((6, p, c), F32),
        scratch_shapes=[pltpu.VMEM((l, FILTER_HIDDEN), F32)],
        compiler_params=_params(("arbitrary",), 40 << 20),
        name="filt",
    )(z, t, w1, b1, f1, w2, b2, f2, w3, w3, deltas, cmat, smat)


def _hyena_body(x0_ref, x1_ref, v_ref, cw_ref, cb_ref, ks_ref, hb_ref, c_ref, sf_ref, si_ref, o_ref, *, p):
    cmat = c_ref[...]
    sfwd = sf_ref[...]
    sinv = si_ref[...]
    seq = 2 * p
    row0 = lax.broadcasted_iota(jnp.int32, (p, o_ref.shape[1]), 0) == 0

    def cmul(i, uu):
        kre, kim = ks_ref[2 * i], ks_ref[2 * i + 1]
        ii = kim * uu[1]
        return (kre * uu[0] - jnp.where(row0, 0.0, ii),
                jnp.where(row0, ii, kre * uu[1] + kim * uu[0]))

    def inv(a, b):
        return _bdot(cmat, (a[0] + b[0]).astype(BF16)) + _bdot(sinv, (a[1] + b[1]).astype(BF16))

    for r0 in range(0, o_ref.shape[0], seq):
        rs = slice(r0, r0 + seq)
        x0 = _seq_conv3(x0_ref[rs, :].astype(F32), cw_ref[0], cb_ref[0])
        x1 = _seq_conv3(x1_ref[rs, :].astype(F32), cw_ref[1], cb_ref[1])
        v = _seq_conv3(v_ref[rs, :].astype(F32), cw_ref[2], cb_ref[2])
        u = x1 * v
        ub = u.astype(BF16)
        spec = [(_bdot(cmat, ub[j * p:(j + 1) * p]), _bdot(sfwd, ub[j * p:(j + 1) * p])) for j in range(2)]
        y_lo = inv(cmul(0, spec[0]), cmul(2, spec[1]))
        y_hi = inv(cmul(1, spec[0]), cmul(0, spec[1]))
        y = jnp.concatenate([y_lo, y_hi], axis=0)
        o_ref[rs, :] = (x0 * (y + u * hb_ref[...])).astype(o_ref.dtype)


def _hyena(proj, conv_w, conv_b, kspec, hbias, cmat, sfwd, sinv, batch, seq, p):
    c = HYENA_WIDTH
    tc = 256
    nb = c // tc
    nseq = 1
    xspec = lambda off: pl.BlockSpec((nseq * seq, tc), lambda j, b: (b, j + off * nb))
    full = lambda a: pl.BlockSpec(a.shape, lambda j, b: (0,) * a.ndim)
    vmem = (nseq * 6 * seq * tc * 2 + 2 * 6 * p * tc * 4 + 6 * p * p * 2 + nseq * 2 * seq * tc * 2
            + nseq * 6 * seq * tc * 4 + (6 << 20))
    return pl.pallas_call(
        functools.partial(_hyena_body, p=p),
        grid=(nb, batch // nseq),
        in_specs=[xspec(0), xspec(1), xspec(2),
                  pl.BlockSpec((3, CONV_WIDTH, tc), lambda j, b: (0, 0, j)),
                  pl.BlockSpec((3, 1, tc), lambda j, b: (0, 0, j)),
                  pl.BlockSpec((6, p, tc), lambda j, b: (0, 0, j)),
                  pl.BlockSpec((1, tc), lambda j, b: (0, j)),
                  full(cmat), full(sfwd), full(sinv)],
        out_specs=pl.BlockSpec((nseq * seq, tc), lambda j, b: (b, j)),
        out_shape=jax.ShapeDtypeStruct((batch * seq, c), BF16),
        compiler_params=_params(("parallel", "parallel"), vmem),
        name="hyena",
    )(proj, proj, proj, conv_w, conv_b, kspec, hbias, cmat, sfwd, sinv)


def _qkv_body(cq_ref, ckv_ref, kpe_ref, rot_ref, cos_ref, sin_ref, gq_ref, gkv_ref, wa_ref, wb_ref, wkv_ref,
              q_ref, k_ref, v_ref, *, rows):
    scale = (QK_NOPE_DIM + QK_ROPE_DIM) ** -0.5 * math.log2(math.e)
    n = cq_ref.shape[0] // rows

    def project(c):
        rs = slice(c * rows, (c + 1) * rows)
        cqn = _rms(cq_ref[rs, :].astype(F32), gq_ref[...]).astype(BF16)
        ckvn = _rms(ckv_ref[rs, :].astype(F32), gkv_ref[...]).astype(BF16)
        return (_bdot(cqn, wa_ref[...]),
                _bdot(cqn, wb_ref[...]),
                _bdot(ckvn, wkv_ref[...]),
                _bdot(kpe_ref[rs, :], rot_ref[...]))

    nxt = project(0)
    for c in range(n):
        qa, qb, kv, kpe_rot = nxt
        if c + 1 < n:
            nxt = project(c + 1)
        rs = slice(c * rows, (c + 1) * rows)
        cos = cos_ref[rs, :]
        sin = sin_ref[rs, :]
        kpe = (kpe_ref[rs, :].astype(F32) * cos + kpe_rot * sin).astype(BF16)
        for h in range(N_HEADS):
            a = h * QK_PAD_DIM
            r = h * V7X_LANES
            q_ref[0, h, rs, 0:128] = (qa[:, a:a + 128] * scale).astype(BF16)
            q_ref[0, h, rs, 128:256] = ((qa[:, a + 128:a + 256] * cos + qb[:, r:r + 128] * sin) * scale).astype(BF16)
            k_ref[0, h, rs, 0:128] = kv[:, a:a + 128].astype(BF16)
            k_ref[0, h, rs, 128:256] = kpe
            v_ref[0, h, rs, :] = kv[:, a + 128:a + 256].astype(BF16)


def _qkv(proj, rot, cos_t, sin_t, gq, gkv, wa, wb, wkv, batch, seq):
    tm = 512
    ns = seq // tm
    full = lambda a: pl.BlockSpec(a.shape, lambda b, i: (0,) * a.ndim)
    col = lambda width, off: pl.BlockSpec((tm, width), lambda b, i: (b * ns + i, off // width))
    hd = lambda w: pl.BlockSpec((1, N_HEADS, tm, w), lambda b, i: (b, 0, i, 0))
    return pl.pallas_call(
        functools.partial(_qkv_body, rows=tm // 2),
        grid=(batch, ns),
        in_specs=[col(Q_LORA_RANK, COL_CQ), col(KV_LORA_RANK, COL_CKV),
                  col(V7X_LANES, COL_KPE), full(rot),
                  pl.BlockSpec((tm, V7X_LANES), lambda b, i: (i, 0)),
                  pl.BlockSpec((tm, V7X_LANES), lambda b, i: (i, 0)),
                  full(gq), full(gkv), full(wa), full(wb), full(wkv)],
        out_specs=[hd(QK_PAD_DIM), hd(QK_PAD_DIM), hd(V_HEAD_DIM)],
        out_shape=[jax.ShapeDtypeStruct((batch, N_HEADS, seq, QK_PAD_DIM), BF16),
                   jax.ShapeDtypeStruct((batch, N_HEADS, seq, QK_PAD_DIM), BF16),
                   jax.ShapeDtypeStruct((batch, N_HEADS, seq, V_HEAD_DIM), BF16)],
        compiler_params=_params(("parallel", "parallel"), 48 << 20),
        name="qkv",
    )(proj, proj, proj, rot, cos_t, sin_t, gq, gkv, wa, wb, wkv)


def _attn_body(q_ref, k_ref, v_ref, o_ref, *, tq):
    heads, seq = q_ref.shape[1], q_ref.shape[2]
    work = [(h, c) for h in range(heads) for c in range(seq // tq)]

    def scores(h, c):
        return lax.dot_general(q_ref[0, h, c * tq:(c + 1) * tq, :], k_ref[0, h], (((1,), (1,)), ((), ())),
                               preferred_element_type=F32)

    s_next = scores(*work[0])
    for i, (h, c) in enumerate(work):
        s = s_next
        if i + 1 < len(work):
            s_next = scores(*work[i + 1])
        m = jnp.max(s, axis=-1, keepdims=True)
        e = jnp.exp2(s - m)
        l = jnp.sum(e, axis=-1, keepdims=True)
        o = _bdot(e.astype(BF16), v_ref[0, h])
        o_ref[0, c * tq:(c + 1) * tq, h * V_HEAD_DIM:(h + 1) * V_HEAD_DIM] = (o / l).astype(o_ref.dtype)


def _attn(q, k, v):
    batch, heads, seq, _ = q.shape
    tq = 256
    hps = 2
    hspec = lambda w: pl.BlockSpec((1, hps, seq, w), lambda b, h: (b, h, 0, 0))
    return pl.pallas_call(
        functools.partial(_attn_body, tq=tq),
        grid=(batch, heads // hps),
        in_specs=[hspec(QK_PAD_DIM), hspec(QK_PAD_DIM), hspec(V_HEAD_DIM)],
        out_specs=pl.BlockSpec((1, seq, hps * V_HEAD_DIM), lambda b, h: (b, 0, h)),
        out_shape=jax.ShapeDtypeStruct((batch, seq, heads * V_HEAD_DIM), BF16),
        compiler_params=_params(("parallel", "parallel"), 48 << 20),
        name="attn",
    )(q, k, v)


def _outproj_body(yh_ref, ya_ref, gh_ref, ga_ref, w_ref, x_ref, gpm_ref, gpf_ref, h_ref, hn_ref, *, rows):
    n = x_ref.shape[0] // rows

    def mix(c):
        rs = slice(c * rows, (c + 1) * rows)
        a = _rms(yh_ref[rs, :].astype(F32), gh_ref[...]).astype(BF16)
        b = _rms(ya_ref[rs, :].astype(F32), ga_ref[...]).astype(BF16)
        return _bdot(a, w_ref[0:HYENA_WIDTH, :]) + _bdot(b, w_ref[HYENA_WIDTH:, :])

    nxt = mix(0)
    for c in range(n):
        mixed = nxt
        if c + 1 < n:
            nxt = mix(c + 1)
        rs = slice(c * rows, (c + 1) * rows)
        h = x_ref[rs, :] + _rms(mixed, gpm_ref[...])
        h_ref[rs, :] = h
        hn_ref[rs, :] = _rms(h, gpf_ref[...]).astype(BF16)


def _outproj(yh, ya, gh, ga, w_out, x2d, gpm, gpf):
    m, d = x2d.shape
    tm = 512
    full = lambda a: pl.BlockSpec(a.shape, lambda i: (0,) * a.ndim)
    rows = lambda w: pl.BlockSpec((tm, w), lambda i: (i, 0))
    vmem = 2 * 2 * tm * HYENA_WIDTH * 2 + 2 * d * d * 2 + 2 * tm * d * (4 + 4 + 2) + 4 * tm * d * 4 + (6 << 20)
    return pl.pallas_call(
        functools.partial(_outproj_body, rows=tm // 2),
        grid=(m // tm,),
        in_specs=[rows(HYENA_WIDTH), rows(ATTN_WIDTH), full(gh), full(ga), full(w_out), rows(d),
                  full(gpm), full(gpf)],
        out_specs=[rows(d), rows(d)],
        out_shape=[jax.ShapeDtypeStruct((m, d), F32), jax.ShapeDtypeStruct((m, d), BF16)],
        compiler_params=_params(("parallel",), vmem),
        name="outproj",
    )(yh, ya, gh, ga, w_out, x2d, gpm, gpf)


def _ffn_up_body(hn_ref, wg_ref, wu_ref, cw_ref, cb_ref, o_ref, *, cols):
    hn = hn_ref[...]
    n = o_ref.shape[1] // cols

    def dots(c):
        cs = slice(c * cols, (c + 1) * cols)
        return _bdot(hn, wg_ref[:, cs]), _bdot(hn, wu_ref[:, cs])

    nxt = dots(0)
    for c in range(n):
        g, u = nxt
        if c + 1 < n:
            nxt = dots(c + 1)
        cs = slice(c * cols, (c + 1) * cols)
        g = _seq_conv3(g, cw_ref[:, cs], cb_ref[:, cs])
        gelu = 0.5 * g * (1.0 + jnp.tanh(math.sqrt(2.0 / math.pi) * (g + 0.044715 * (g * g * g))))
        o_ref[:, cs] = (gelu * u).astype(o_ref.dtype)


def _ffn_up(hn, w_up, conv_w, conv_b, batch, seq):
    d = hn.shape[1]
    tn = 512
    nb = D_FF // tn
    vmem = 2 * seq * d * 2 + 2 * 2 * d * tn * 2 + 2 * seq * tn * 2 + 8 * seq * tn * 4 + (6 << 20)
    return pl.pallas_call(
        functools.partial(_ffn_up_body, cols=256),
        grid=(batch, nb),
        in_specs=[pl.BlockSpec((seq, d), lambda b, j: (b, 0)),
                  pl.BlockSpec((d, tn), lambda b, j: (0, j)),
                  pl.BlockSpec((d, tn), lambda b, j: (0, j + nb)),
                  pl.BlockSpec((CONV_WIDTH, tn), lambda b, j: (0, j)),
                  pl.BlockSpec((1, tn), lambda b, j: (0, j))],
        out_specs=pl.BlockSpec((seq, tn), lambda b, j: (b, j)),
        out_shape=jax.ShapeDtypeStruct((batch * seq, D_FF), BF16),
        compiler_params=_params(("parallel", "arbitrary"), vmem),
        name="ffn_up",
    )(hn, w_up, w_up, conv_w, conv_b)


def _ffn_down_body(a_ref, w_ref, h_ref, g_ref, o_ref, acc_ref):
    k = pl.program_id(1)

    @pl.when(k == 0)
    def _():
        acc_ref[...] = jnp.zeros_like(acc_ref)

    acc_ref[...] += _bdot(a_ref[...], w_ref[...])

    @pl.when(k == pl.num_programs(1) - 1)
    def _():
        o_ref[...] = h_ref[...] + _rms(acc_ref[...], g_ref[...])


def _ffn_down(act, w_down, h, gain):
    m, d = h.shape
    tm, tk = 1024, 512
    vmem = 2 * tm * tk * 2 + 2 * tk * d * 2 + 2 * 2 * tm * d * 4 + 2 * tm * d * 4 + (4 << 20)
    return pl.pallas_call(
        _ffn_down_body,
        grid=(m // tm, D_FF // tk),
        in_specs=[pl.BlockSpec((tm, tk), lambda i, k: (i, k)),
                  pl.BlockSpec((tk, d), lambda i, k: (k, 0)),
                  pl.BlockSpec((tm, d), lambda i, k: (i, 0)),
                  pl.BlockSpec((1, d), lambda i, k: (0, 0))],
        out_specs=pl.BlockSpec((tm, d), lambda i, k: (i, 0)),
        out_shape=jax.ShapeDtypeStruct((m, d), F32),
        scratch_shapes=[pltpu.VMEM((tm, d), F32)],
        compiler_params=_params(("parallel", "arbitrary"), vmem),
        name="ffn_down",
    )(act, w_down, h, gain)


def _position_features(l):
    t = jnp.linspace(0.0, 1.0, l, dtype=F32)[:, None]
    bands = (FILTER_EMB_DIM - 1) // 2
    w = 2.0 * math.pi * jnp.arange(l, dtype=F32) / l
    f = jnp.linspace(1e-4, bands - 1, bands, dtype=F32)
    ang = w[:, None] * f[None, :]
    return t, jnp.concatenate([t, jnp.cos(ang), -jnp.sin(ang)], axis=-1)


def _layer(h, l, seq, prm):
    (pre_mix_gain, w_in, hyena_conv_w, hyena_conv_b, filt_w1, filt_b1, filt_freq1, filt_w2, filt_b2,
     filt_freq2, filt_w3, hyena_bias, q_norm_gain, w_uq, kv_norm_gain, w_ukv, hyena_out_gain, attn_out_gain,
     w_out, post_mix_gain, pre_ffn_gain, w_up, ffn_conv_w, ffn_conv_b, w_down, post_ffn_gain) = prm
    batch = h.shape[0]
    p = seq // 2
    x2d = h.reshape(batch * seq, D_MODEL)
    row = lambda a: a[l][None, :].astype(F32)

    wt = jnp.swapaxes(w_in[l], 0, 1)
    w_tail = jnp.pad(wt[COL_CQ:].astype(BF16), ((0, PROJ_WIDTH - wt.shape[0]), (0, 0)))
    proj = _inproj(x2d, row(pre_mix_gain), wt[:COL_CQ].astype(BF16), w_tail)

    cmat, sfwd, sinv = _dft_mats(p)
    t, z = _position_features(seq)
    zp = jnp.pad(z, ((0, 0), (0, V7X_LANES - FILTER_EMB_DIM)))
    w1p = jnp.pad(filt_w1[l], ((0, V7X_LANES - FILTER_EMB_DIM), (0, 0)))
    max_decay = math.log(DECAY_TARGET) / FAST_DECAY_PCT
    min_decay = math.log(DECAY_TARGET) / SLOW_DECAY_PCT
    deltas = jnp.abs(jnp.linspace(min_decay, max_decay, HYENA_WIDTH, dtype=F32))[None, :]
    kspec = _filter_spectra(zp, t, w1p, row(filt_b1), row(filt_freq1), filt_w2[l], row(filt_b2),
                            row(filt_freq2), filt_w3[l], deltas, cmat, sfwd, p)
    conv_w = hyena_conv_w[l].reshape(CONV_WIDTH, 3, HYENA_WIDTH).transpose(1, 0, 2)
    conv_b = hyena_conv_b[l].reshape(3, 1, HYENA_WIDTH)
    y_hyena = _hyena(proj, conv_w, conv_b, kspec, row(hyena_bias), cmat, sfwd, sinv, batch, seq, p)

    pos = jnp.arange(seq, dtype=F32)
    inv_freq = 1.0 / (ROPE_THETA ** (jnp.arange(0, QK_ROPE_DIM, 2, dtype=F32) / QK_ROPE_DIM))
    ang = pos[:, None] * inv_freq[None, :]
    ang = jnp.concatenate([ang, ang], axis=-1)
    lpad = ((0, 0), (0, V7X_LANES - QK_ROPE_DIM))
    cos_t = jnp.pad(jnp.cos(ang), lpad)
    sin_t = jnp.pad(jnp.sin(ang), lpad)
    dqk = QK_NOPE_DIM + QK_ROPE_DIM
    wq = w_uq[l].reshape(Q_LORA_RANK, N_HEADS, dqk)
    wa = jnp.pad(wq, ((0, 0), (0, 0), (0, QK_PAD_DIM - dqk))).reshape(Q_LORA_RANK, N_HEADS * QK_PAD_DIM)
    wq_pe = wq[:, :, QK_NOPE_DIM:]
    wq_rot = jnp.concatenate([-wq_pe[..., QK_ROPE_DIM // 2:], wq_pe[..., :QK_ROPE_DIM // 2]], axis=-1)
    wb = jnp.pad(wq_rot, ((0, 0), (0, 0), (0, V7X_LANES - QK_ROPE_DIM))).reshape(Q_LORA_RANK, N_HEADS * V7X_LANES)
    ri = jnp.arange(V7X_LANES, dtype=jnp.int32)[:, None]
    rj = jnp.arange(V7X_LANES, dtype=jnp.int32)[None, :]
    half = QK_ROPE_DIM // 2
    rot = (jnp.where((ri == rj + half) & (rj < half), -1.0, 0.0)
           + jnp.where((ri == rj - half) & (rj >= half) & (rj < QK_ROPE_DIM), 1.0, 0.0)).astype(BF16)
    q, k, v = _qkv(proj, rot, cos_t, sin_t, row(q_norm_gain), row(kv_norm_gain), wa.astype(BF16),
                   wb.astype(BF16), w_ukv[l].astype(BF16), batch, seq)
    y_attn = _attn(q, k, v).reshape(batch * seq, ATTN_WIDTH)

    h2d, hn = _outproj(y_hyena, y_attn, row(hyena_out_gain), row(attn_out_gain), w_out[l].astype(BF16), x2d,
                       row(post_mix_gain), row(pre_ffn_gain))

    act = _ffn_up(hn, w_up[l].astype(BF16), ffn_conv_w[l], row(ffn_conv_b), batch, seq)
    out = _ffn_down(act, w_down[l].astype(BF16), h2d, row(post_ffn_gain))
    return out.reshape(batch, seq, D_MODEL)


def kernel(x, pre_mix_gain, w_in, hyena_conv_w, hyena_conv_b, filt_w1, filt_b1, filt_freq1, filt_w2, filt_b2,
           filt_freq2, filt_w3, hyena_bias, q_norm_gain, w_uq, kv_norm_gain, w_ukv, hyena_out_gain,
           attn_out_gain, w_out, post_mix_gain, pre_ffn_gain, w_up, ffn_conv_w, ffn_conv_b, w_down,
           post_ffn_gain):
    prm = (pre_mix_gain, w_in, hyena_conv_w, hyena_conv_b, filt_w1, filt_b1, filt_freq1, filt_w2, filt_b2,
           filt_freq2, filt_w3, hyena_bias, q_norm_gain, w_uq, kv_norm_gain, w_ukv, hyena_out_gain,
           attn_out_gain, w_out, post_mix_gain, pre_ffn_gain, w_up, ffn_conv_w, ffn_conv_b, w_down,
           post_ffn_gain)
    seq = x.shape[1]
    h = x
    for l in range(w_in.shape[0]):
        h = _layer(h, l, seq, prm)
    return h
```

```python
import functools
import math

import jax
import jax.numpy as jnp
from jax import lax
from jax.experimental import pallas as pl
from jax.experimental.pallas import tpu as pltpu

F32 = jnp.float32
BF16 = jnp.bfloat16

D_MODEL = 2048
HYENA_WIDTH = 1024
CONV_WIDTH = 3
FILTER_EMB_DIM = 33
FILTER_HIDDEN = 64
DECAY_TARGET = 1e-2
FAST_DECAY_PCT = 0.3
SLOW_DECAY_PCT = 1.5
DECAY_SHIFT = 0.05
N_HEADS = 8
QK_NOPE_DIM = 128
QK_ROPE_DIM = 64
V_HEAD_DIM = 128
Q_LORA_RANK = 512
KV_LORA_RANK = 256
ROPE_THETA = 10000.0
ATTN_WIDTH = N_HEADS * V_HEAD_DIM
D_FF = 5632
NORM_EPS = 1e-6

V7X_VMEM_BYTES = 64 * 1024 * 1024
V7X_LANES = 128
V7X_MXU_DIM = 256
V7X_SUBLANES = 8
HALO = V7X_SUBLANES

QK_PAD_DIM = V7X_MXU_DIM
PROJ_WIDTH = 4096
COL_CQ = 3 * HYENA_WIDTH
COL_CKV = COL_CQ + Q_LORA_RANK
COL_KPE = COL_CKV + KV_LORA_RANK


def _params(semantics, vmem_bytes, flags=None):
    return pltpu.CompilerParams(dimension_semantics=semantics, flags=flags,
                                vmem_limit_bytes=min(int(vmem_bytes), V7X_VMEM_BYTES - (4 << 20)))


def _rms(x, gain):
    return x * lax.rsqrt(jnp.mean(x * x, axis=-1, keepdims=True) + NORM_EPS) * gain


def _bdot(a, b):
    return jnp.dot(a, b, preferred_element_type=F32)


def _seq_conv3(x, w, b):
    t = x.shape[0]
    row = lax.broadcasted_iota(jnp.int32, x.shape, 0)
    prev = jnp.where(row == 0, 0.0, pltpu.roll(x, 1, 0))
    nxt = jnp.where(row == t - 1, 0.0, pltpu.roll(x, t - 1, 0))
    return prev * w[0:1] + x * w[1:2] + nxt * w[2:3] + b


def _inproj_body(x_ref, g_ref, wm_ref, wt_ref, o_ref, *, n_main, rows):
    j = pl.program_id(1)

    def project(w_ref):
        w = w_ref[...]
        for r0 in range(0, x_ref.shape[0], rows):
            xn = _rms(x_ref[r0:r0 + rows, :], g_ref[...]).astype(BF16)
            o_ref[r0:r0 + rows, :] = lax.dot_general(xn, w, (((1,), (1,)), ((), ())),
                                                     preferred_element_type=F32).astype(o_ref.dtype)

    @pl.when(j < n_main)
    def _():
        project(wm_ref)

    @pl.when(j == n_main)
    def _():
        project(wt_ref)


def _inproj(x2d, gain, w_main, w_tail):
    m, d = x2d.shape
    tm, tn = 1024, w_tail.shape[0]
    n_main = w_main.shape[0] // tn
    vmem = 2 * tm * d * 4 + tm * d * 2 + 4 * d * tn * 2 + 2 * tm * tn * 2 + tm * tn * 4 + (8 << 20)
    return pl.pallas_call(
        functools.partial(_inproj_body, n_main=n_main, rows=256),
        grid=(m // tm, n_main + 1),
        in_specs=[pl.BlockSpec((tm, d), lambda i, j: (i, 0)),
                  pl.BlockSpec((1, d), lambda i, j: (0, 0)),
                  pl.BlockSpec((tn, d), lambda i, j: (jnp.minimum(j, n_main - 1), 0)),
                  pl.BlockSpec((tn, d), lambda i, j: (0, 0))],
        out_specs=pl.BlockSpec((tm, tn), lambda i, j: (i, j)),
        out_shape=jax.ShapeDtypeStruct((m, (n_main + 1) * tn), BF16),
        compiler_params=_params(("parallel", "arbitrary"), vmem),
        name="inproj",
    )(x2d, gain, w_main, w_tail)


def _dft_mats(p):
    r = 32
    assert p == r * r
    idx = jnp.arange(p, dtype=jnp.int32)
    sub = jnp.arange(r, dtype=jnp.int32)
    ang_hi = ((idx[:, None] * (r * sub)[None, :]) % (2 * p)).astype(F32) * (math.pi / p)
    ang_lo = ((idx[:, None] * sub[None, :]) % (2 * p)).astype(F32) * (math.pi / p)
    ch, sh = jnp.cos(ang_hi)[:, :, None], jnp.sin(ang_hi)[:, :, None]
    cl, sl = jnp.cos(ang_lo)[:, None, :], jnp.sin(ang_lo)[:, None, :]
    c = (ch * cl - sh * sl).reshape(p, p)
    s = -(sh * cl + ch * sl).reshape(p, p)
    alt = jnp.where(idx % 2 == 0, 1.0, -1.0).astype(F32)
    s_fwd = jnp.where(idx[:, None] == 0, alt[None, :], s)
    s_inv = jnp.where(idx[None, :] == 0, alt[:, None], s)
    return c.astype(BF16), s_fwd.astype(BF16), s_inv.astype(BF16)


def _filter_body(z_ref, t_ref, w1_ref, b1_ref, f1_ref, w2_ref, b2_ref, f2_ref, w3f_ref, w3b_ref,
                 dl_ref, c_ref, s_ref, o_ref, hid_ref, *, p):
    hp = lax.Precision.HIGHEST

    @pl.when(pl.program_id(0) == 0)
    def _():
        h1 = jnp.sin(f1_ref[...] * (jnp.dot(z_ref[...], w1_ref[...], precision=hp,
                                            preferred_element_type=F32) + b1_ref[...]))
        hid_ref[...] = jnp.sin(f2_ref[...] * (jnp.dot(h1, w2_ref[...], precision=hp,
                                                      preferred_element_type=F32) + b2_ref[...]))

    h = hid_ref[...]
    win =jnp.exp(-t_ref[...] * dl_ref[...]) + DECAY_SHIFT
    hf = jnp.dot(h, w3f_ref[...], precision=hp, preferred_element_type=F32) * win
    hb = jnp.dot(h, w3b_ref[...], precision=hp, preferred_element_type=F32) * win
    lrow = lax.broadcasted_iota(jnp.int32, hb.shape, 0)
    hb = jnp.where(lrow == 0, 0.0, hb)

    cmat = c_ref[...]
    smat = s_ref[...]

    def fwd(x):
        hi = x.astype(BF16)
        lo = (x - hi.astype(F32)).astype(BF16)
        return (_bdot(cmat, hi) + _bdot(cmat, lo), _bdot(smat, hi) + _bdot(smat, lo))

    row = lax.broadcasted_iota(jnp.int32, (p, hf.shape[1]), 0)
    row0 = row == 0
    sigma = jnp.where(row % 2 == 1, -1.0, 1.0)

    def conj(a):
        return a[0], jnp.where(row0, a[1], -a[1])

    af0, af1 = fwd(hf[:p]), fwd(hf[p:])
    ab0, ab1 = fwd(hb[:p]), fwd(hb[p:])
    cb0 = conj(ab0)
    k0 = (af0[0] + cb0[0], af0[1] + cb0[1])
    k1 = (af1[0] + sigma * af0[0], af1[1] + sigma * af0[1])
    km1 = conj((ab1[0] + sigma * ab0[0], ab1[1] + sigma * ab0[1]))
    scale = jnp.where(row0, 0.5 / p, 1.0 / p)
    for i, a in enumerate((k0, k1, km1)):
        o_ref[2 * i] = a[0] * scale
        o_ref[2 * i + 1] = a[1] * scale


def _filter_spectra(z, t, w1, b1, f1, w2, b2, f2, w3, deltas, cmat, smat, p):
    l = z.shape[0]
    c = HYENA_WIDTH
    tc = 256
    nb = c // tc
    full = lambda a: pl.BlockSpec(a.shape, lambda j: (0,) * a.ndim)
    return pl.pallas_call(
        functools.partial(_filter_body, p=p),
        grid=(nb,),
        in_specs=[full(z), full(t), full(w1), full(b1), full(f1), full(w2), full(b2), full(f2),
                  pl.BlockSpec((FILTER_HIDDEN, tc), lambda j: (0, j)),
                  pl.BlockSpec((FILTER_HIDDEN, tc), lambda j: (0, j + nb)),
                  pl.BlockSpec((1, tc), lambda j: (0, j)),
                  full(cmat), full(smat)],
        out_specs=pl.BlockSpec((6, p, tc), lambda j: (0, 0, j)),
        out_shape=jax.ShapeDtypeStruct((6, p, c), F32),
        scratch_shapes=[pltpu.VMEM((l, FILTER_HIDDEN), F32)],
        compiler_params=_params(("arbitrary",), 40 << 20),
        name="filt",
    )(z, t, w1, b1, f1, w2, b2, f2, w3, w3, deltas, cmat, smat)


def _hyena_body(x0_ref, x1_ref, v_ref, cw_ref, cb_ref, ks_ref, hb_ref, c_ref, sf_ref, si_ref, o_ref, *, p):
    cmat = c_ref[...]
    sfwd = sf_ref[...]
    sinv = si_ref[...]
    seq = 2 * p
    row0 = lax.broadcasted_iota(jnp.int32, (p, o_ref.shape[1]), 0) == 0

    def cmul(i, uu):
        kre, kim = ks_ref[2 * i], ks_ref[2 * i + 1]
        ii = kim * uu[1]
        return (kre * uu[0] - jnp.where(row0, 0.0, ii),
                jnp.where(row0, ii, kre * uu[1] + kim * uu[0]))

    def inv(a, b):
        return _bdot(cmat, (a[0] + b[0]).astype(BF16)) + _bdot(sinv, (a[1] + b[1]).astype(BF16))

    for r0 in range(0, o_ref.shape[0], seq):
        rs = slice(r0, r0 + seq)
        x0 = _seq_conv3(x0_ref[rs, :].astype(F32), cw_ref[0], cb_ref[0])
        x1 = _seq_conv3(x1_ref[rs, :].astype(F32), cw_ref[1], cb_ref[1])
        v = _seq_conv3(v_ref[rs, :].astype(F32), cw_ref[2], cb_ref[2])
        u = x1 * v
        ub = u.astype(BF16)
        spec = [(_bdot(cmat, ub[j * p:(j + 1) * p]), _bdot(sfwd, ub[j * p:(j + 1) * p])) for j in range(2)]
        y_lo = inv(cmul(0, spec[0]), cmul(2, spec[1]))
        y_hi = inv(cmul(1, spec[0]), cmul(0, spec[1]))
        y = jnp.concatenate([y_lo, y_hi], axis=0)
        o_ref[rs, :] = (x0 * (y + u * hb_ref[...])).astype(o_ref.dtype)


def _hyena(proj, conv_w, conv_b, kspec, hbias, cmat, sfwd, sinv, batch, seq, p):
    c = HYENA_WIDTH
    tc = 256
    nb = c // tc
    nseq = 1
    xspec = lambda off: pl.BlockSpec((nseq * seq, tc), lambda j, b: (b, j + off * nb))
    full = lambda a: pl.BlockSpec(a.shape, lambda j, b: (0,) * a.ndim)
    vmem = (nseq * 6 * seq * tc * 2 + 2 * 6 * p * tc * 4 + 6 * p * p * 2 + nseq * 2 * seq * tc * 2
            + nseq * 6 * seq * tc * 4 + (6 << 20))
    return pl.pallas_call(
        functools.partial(_hyena_body, p=p),
        grid=(nb, batch // nseq),
        in_specs=[xspec(0), xspec(1), xspec(2),
                  pl.BlockSpec((3, CONV_WIDTH, tc), lambda j, b: (0, 0, j)),
                  pl.BlockSpec((3, 1, tc), lambda j, b: (0, 0, j)),
                  pl.BlockSpec((6, p, tc), lambda j, b: (0, 0, j)),
                  pl.BlockSpec((1, tc), lambda j, b: (0, j)),
                  full(cmat), full(sfwd), full(sinv)],
        out_specs=pl.BlockSpec((nseq * seq, tc), lambda j, b: (b, j)),
        out_shape=jax.ShapeDtypeStruct((batch * seq, c), BF16),
        compiler_params=_params(("parallel", "parallel"), vmem),
        name="hyena",
    )(proj, proj, proj, conv_w, conv_b, kspec, hbias, cmat, sfwd, sinv)


def _qkv_body(cq_ref, ckv_ref, kpe_ref, rot_ref, cos_ref, sin_ref, gq_ref, gkv_ref, wa_ref, wb_ref, wkv_ref,
              q_ref, k_ref, v_ref, *, rows):
    scale = (QK_NOPE_DIM + QK_ROPE_DIM) ** -0.5 * math.log2(math.e)
    n = cq_ref.shape[0] // rows

    def project(c):
        rs = slice(c * rows, (c + 1) * rows)
        cqn = _rms(cq_ref[rs, :].astype(F32), gq_ref[...]).astype(BF16)
        ckvn = _rms(ckv_ref[rs, :].astype(F32), gkv_ref[...]).astype(BF16)
        return (_bdot(cqn, wa_ref[...]),
                _bdot(cqn, wb_ref[...]),
                _bdot(ckvn, wkv_ref[...]),
                _bdot(kpe_ref[rs, :], rot_ref[...]))

    nxt = project(0)
    for c in range(n):
        qa, qb, kv, kpe_rot = nxt
        if c + 1 < n:
            nxt = project(c + 1)
        rs = slice(c * rows, (c + 1) * rows)
        cos = cos_ref[rs, :]
        sin = sin_ref[rs, :]
        kpe = (kpe_ref[rs, :].astype(F32) * cos + kpe_rot * sin).astype(BF16)
        for h in range(N_HEADS):
            a = h * QK_PAD_DIM
            r = h * V7X_LANES
            q_ref[0, h, rs, 0:128] = (qa[:, a:a + 128] * scale).astype(BF16)
            q_ref[0, h, rs, 128:256] = ((qa[:, a + 128:a + 256] * cos + qb[:, r:r + 128] * sin) * scale).astype(BF16)
            k_ref[0, h, rs, 0:128] = kv[:, a:a + 128].astype(BF16)
            k_ref[0, h, rs, 128:256] = kpe
            v_ref[0, h, rs, :] = kv[:, a + 128:a + 256].astype(BF16)


def _qkv(proj, rot, cos_t, sin_t, gq, gkv, wa, wb, wkv, batch, seq):
    tm = 512
    ns = seq // tm
    full = lambda a: pl.BlockSpec(a.shape, lambda b, i: (0,) * a.ndim)
    col = lambda width, off: pl.BlockSpec((tm, width), lambda b, i: (b * ns + i, off // width))
    hd = lambda w: pl.BlockSpec((1, N_HEADS, tm, w), lambda b, i: (b, 0, i, 0))
    return pl.pallas_call(
        functools.partial(_qkv_body, rows=tm // 2),
        grid=(batch, ns),
        in_specs=[col(Q_LORA_RANK, COL_CQ), col(KV_LORA_RANK, COL_CKV),
                  col(V7X_LANES, COL_KPE), full(rot),
                  pl.BlockSpec((tm, V7X_LANES), lambda b, i: (i, 0)),
                  pl.BlockSpec((tm, V7X_LANES), lambda b, i: (i, 0)),
                  full(gq), full(gkv), full(wa), full(wb), full(wkv)],
        out_specs=[hd(QK_PAD_DIM), hd(QK_PAD_DIM), hd(V_HEAD_DIM)],
        out_shape=[jax.ShapeDtypeStruct((batch, N_HEADS, seq, QK_PAD_DIM), BF16),
                   jax.ShapeDtypeStruct((batch, N_HEADS, seq, QK_PAD_DIM), BF16),
                   jax.ShapeDtypeStruct((batch, N_HEADS, seq, V_HEAD_DIM), BF16)],
        compiler_params=_params(("parallel", "parallel"), 48 << 20),
        name="qkv",
    )(proj, proj, proj, rot, cos_t, sin_t, gq, gkv, wa, wb, wkv)


def _attn_body(q_ref, k_ref, v_ref, o_ref, *, tq):
    heads, seq = q_ref.shape[1], q_ref.shape[2]
    work = [(h, c) for h in range(heads) for c in range(seq // tq)]

    def scores(h, c):
        return lax.dot_general(q_ref[0, h, c * tq:(c + 1) * tq, :], k_ref[0, h], (((1,), (1,)), ((), ())),
                               preferred_element_type=F32)

    s_next = scores(*work[0])
    for i, (h, c) in enumerate(work):
        s = s_next
        if i + 1 < len(work):
            s_next = scores(*work[i + 1])
        m = jnp.max(s, axis=-1, keepdims=True)
        e = jnp.exp2(s - m)
        l = jnp.sum(e, axis=-1, keepdims=True)
        o = _bdot(e.astype(BF16), v_ref[0, h])
        o_ref[0, c * tq:(c + 1) * tq, h * V_HEAD_DIM:(h + 1) * V_HEAD_DIM] = (o / l).astype(o_ref.dtype)


def _attn(q, k, v):
    batch, heads, seq, _ = q.shape
    tq = 256
    hps = 2
    hspec = lambda w: pl.BlockSpec((1, hps, seq, w), lambda b, h: (b, h, 0, 0))
    return pl.pallas_call(
        functools.partial(_attn_body, tq=tq),
        grid=(batch, heads // hps),
        in_specs=[hspec(QK_PAD_DIM), hspec(QK_PAD_DIM), hspec(V_HEAD_DIM)],
        out_specs=pl.BlockSpec((1, seq, hps * V_HEAD_DIM), lambda b, h: (b, 0, h)),
        out_shape=jax.ShapeDtypeStruct((batch, seq, heads * V_HEAD_DIM), BF16),
        compiler_params=_params(("parallel", "parallel"), 48 << 20),
        name="attn",
    )(q, k, v)


def _outproj_body(yh_ref, ya_ref, gh_ref, ga_ref, w_ref, x_ref, gpm_ref, gpf_ref, h_ref, hn_ref, *, rows):
    n = x_ref.shape[0] // rows

    def mix(c):
        rs = slice(c * rows, (c + 1) * rows)
        a = _rms(yh_ref[rs, :].astype(F32), gh_ref[...]).astype(BF16)
        b = _rms(ya_ref[rs, :].astype(F32), ga_ref[...]).astype(BF16)
        return _bdot(a, w_ref[0:HYENA_WIDTH, :]) + _bdot(b, w_ref[HYENA_WIDTH:, :])

    nxt = mix(0)
    for c in range(n):
        mixed = nxt
        if c + 1 < n:
            nxt = mix(c + 1)
        rs = slice(c * rows, (c + 1) * rows)
        h = x_ref[rs, :] + _rms(mixed, gpm_ref[...])
        h_ref[rs, :] = h
        hn_ref[rs, :] = _rms(h, gpf_ref[...]).astype(BF16)


def _outproj(yh, ya, gh, ga, w_out, x2d, gpm, gpf):
    m, d = x2d.shape
    tm = 512
    full = lambda a: pl.BlockSpec(a.shape, lambda i: (0,) * a.ndim)
    rows = lambda w: pl.BlockSpec((tm, w), lambda i: (i, 0))
    vmem = 2 * 2 * tm * HYENA_WIDTH * 2 + 2 * d * d * 2 + 2 * tm * d * (4 + 4 + 2) + 4 * tm * d * 4 + (6 << 20)
    return pl.pallas_call(
        functools.partial(_outproj_body, rows=tm // 2),
        grid=(m // tm,),
        in_specs=[rows(HYENA_WIDTH), rows(ATTN_WIDTH), full(gh), full(ga), full(w_out), rows(d),
                  full(gpm), full(gpf)],
        out_specs=[rows(d), rows(d)],
        out_shape=[jax.ShapeDtypeStruct((m, d), F32), jax.ShapeDtypeStruct((m, d), BF16)],
        compiler_params=_params(("parallel",), vmem),
        name="outproj",
    )(yh, ya, gh, ga, w_out, x2d, gpm, gpf)


def _ffn_up_body(hn_ref, wg_ref, wu_ref, cw_ref, cb_ref, o_ref, *, cols):
    hn = hn_ref[...]
    n = o_ref.shape[1] // cols

    def dots(c):
        cs = slice(c * cols, (c + 1) * cols)
        return _bdot(hn, wg_ref[:, cs]), _bdot(hn, wu_ref[:, cs])

    nxt = dots(0)
    for c in range(n):
        g, u = nxt
        if c + 1 < n:
            nxt = dots(c + 1)
        cs = slice(c * cols, (c + 1) * cols)
        g = _seq_conv3(g, cw_ref[:, cs], cb_ref[:, cs])
        gelu = 0.5 * g * (1.0 + jnp.tanh(math.sqrt(2.0 / math.pi) * (g + 0.044715 * (g * g * g))))
        o_ref[:, cs] = (gelu * u).astype(o_ref.dtype)


def _ffn_up(hn, w_up, conv_w, conv_b, batch, seq):
    d = hn.shape[1]
    tn = 512
    nb = D_FF // tn
    vmem = 2 * seq * d * 2 + 2 * 2 * d * tn * 2 + 2 * seq * tn * 2 + 8 * seq * tn * 4 + (6 << 20)
    return pl.pallas_call(
        functools.partial(_ffn_up_body, cols=256),
        grid=(batch, nb),
        in_specs=[pl.BlockSpec((seq, d), lambda b, j: (b, 0)),
                  pl.BlockSpec((d, tn), lambda b, j: (0, j)),
                  pl.BlockSpec((d, tn), lambda b, j: (0, j + nb)),
                  pl.BlockSpec((CONV_WIDTH, tn), lambda b, j: (0, j)),
                  pl.BlockSpec((1, tn), lambda b, j: (0, j))],
        out_specs=pl.BlockSpec((seq, tn), lambda b, j: (b, j)),
        out_shape=jax.ShapeDtypeStruct((batch * seq, D_FF), BF16),
        compiler_params=_params(("parallel", "arbitrary"), vmem),
        name="ffn_up",
    )(hn, w_up, w_up, conv_w, conv_b)


def _ffn_down_body(a_ref, w_ref, h_ref, g_ref, o_ref, *, rows):
    for r0 in range(0, a_ref.shape[0], rows):
        rs = slice(r0, r0 + rows)
        o_ref[rs, :] = h_ref[rs, :] + _rms(_bdot(a_ref[rs, :], w_ref[...]), g_ref[...])


def _ffn_down(act, w_down, h, gain):
    m, d = h.shape
    k = act.shape[1]
    tm = 512
    vmem = k * d * 2 + 2 * tm * k * 2 + 2 * 2 * tm * d * 4 + tm * d * 4 + (4 << 20)
    return pl.pallas_call(
        functools.partial(_ffn_down_body, rows=tm // 2),
        grid=(m // tm,),
        in_specs=[pl.BlockSpec((tm, k), lambda i: (i, 0)),
                  pl.BlockSpec((k, d), lambda i: (0, 0), pipeline_mode=pl.Buffered(1)),
                  pl.BlockSpec((tm, d), lambda i: (i, 0)),
                  pl.BlockSpec((1, d), lambda i: (0, 0))],
        out_specs=pl.BlockSpec((tm, d), lambda i: (i, 0)),
        out_shape=jax.ShapeDtypeStruct((m, d), F32),
        compiler_params=_params(("arbitrary",), vmem),
        name="ffn_down",
    )(act, w_down, h, gain)


def _position_features(l):
    t = jnp.linspace(0.0, 1.0, l, dtype=F32)[:, None]
    bands = (FILTER_EMB_DIM - 1) // 2
    w = 2.0 * math.pi * jnp.arange(l, dtype=F32) / l
    f = jnp.linspace(1e-4, bands - 1, bands, dtype=F32)
    ang = w[:, None] * f[None, :]
    return t, jnp.concatenate([t, jnp.cos(ang), -jnp.sin(ang)], axis=-1)


def _layer(h, l, seq, prm):
    (pre_mix_gain, w_in, hyena_conv_w, hyena_conv_b, filt_w1, filt_b1, filt_freq1, filt_w2, filt_b2,
     filt_freq2, filt_w3, hyena_bias, q_norm_gain, w_uq, kv_norm_gain, w_ukv, hyena_out_gain, attn_out_gain,
     w_out, post_mix_gain, pre_ffn_gain, w_up, ffn_conv_w, ffn_conv_b, w_down, post_ffn_gain) = prm
    batch = h.shape[0]
    p = seq // 2
    x2d = h.reshape(batch * seq, D_MODEL)
    row = lambda a: a[l][None, :].astype(F32)

    wt = jnp.swapaxes(w_in[l], 0, 1)
    w_tail = jnp.pad(wt[COL_CQ:].astype(BF16), ((0, PROJ_WIDTH - wt.shape[0]), (0, 0)))
    proj = _inproj(x2d, row(pre_mix_gain), wt[:COL_CQ].astype(BF16), w_tail)

    cmat, sfwd, sinv = _dft_mats(p)
    t, z = _position_features(seq)
    zp = jnp.pad(z, ((0, 0), (0, V7X_LANES - FILTER_EMB_DIM)))
    w1p = jnp.pad(filt_w1[l], ((0, V7X_LANES - FILTER_EMB_DIM), (0, 0)))
    max_decay = math.log(DECAY_TARGET) / FAST_DECAY_PCT
    min_decay = math.log(DECAY_TARGET) / SLOW_DECAY_PCT
    deltas = jnp.abs(jnp.linspace(min_decay, max_decay, HYENA_WIDTH, dtype=F32))[None, :]
    kspec = _filter_spectra(zp, t, w1p, row(filt_b1), row(filt_freq1), filt_w2[l], row(filt_b2),
                            row(filt_freq2), filt_w3[l], deltas, cmat, sfwd, p)
    conv_w = hyena_conv_w[l].reshape(CONV_WIDTH, 3, HYENA_WIDTH).transpose(1, 0, 2)
    conv_b = hyena_conv_b[l].reshape(3, 1, HYENA_WIDTH)
    y_hyena = _hyena(proj, conv_w, conv_b, kspec, row(hyena_bias), cmat, sfwd, sinv, batch, seq, p)

    pos = jnp.arange(seq, dtype=F32)
    inv_freq = 1.0 / (ROPE_THETA ** (jnp.arange(0, QK_ROPE_DIM, 2, dtype=F32) / QK_ROPE_DIM))
    ang = pos[:, None] * inv_freq[None, :]
    ang = jnp.concatenate([ang, ang], axis=-1)
    lpad = ((0, 0), (0, V7X_LANES - QK_ROPE_DIM))
    cos_t = jnp.pad(jnp.cos(ang), lpad)
    sin_t = jnp.pad(jnp.sin(ang), lpad)
    dqk = QK_NOPE_DIM + QK_ROPE_DIM
    wq = w_uq[l].reshape(Q_LORA_RANK, N_HEADS, dqk)
    wa = jnp.pad(wq, ((0, 0), (0, 0), (0, QK_PAD_DIM - dqk))).reshape(Q_LORA_RANK, N_HEADS * QK_PAD_DIM)
    wq_pe = wq[:, :, QK_NOPE_DIM:]
    wq_rot = jnp.concatenate([-wq_pe[..., QK_ROPE_DIM // 2:], wq_pe[..., :QK_ROPE_DIM // 2]], axis=-1)
    wb = jnp.pad(wq_rot, ((0, 0), (0, 0), (0, V7X_LANES - QK_ROPE_DIM))).reshape(Q_LORA_RANK, N_HEADS * V7X_LANES)
    ri = jnp.arange(V7X_LANES, dtype=jnp.int32)[:, None]
    rj = jnp.arange(V7X_LANES, dtype=jnp.int32)[None, :]
    half = QK_ROPE_DIM // 2
    rot = (jnp.where((ri == rj + half) & (rj < half), -1.0, 0.0)
           + jnp.where((ri == rj - half) & (rj >= half) & (rj < QK_ROPE_DIM), 1.0, 0.0)).astype(BF16)
    q, k, v = _qkv(proj, rot, cos_t, sin_t, row(q_norm_gain), row(kv_norm_gain), wa.astype(BF16),
                   wb.astype(BF16), w_ukv[l].astype(BF16), batch, seq)
    y_attn = _attn(q, k, v).reshape(batch * seq, ATTN_WIDTH)

    h2d, hn = _outproj(y_hyena, y_attn, row(hyena_out_gain), row(attn_out_gain), w_out[l].astype(BF16), x2d,
                       row(post_mix_gain), row(pre_ffn_gain))

    act = _ffn_up(hn, w_up[l].astype(BF16), ffn_conv_w[l], row(ffn_conv_b), batch, seq)
    out = _ffn_down(act, w_down[l].astype(BF16), h2d, row(post_ffn_gain))
    return out.reshape(batch, seq, D_MODEL)


def kernel(x, pre_mix_gain, w_in, hyena_conv_w, hyena_conv_b, filt_w1, filt_b1, filt_freq1, filt_w2, filt_b2,
           filt_freq2, filt_w3, hyena_bias, q_norm_gain, w_uq, kv_norm_gain, w_ukv, hyena_out_gain,
           attn_out_gain, w_out, post_mix_gain, pre_ffn_gain, w_up, ffn_conv_w, ffn_conv_b, w_down,
           post_ffn_gain):
    prm = (pre_mix_gain, w_in, hyena_conv_w, hyena_conv_b, filt_w1, filt_b1, filt_freq1, filt_w2, filt_b2,
           filt_freq2, filt_w3, hyena_bias, q_norm_gain, w_uq, kv_norm_gain, w_ukv, hyena_out_gain,
           attn_out_gain, w_out, post_mix_gain, pre_ffn_gain, w_up, ffn_conv_w, ffn_conv_b, w_down,
           post_ffn_gain)
    seq = x.shape[1]
    h = x
    for l in range(w_in.shape[0]):
        h = _layer(h, l, seq, prm)
    return h
```

```python
import functools
import math

import jax
import jax.numpy as jnp
from jax import lax
from jax.experimental import pallas as pl
from jax.experimental.pallas import tpu as pltpu

F32 = jnp.float32
BF16 = jnp.bfloat16

D_MODEL = 2048
HYENA_WIDTH = 1024
CONV_WIDTH = 3
FILTER_EMB_DIM = 33
FILTER_HIDDEN = 64
DECAY_TARGET = 1e-2
FAST_DECAY_PCT = 0.3
SLOW_DECAY_PCT = 1.5
DECAY_SHIFT = 0.05
N_HEADS = 8
QK_NOPE_DIM = 128
QK_ROPE_DIM = 64
V_HEAD_DIM = 128
Q_LORA_RANK = 512
KV_LORA_RANK = 256
ROPE_THETA = 10000.0
ATTN_WIDTH = N_HEADS * V_HEAD_DIM
D_FF = 5632
NORM_EPS = 1e-6

V7X_VMEM_BYTES = 64 * 1024 * 1024
V7X_LANES = 128
V7X_MXU_DIM = 256
V7X_SUBLANES = 8
HALO = V7X_SUBLANES

QK_PAD_DIM = V7X_MXU_DIM
PROJ_WIDTH = 4096
COL_CQ = 3 * HYENA_WIDTH
COL_CKV = COL_CQ + Q_LORA_RANK
COL_KPE = COL_CKV + KV_LORA_RANK


def _params(semantics, vmem_bytes, flags=None):
    return pltpu.CompilerParams(dimension_semantics=semantics, flags=flags,
                                vmem_limit_bytes=min(int(vmem_bytes), V7X_VMEM_BYTES - (4 << 20)))


def _rms(x, gain):
    return x * lax.rsqrt(jnp.mean(x * x, axis=-1, keepdims=True) + NORM_EPS) * gain


def _bdot(a, b):
    return jnp.dot(a, b, preferred_element_type=F32)


def _seq_conv3(x, w, b):
    t = x.shape[0]
    row = lax.broadcasted_iota(jnp.int32, x.shape, 0)
    prev = jnp.where(row == 0, 0.0, pltpu.roll(x, 1, 0))
    nxt = jnp.where(row == t - 1, 0.0, pltpu.roll(x, t - 1, 0))
    return prev * w[0:1] + x * w[1:2] + nxt * w[2:3] + b


def _inproj_body(x_ref, g_ref, wm_ref, wt_ref, o_ref, *, n_main, rows):
    j = pl.program_id(1)

    def project(w_ref):
        w = w_ref[...]
        for r0 in range(0, x_ref.shape[0], rows):
            xn = _rms(x_ref[r0:r0 + rows, :], g_ref[...]).astype(BF16)
            o_ref[r0:r0 + rows, :] = lax.dot_general(xn, w, (((1,), (1,)), ((), ())),
                                                     preferred_element_type=F32).astype(o_ref.dtype)

    @pl.when(j < n_main)
    def _():
        project(wm_ref)

    @pl.when(j == n_main)
    def _():
        project(wt_ref)


def _inproj(x2d, gain, w_main, w_tail):
    m, d = x2d.shape
    tm, tn = 1024, w_tail.shape[0]
    n_main = w_main.shape[0] // tn
    vmem = 2 * tm * d * 4 + tm * d * 2 + 4 * d * tn * 2 + 2 * tm * tn * 2 + tm * tn * 4 + (8 << 20)
    return pl.pallas_call(
        functools.partial(_inproj_body, n_main=n_main, rows=256),
        grid=(m // tm, n_main + 1),
        in_specs=[pl.BlockSpec((tm, d), lambda i, j: (i, 0)),
                  pl.BlockSpec((1, d), lambda i, j: (0, 0)),
                  pl.BlockSpec((tn, d), lambda i, j: (jnp.minimum(j, n_main - 1), 0)),
                  pl.BlockSpec((tn, d), lambda i, j: (0, 0))],
        out_specs=pl.BlockSpec((tm, tn), lambda i, j: (i, j)),
        out_shape=jax.ShapeDtypeStruct((m, (n_main + 1) * tn), BF16),
        compiler_params=_params(("parallel", "arbitrary"), vmem),
        name="inproj",
    )(x2d, gain, w_main, w_tail)


def _dft_mats(p):
    r = 32
    assert p == r * r
    idx = jnp.arange(p, dtype=jnp.int32)
    sub = jnp.arange(r, dtype=jnp.int32)
    ang_hi = ((idx[:, None] * (r * sub)[None, :]) % (2 * p)).astype(F32) * (math.pi / p)
    ang_lo = ((idx[:, None] * sub[None, :]) % (2 * p)).astype(F32) * (math.pi / p)
    ch, sh = jnp.cos(ang_hi)[:, :, None], jnp.sin(ang_hi)[:, :, None]
    cl, sl = jnp.cos(ang_lo)[:, None, :], jnp.sin(ang_lo)[:, None, :]
    c = (ch * cl - sh * sl).reshape(p, p)
    s = -(sh * cl + ch * sl).reshape(p, p)
    alt = jnp.where(idx % 2 == 0, 1.0, -1.0).astype(F32)
    s_fwd = jnp.where(idx[:, None] == 0, alt[None, :], s)
    s_inv = jnp.where(idx[None, :] == 0, alt[:, None], s)
    return c.astype(BF16), s_fwd.astype(BF16), s_inv.astype(BF16)


def _filter_body(z_ref, t_ref, w1_ref, b1_ref, f1_ref, w2_ref, b2_ref, f2_ref, w3f_ref, w3b_ref,
                 dl_ref, c_ref, s_ref, o_ref, hid_ref, *, p):
    hp = lax.Precision.HIGHEST

    @pl.when(pl.program_id(0) == 0)
    def _():
        h1 = jnp.sin(f1_ref[...] * (jnp.dot(z_ref[...], w1_ref[...], precision=hp,
                                            preferred_element_type=F32) + b1_ref[...]))
        hid_ref[...] = jnp.sin(f2_ref[...] * (jnp.dot(h1, w2_ref[...], precision=hp,
                                                      preferred_element_type=F32) + b2_ref[...]))

    h = hid_ref[...]
    win =jnp.exp(-t_ref[...] * dl_ref[...]) + DECAY_SHIFT
    hf = jnp.dot(h, w3f_ref[...], precision=hp, preferred_element_type=F32) * win
    hb = jnp.dot(h, w3b_ref[...], precision=hp, preferred_element_type=F32) * win
    lrow = lax.broadcasted_iota(jnp.int32, hb.shape, 0)
    hb = jnp.where(lrow == 0, 0.0, hb)

    cmat = c_ref[...]
    smat = s_ref[...]

    def fwd(x):
        hi = x.astype(BF16)
        lo = (x - hi.astype(F32)).astype(BF16)
        return (_bdot(cmat, hi) + _bdot(cmat, lo), _bdot(smat, hi) + _bdot(smat, lo))

    row = lax.broadcasted_iota(jnp.int32, (p, hf.shape[1]), 0)
    row0 = row == 0
    sigma = jnp.where(row % 2 == 1, -1.0, 1.0)

    def conj(a):
        return a[0], jnp.where(row0, a[1], -a[1])

    af0, af1 = fwd(hf[:p]), fwd(hf[p:])
    ab0, ab1 = fwd(hb[:p]), fwd(hb[p:])
    cb0 = conj(ab0)
    k0 = (af0[0] + cb0[0], af0[1] + cb0[1])
    k1 = (af1[0] + sigma * af0[0], af1[1] + sigma * af0[1])
    km1 = conj((ab1[0] + sigma * ab0[0], ab1[1] + sigma * ab0[1]))
    scale = jnp.where(row0, 0.5 / p, 1.0 / p)
    for i, a in enumerate((k0, k1, km1)):
        o_ref[2 * i] = a[0] * scale
        o_ref[2 * i + 1] = a[1] * scale


def _filter_spectra(z, t, w1, b1, f1, w2, b2, f2, w3, deltas, cmat, smat, p):
    l = z.shape[0]
    c = HYENA_WIDTH
    tc = 256
    nb = c // tc
    full = lambda a: pl.BlockSpec(a.shape, lambda j: (0,) * a.ndim)
    return pl.pallas_call(
        functools.partial(_filter_body, p=p),
        grid=(nb,),
        in_specs=[full(z), full(t), full(w1), full(b1), full(f1), full(w2), full(b2), full(f2),
                  pl.BlockSpec((FILTER_HIDDEN, tc), lambda j: (0, j)),
                  pl.BlockSpec((FILTER_HIDDEN, tc), lambda j: (0, j + nb)),
                  pl.BlockSpec((1, tc), lambda j: (0, j)),
                  full(cmat), full(smat)],
        out_specs=pl.BlockSpec((6, p, tc), lambda j: (0, 0, j)),
        out_shape=jax.ShapeDtypeStruct((6, p, c), F32),
        scratch_shapes=[pltpu.VMEM((l, FILTER_HIDDEN), F32)],
        compiler_params=_params(("arbitrary",), 40 << 20),
        name="filt",
    )(z, t, w1, b1, f1, w2, b2, f2, w3, w3, deltas, cmat, smat)


def _hyena_body(x0_ref, x1_ref, v_ref, cw_ref, cb_ref, ks_ref, hb_ref, c_ref, sf_ref, si_ref, o_ref, *, p):
    cmat = c_ref[...]
    sfwd = sf_ref[...]
    sinv = si_ref[...]
    seq = 2 * p
    row0 = lax.broadcasted_iota(jnp.int32, (p, o_ref.shape[1]), 0) == 0

    def cmul(i, uu):
        kre, kim = ks_ref[2 * i], ks_ref[2 * i + 1]
        ii = kim * uu[1]
        return (kre * uu[0] - jnp.where(row0, 0.0, ii),
                jnp.where(row0, ii, kre * uu[1] + kim * uu[0]))

    def inv(a, b):
        return _bdot(cmat, (a[0] + b[0]).astype(BF16)) + _bdot(sinv, (a[1] + b[1]).astype(BF16))

    for r0 in range(0, o_ref.shape[0], seq):
        rs = slice(r0, r0 + seq)
        x0 = _seq_conv3(x0_ref[rs, :].astype(F32), cw_ref[0], cb_ref[0])
        x1 = _seq_conv3(x1_ref[rs, :].astype(F32), cw_ref[1], cb_ref[1])
        v = _seq_conv3(v_ref[rs, :].astype(F32), cw_ref[2], cb_ref[2])
        u = x1 * v
        ub = u.astype(BF16)
        spec = [(_bdot(cmat, ub[j * p:(j + 1) * p]), _bdot(sfwd, ub[j * p:(j + 1) * p])) for j in range(2)]
        y_lo = inv(cmul(0, spec[0]), cmul(2, spec[1]))
        y_hi = inv(cmul(1, spec[0]), cmul(0, spec[1]))
        y = jnp.concatenate([y_lo, y_hi], axis=0)
        o_ref[rs, :] = (x0 * (y + u * hb_ref[...])).astype(o_ref.dtype)


def _hyena(proj, conv_w, conv_b, kspec, hbias, cmat, sfwd, sinv, batch, seq, p):
    c = HYENA_WIDTH
    tc = 256
    nb = c // tc
    nseq = 1
    xspec = lambda off: pl.BlockSpec((nseq * seq, tc), lambda j, b: (b, j + off * nb))
    full = lambda a: pl.BlockSpec(a.shape, lambda j, b: (0,) * a.ndim)
    vmem = (nseq * 6 * seq * tc * 2 + 2 * 6 * p * tc * 4 + 6 * p * p * 2 + nseq * 2 * seq * tc * 2
            + nseq * 6 * seq * tc * 4 + (6 << 20))
    return pl.pallas_call(
        functools.partial(_hyena_body, p=p),
        grid=(nb, batch // nseq),
        in_specs=[xspec(0), xspec(1), xspec(2),
                  pl.BlockSpec((3, CONV_WIDTH, tc), lambda j, b: (0, 0, j)),
                  pl.BlockSpec((3, 1, tc), lambda j, b: (0, 0, j)),
                  pl.BlockSpec((6, p, tc), lambda j, b: (0, 0, j)),
                  pl.BlockSpec((1, tc), lambda j, b: (0, j)),
                  full(cmat), full(sfwd), full(sinv)],
        out_specs=pl.BlockSpec((nseq * seq, tc), lambda j, b: (b, j)),
        out_shape=jax.ShapeDtypeStruct((batch * seq, c), BF16),
        compiler_params=_params(("parallel", "parallel"), vmem),
        name="hyena",
    )(proj, proj, proj, conv_w, conv_b, kspec, hbias, cmat, sfwd, sinv)


def _qkv_body(cq_ref, ckv_ref, kpe_ref, rot_ref, cos_ref, sin_ref, gq_ref, gkv_ref, wa_ref, wb_ref, wkv_ref,
              q_ref, k_ref, v_ref, *, rows):
    scale = (QK_NOPE_DIM + QK_ROPE_DIM) ** -0.5 * math.log2(math.e)
    n = cq_ref.shape[0] // rows

    def project(c):
        rs = slice(c * rows, (c + 1) * rows)
        cqn = _rms(cq_ref[rs, :].astype(F32), gq_ref[...]).astype(BF16)
        ckvn = _rms(ckv_ref[rs, :].astype(F32), gkv_ref[...]).astype(BF16)
        return (_bdot(cqn, wa_ref[...]),
                _bdot(cqn, wb_ref[...]),
                _bdot(ckvn, wkv_ref[...]),
                _bdot(kpe_ref[rs, :], rot_ref[...]))

    nxt = project(0)
    for c in range(n):
        qa, qb, kv, kpe_rot = nxt
        if c + 1 < n:
            nxt = project(c + 1)
        rs = slice(c * rows, (c + 1) * rows)
        cos = cos_ref[rs, :]
        sin = sin_ref[rs, :]
        kpe = (kpe_ref[rs, :].astype(F32) * cos + kpe_rot * sin).astype(BF16)
        for h in range(N_HEADS):
            a = h * QK_PAD_DIM
            r = h * V7X_LANES
            q_ref[0, h, rs, 0:128] = (qa[:, a:a + 128] * scale).astype(BF16)
            q_ref[0, h, rs, 128:256] = ((qa[:, a + 128:a + 256] * cos + qb[:, r:r + 128] * sin) * scale).astype(BF16)
            k_ref[0, h, rs, 0:128] = kv[:, a:a + 128].astype(BF16)
            k_ref[0, h, rs, 128:256] = kpe
            v_ref[0, h, rs, :] = kv[:, a + 128:a + 256].astype(BF16)


def _qkv(proj, rot, cos_t, sin_t, gq, gkv, wa, wb, wkv, batch, seq):
    tm = 512
    ns = seq // tm
    full = lambda a: pl.BlockSpec(a.shape, lambda b, i: (0,) * a.ndim)
    col = lambda width, off: pl.BlockSpec((tm, width), lambda b, i: (b * ns + i, off // width))
    hd = lambda w: pl.BlockSpec((1, N_HEADS, tm, w), lambda b, i: (b, 0, i, 0))
    return pl.pallas_call(
        functools.partial(_qkv_body, rows=tm // 2),
        grid=(batch, ns),
        in_specs=[col(Q_LORA_RANK, COL_CQ), col(KV_LORA_RANK, COL_CKV),
                  col(V7X_LANES, COL_KPE), full(rot),
                  pl.BlockSpec((tm, V7X_LANES), lambda b, i: (i, 0)),
                  pl.BlockSpec((tm, V7X_LANES), lambda b, i: (i, 0)),
                  full(gq), full(gkv), full(wa), full(wb), full(wkv)],
        out_specs=[hd(QK_PAD_DIM), hd(QK_PAD_DIM), hd(V_HEAD_DIM)],
        out_shape=[jax.ShapeDtypeStruct((batch, N_HEADS, seq, QK_PAD_DIM), BF16),
                   jax.ShapeDtypeStruct((batch, N_HEADS, seq, QK_PAD_DIM), BF16),
                   jax.ShapeDtypeStruct((batch, N_HEADS, seq, V_HEAD_DIM), BF16)],
        compiler_params=_params(("parallel", "parallel"), 48 << 20),
        name="qkv",
    )(proj, proj, proj, rot, cos_t, sin_t, gq, gkv, wa, wb, wkv)


def _attn_body(q_ref, k_ref, v_ref, o_ref, *, tq):
    heads, seq = q_ref.shape[1], q_ref.shape[2]
    work = [(h, c) for h in range(heads) for c in range(seq // tq)]

    def scores(h, c):
        return lax.dot_general(q_ref[0, h, c * tq:(c + 1) * tq, :], k_ref[0, h], (((1,), (1,)), ((), ())),
                               preferred_element_type=F32)

    s_next = scores(*work[0])
    for i, (h, c) in enumerate(work):
        s = s_next
        if i + 1 < len(work):
            s_next = scores(*work[i + 1])
        m = jnp.max(s, axis=-1, keepdims=True)
        e = jnp.exp2(s - m)
        l = jnp.sum(e, axis=-1, keepdims=True)
        o = _bdot(e.astype(BF16), v_ref[0, h])
        o_ref[0, c * tq:(c + 1) * tq, h * V_HEAD_DIM:(h + 1) * V_HEAD_DIM] = (o / l).astype(o_ref.dtype)


def _attn(q, k, v):
    batch, heads, seq, _ = q.shape
    tq = 256
    hps = 2
    hspec = lambda w: pl.BlockSpec((1, hps, seq, w), lambda b, h: (b, h, 0, 0))
    return pl.pallas_call(
        functools.partial(_attn_body, tq=tq),
        grid=(batch, heads // hps),
        in_specs=[hspec(QK_PAD_DIM), hspec(QK_PAD_DIM), hspec(V_HEAD_DIM)],
        out_specs=pl.BlockSpec((1, seq, hps * V_HEAD_DIM), lambda b, h: (b, 0, h)),
        out_shape=jax.ShapeDtypeStruct((batch, seq, heads * V_HEAD_DIM), BF16),
        compiler_params=_params(("parallel", "parallel"), 48 << 20),
        name="attn",
    )(q, k, v)


def _outproj_body(yh_ref, ya_ref, gh_ref, ga_ref, w_ref, x_ref, gpm_ref, gpf_ref, h_ref, hn_ref, *, rows):
    n = x_ref.shape[0] // rows

    def mix(c):
        rs = slice(c * rows, (c + 1) * rows)
        a = _rms(yh_ref[rs, :].astype(F32), gh_ref[...]).astype(BF16)
        b = _rms(ya_ref[rs, :].astype(F32), ga_ref[...]).astype(BF16)
        return _bdot(a, w_ref[0:HYENA_WIDTH, :]) + _bdot(b, w_ref[HYENA_WIDTH:, :])

    nxt = mix(0)
    for c in range(n):
        mixed = nxt
        if c + 1 < n:
            nxt = mix(c + 1)
        rs = slice(c * rows, (c + 1) * rows)
        h = x_ref[rs, :] + _rms(mixed, gpm_ref[...])
        h_ref[rs, :] = h
        hn_ref[rs, :] = _rms(h, gpf_ref[...]).astype(BF16)


def _outproj(yh, ya, gh, ga, w_out, x2d, gpm, gpf):
    m, d = x2d.shape
    tm = 512
    full = lambda a: pl.BlockSpec(a.shape, lambda i: (0,) * a.ndim)
    rows = lambda w: pl.BlockSpec((tm, w), lambda i: (i, 0))
    vmem = 2 * 2 * tm * HYENA_WIDTH * 2 + 2 * d * d * 2 + 2 * tm * d * (4 + 4 + 2) + 4 * tm * d * 4 + (6 << 20)
    return pl.pallas_call(
        functools.partial(_outproj_body, rows=tm // 2),
        grid=(m // tm,),
        in_specs=[rows(HYENA_WIDTH), rows(ATTN_WIDTH), full(gh), full(ga), full(w_out), rows(d),
                  full(gpm), full(gpf)],
        out_specs=[rows(d), rows(d)],
        out_shape=[jax.ShapeDtypeStruct((m, d), F32), jax.ShapeDtypeStruct((m, d), BF16)],
        compiler_params=_params(("parallel",), vmem),
        name="outproj",
    )(yh, ya, gh, ga, w_out, x2d, gpm, gpf)


def _ffn_up_body(hn_ref, wg_ref, wu_ref, cw_ref, cb_ref, o_ref, wgb_ref, wub_ref, *, cols):
    @pl.when(pl.program_id(1) == 0)
    def _():
        wgb_ref[...] = wg_ref[...].astype(BF16)
        wub_ref[...] = wu_ref[...].astype(BF16)

    hn = hn_ref[...]
    for c0 in range(0, o_ref.shape[1], cols):
        cs = slice(c0, c0 + cols)
        g = _seq_conv3(_bdot(hn, wgb_ref[:, cs]), cw_ref[:, cs], cb_ref[:, cs])
        u = _bdot(hn, wub_ref[:, cs])
        gelu = 0.5 * g * (1.0 + jnp.tanh(math.sqrt(2.0 / math.pi) * (g + 0.044715 * (g * g * g))))
        o_ref[:, cs] = (gelu * u).astype(o_ref.dtype)


def _ffn_up(hn, w_up, conv_w, conv_b, batch, seq):
    d = hn.shape[1]
    tn = 512
    nb = D_FF // tn
    wbf = pltpu.VMEM((d, tn), BF16)
    vmem = (2 * seq * d * 2 + 2 * 2 * d * tn * 4 + 2 * d * tn * 2 + 2 * seq * tn * 2 + 8 * seq * tn * 4
            + (6 << 20))
    return pl.pallas_call(
        functools.partial(_ffn_up_body, cols=256),
        grid=(nb, batch),
        in_specs=[pl.BlockSpec((seq, d), lambda j, b: (b, 0)),
                  pl.BlockSpec((d, tn), lambda j, b: (0, j)),
                  pl.BlockSpec((d, tn), lambda j, b: (0, j + nb)),
                  pl.BlockSpec((CONV_WIDTH, tn), lambda j, b: (0, j)),
                  pl.BlockSpec((1, tn), lambda j, b: (0, j))],
        out_specs=pl.BlockSpec((seq, tn), lambda j, b: (b, j)),
        out_shape=jax.ShapeDtypeStruct((batch * seq, D_FF), BF16),
        scratch_shapes=[wbf, wbf],
        compiler_params=_params(("parallel", "arbitrary"), vmem),
        name="ffn_up",
    )(hn, w_up, w_up, conv_w, conv_b)


def _ffn_down_body(a_ref, w_ref, h_ref, g_ref, o_ref, *, rows):
    for r0 in range(0, a_ref.shape[0], rows):
        rs = slice(r0, r0 + rows)
        o_ref[rs, :] = h_ref[rs, :] + _rms(_bdot(a_ref[rs, :], w_ref[...]), g_ref[...])


def _ffn_down(act, w_down, h, gain):
    m, d = h.shape
    k = act.shape[1]
    tm = 512
    vmem = k * d * 2 + 2 * tm * k * 2 + 2 * 2 * tm * d * 4 + tm * d * 4 + (4 << 20)
    return pl.pallas_call(
        functools.partial(_ffn_down_body, rows=tm // 2),
        grid=(m // tm,),
        in_specs=[pl.BlockSpec((tm, k), lambda i: (i, 0)),
                  pl.BlockSpec((k, d), lambda i: (0, 0), pipeline_mode=pl.Buffered(1)),
                  pl.BlockSpec((tm, d), lambda i: (i, 0)),
                  pl.BlockSpec((1, d), lambda i: (0, 0))],
        out_specs=pl.BlockSpec((tm, d), lambda i: (i, 0)),
        out_shape=jax.ShapeDtypeStruct((m, d), F32),
        compiler_params=_params(("arbitrary",), vmem),
        name="ffn_down",
    )(act, w_down, h, gain)


def _position_features(l):
    t = jnp.linspace(0.0, 1.0, l, dtype=F32)[:, None]
    bands = (FILTER_EMB_DIM - 1) // 2
    w = 2.0 * math.pi * jnp.arange(l, dtype=F32) / l
    f = jnp.linspace(1e-4, bands - 1, bands, dtype=F32)
    ang = w[:, None] * f[None, :]
    return t, jnp.concatenate([t, jnp.cos(ang), -jnp.sin(ang)], axis=-1)


def _layer(h, l, seq, prm):
    (pre_mix_gain, w_in, hyena_conv_w, hyena_conv_b, filt_w1, filt_b1, filt_freq1, filt_w2, filt_b2,
     filt_freq2, filt_w3, hyena_bias, q_norm_gain, w_uq, kv_norm_gain, w_ukv, hyena_out_gain, attn_out_gain,
     w_out, post_mix_gain, pre_ffn_gain, w_up, ffn_conv_w, ffn_conv_b, w_down, post_ffn_gain) = prm
    batch = h.shape[0]
    p = seq // 2
    x2d = h.reshape(batch * seq, D_MODEL)
    row = lambda a: a[l][None, :].astype(F32)

    wt = jnp.swapaxes(w_in[l], 0, 1)
    w_tail = jnp.pad(wt[COL_CQ:].astype(BF16), ((0, PROJ_WIDTH - wt.shape[0]), (0, 0)))
    proj = _inproj(x2d, row(pre_mix_gain), wt[:COL_CQ].astype(BF16), w_tail)

    cmat, sfwd, sinv = _dft_mats(p)
    t, z = _position_features(seq)
    zp = jnp.pad(z, ((0, 0), (0, V7X_LANES - FILTER_EMB_DIM)))
    w1p = jnp.pad(filt_w1[l], ((0, V7X_LANES - FILTER_EMB_DIM), (0, 0)))
    max_decay = math.log(DECAY_TARGET) / FAST_DECAY_PCT
    min_decay = math.log(DECAY_TARGET) / SLOW_DECAY_PCT
    deltas = jnp.abs(jnp.linspace(min_decay, max_decay, HYENA_WIDTH, dtype=F32))[None, :]
    kspec = _filter_spectra(zp, t, w1p, row(filt_b1), row(filt_freq1), filt_w2[l], row(filt_b2),
                            row(filt_freq2), filt_w3[l], deltas, cmat, sfwd, p)
    conv_w = hyena_conv_w[l].reshape(CONV_WIDTH, 3, HYENA_WIDTH).transpose(1, 0, 2)
    conv_b = hyena_conv_b[l].reshape(3, 1, HYENA_WIDTH)
    y_hyena = _hyena(proj, conv_w, conv_b, kspec, row(hyena_bias), cmat, sfwd, sinv, batch, seq, p)

    pos = jnp.arange(seq, dtype=F32)
    inv_freq = 1.0 / (ROPE_THETA ** (jnp.arange(0, QK_ROPE_DIM, 2, dtype=F32) / QK_ROPE_DIM))
    ang = pos[:, None] * inv_freq[None, :]
    ang = jnp.concatenate([ang, ang], axis=-1)
    lpad = ((0, 0), (0, V7X_LANES - QK_ROPE_DIM))
    cos_t = jnp.pad(jnp.cos(ang), lpad)
    sin_t = jnp.pad(jnp.sin(ang), lpad)
    dqk = QK_NOPE_DIM + QK_ROPE_DIM
    wq = w_uq[l].reshape(Q_LORA_RANK, N_HEADS, dqk)
    wa = jnp.pad(wq, ((0, 0), (0, 0), (0, QK_PAD_DIM - dqk))).reshape(Q_LORA_RANK, N_HEADS * QK_PAD_DIM)
    wq_pe = wq[:, :, QK_NOPE_DIM:]
    wq_rot = jnp.concatenate([-wq_pe[..., QK_ROPE_DIM // 2:], wq_pe[..., :QK_ROPE_DIM // 2]], axis=-1)
    wb = jnp.pad(wq_rot, ((0, 0), (0, 0), (0, V7X_LANES - QK_ROPE_DIM))).reshape(Q_LORA_RANK, N_HEADS * V7X_LANES)
    ri = jnp.arange(V7X_LANES, dtype=jnp.int32)[:, None]
    rj = jnp.arange(V7X_LANES, dtype=jnp.int32)[None, :]
    half = QK_ROPE_DIM // 2
    rot = (jnp.where((ri == rj + half) & (rj < half), -1.0, 0.0)
           + jnp.where((ri == rj - half) & (rj >= half) & (rj < QK_ROPE_DIM), 1.0, 0.0)).astype(BF16)
    q, k, v = _qkv(proj, rot, cos_t, sin_t, row(q_norm_gain), row(kv_norm_gain), wa.astype(BF16),
                   wb.astype(BF16), w_ukv[l].astype(BF16), batch, seq)
    y_attn = _attn(q, k, v).reshape(batch * seq, ATTN_WIDTH)

    h2d, hn = _outproj(y_hyena, y_attn, row(hyena_out_gain), row(attn_out_gain), w_out[l].astype(BF16), x2d,
                       row(post_mix_gain), row(pre_ffn_gain))

    act = _ffn_up(hn, w_up[l], ffn_conv_w[l], row(ffn_conv_b), batch, seq)
    out = _ffn_down(act, w_down[l].astype(BF16), h2d, row(post_ffn_gain))
    return out.reshape(batch, seq, D_MODEL)


def kernel(x, pre_mix_gain, w_in, hyena_conv_w, hyena_conv_b, filt_w1, filt_b1, filt_freq1, filt_w2, filt_b2,
           filt_freq2, filt_w3, hyena_bias, q_norm_gain, w_uq, kv_norm_gain, w_ukv, hyena_out_gain,
           attn_out_gain, w_out, post_mix_gain, pre_ffn_gain, w_up, ffn_conv_w, ffn_conv_b, w_down,
           post_ffn_gain):
    prm = (pre_mix_gain, w_in, hyena_conv_w, hyena_conv_b, filt_w1, filt_b1, filt_freq1, filt_w2, filt_b2,
           filt_freq2, filt_w3, hyena_bias, q_norm_gain, w_uq, kv_norm_gain, w_ukv, hyena_out_gain,
           attn_out_gain, w_out, post_mix_gain, pre_ffn_gain, w_up, ffn_conv_w, ffn_conv_b, w_down,
           post_ffn_gain)
    seq = x.shape[1]
    h = x
    for l in range(w_in.shape[0]):
        h = _layer(h, l, seq, prm)
    return h
```

```python
import functools
import math

import jax
import jax.numpy as jnp
from jax import lax
from jax.experimental import pallas as pl
from jax.experimental.pallas import tpu as pltpu

F32 = jnp.float32
BF16 = jnp.bfloat16

D_MODEL = 2048
HYENA_WIDTH = 1024
CONV_WIDTH = 3
FILTER_EMB_DIM = 33
FILTER_HIDDEN = 64
DECAY_TARGET = 1e-2
FAST_DECAY_PCT = 0.3
SLOW_DECAY_PCT = 1.5
DECAY_SHIFT = 0.05
N_HEADS = 8
QK_NOPE_DIM = 128
QK_ROPE_DIM = 64
V_HEAD_DIM = 128
Q_LORA_RANK = 512
KV_LORA_RANK = 256
ROPE_THETA = 10000.0
ATTN_WIDTH = N_HEADS * V_HEAD_DIM
D_FF = 5632
NORM_EPS = 1e-6

V7X_VMEM_BYTES = 64 * 1024 * 1024
V7X_LANES = 128
V7X_MXU_DIM = 256
V7X_SUBLANES = 8
HALO_BF16 = 2 * V7X_SUBLANES

QK_PAD_DIM = V7X_MXU_DIM
PROJ_WIDTH = 4096
COL_CQ = 3 * HYENA_WIDTH
COL_CKV = COL_CQ + Q_LORA_RANK
COL_KPE = COL_CKV + KV_LORA_RANK


def _params(semantics, vmem_bytes, flags=None):
    return pltpu.CompilerParams(dimension_semantics=semantics, flags=flags,
                                vmem_limit_bytes=min(int(vmem_bytes), V7X_VMEM_BYTES - (4 << 20)))


def _rms(x, gain):
    return x * lax.rsqrt(jnp.mean(x * x, axis=-1, keepdims=True) + NORM_EPS) * gain


def _bdot(a, b):
    return jnp.dot(a, b, preferred_element_type=F32)


def _chunk_bounds(r0, rows, seq):
    return max(r0 - HALO_BF16, 0), min(r0 + rows + HALO_BF16, seq)


def _seq_conv3_chunk(xe, w, b, r0, rows, seq):
    lo, hi = _chunk_bounds(r0, rows, seq)
    n = hi - lo
    prev = pltpu.roll(xe, 1, 0)
    nxt = pltpu.roll(xe, n - 1, 0)
    if lo == 0 or hi == seq:
        row = lax.broadcasted_iota(jnp.int32, xe.shape, 0)
        if lo == 0:
            prev = jnp.where(row == 0, 0.0, prev)
        if hi == seq:
            nxt = jnp.where(row == n - 1, 0.0, nxt)
    return (prev * w[0:1] + xe * w[1:2] + nxt * w[2:3] + b)[r0 - lo:r0 - lo + rows]


def _inproj_body(x_ref, g_ref, wm_ref, wt_ref, o_ref, *, n_main, rows):
    j = pl.program_id(1)

    def project(w_ref):
        w = w_ref[...]
        for r0 in range(0, x_ref.shape[0], rows):
            xn = _rms(x_ref[r0:r0 + rows, :], g_ref[...]).astype(BF16)
            o_ref[r0:r0 + rows, :] = lax.dot_general(xn, w, (((1,), (1,)), ((), ())),
                                                     preferred_element_type=F32).astype(o_ref.dtype)

    @pl.when(j < n_main)
    def _():
        project(wm_ref)

    @pl.when(j == n_main)
    def _():
        project(wt_ref)


def _inproj(x2d, gain, w_main, w_tail):
    m, d = x2d.shape
    tm, tn = 1024, w_tail.shape[0]
    n_main = w_main.shape[0] // tn
    vmem = 2 * tm * d * 4 + tm * d * 2 + 4 * d * tn * 2 + 2 * tm * tn * 2 + tm * tn * 4 + (8 << 20)
    return pl.pallas_call(
        functools.partial(_inproj_body, n_main=n_main, rows=256),
        grid=(m // tm, n_main + 1),
        in_specs=[pl.BlockSpec((tm, d), lambda i, j: (i, 0)),
                  pl.BlockSpec((1, d), lambda i, j: (0, 0)),
                  pl.BlockSpec((tn, d), lambda i, j: (jnp.minimum(j, n_main - 1), 0)),
                  pl.BlockSpec((tn, d), lambda i, j: (0, 0))],
        out_specs=pl.BlockSpec((tm, tn), lambda i, j: (i, j)),
        out_shape=jax.ShapeDtypeStruct((m, (n_main + 1) * tn), BF16),
        compiler_params=_params(("parallel", "arbitrary"), vmem),
        name="inproj",
    )(x2d, gain, w_main, w_tail)


def _dft_mats(p):
    r = 32
    assert p == r * r
    idx = jnp.arange(p, dtype=jnp.int32)
    sub = jnp.arange(r, dtype=jnp.int32)
    ang_hi = ((idx[:, None] * (r * sub)[None, :]) % (2 * p)).astype(F32) * (math.pi / p)
    ang_lo = ((idx[:, None] * sub[None, :]) % (2 * p)).astype(F32) * (math.pi / p)
    ch, sh = jnp.cos(ang_hi)[:, :, None], jnp.sin(ang_hi)[:, :, None]
    cl, sl = jnp.cos(ang_lo)[:, None, :], jnp.sin(ang_lo)[:, None, :]
    c = (ch * cl - sh * sl).reshape(p, p)
    s = -(sh * cl + ch * sl).reshape(p, p)
    alt = jnp.where(idx % 2 == 0, 1.0, -1.0).astype(F32)
    s_fwd = jnp.where(idx[:, None] == 0, alt[None, :], s)
    s_inv = jnp.where(idx[None, :] == 0, alt[:, None], s)
    return c.astype(BF16), s_fwd.astype(BF16), s_inv.astype(BF16)


def _filter_body(z_ref, t_ref, w1_ref, b1_ref, f1_ref, w2_ref, b2_ref, f2_ref, w3f_ref, w3b_ref,
                 dl_ref, c_ref, s_ref, o_ref, hid_ref, *, p):
    hp = lax.Precision.HIGHEST

    @pl.when(pl.program_id(0) == 0)
    def _():
        h1 = jnp.sin(f1_ref[...] * (jnp.dot(z_ref[...], w1_ref[...], precision=hp,
                                            preferred_element_type=F32) + b1_ref[...]))
        hid_ref[...] = jnp.sin(f2_ref[...] * (jnp.dot(h1, w2_ref[...], precision=hp,
                                                      preferred_element_type=F32) + b2_ref[...]))

    h = hid_ref[...]
    win =jnp.exp(-t_ref[...] * dl_ref[...]) + DECAY_SHIFT
    hf = jnp.dot(h, w3f_ref[...], precision=hp, preferred_element_type=F32) * win
    hb = jnp.dot(h, w3b_ref[...], precision=hp, preferred_element_type=F32) * win
    lrow = lax.broadcasted_iota(jnp.int32, hb.shape, 0)
    hb = jnp.where(lrow == 0, 0.0, hb)

    cmat = c_ref[...]
    smat = s_ref[...]

    def fwd(x):
        hi = x.astype(BF16)
        lo = (x - hi.astype(F32)).astype(BF16)
        return (_bdot(cmat, hi) + _bdot(cmat, lo), _bdot(smat, hi) + _bdot(smat, lo))

    row = lax.broadcasted_iota(jnp.int32, (p, hf.shape[1]), 0)
    row0 = row == 0
    sigma = jnp.where(row % 2 == 1, -1.0, 1.0)

    def conj(a):
        return a[0], jnp.where(row0, a[1], -a[1])

    af0, af1 = fwd(hf[:p]), fwd(hf[p:])
    ab0, ab1 = fwd(hb[:p]), fwd(hb[p:])
    cb0 = conj(ab0)
    k0 = (af0[0] + cb0[0], af0[1] + cb0[1])
    k1 = (af1[0] + sigma * af0[0], af1[1] + sigma * af0[1])
    km1 = conj((ab1[0] + sigma * ab0[0], ab1[1] + sigma * ab0[1]))
    scale = jnp.where(row0, 0.5 / p, 1.0 / p)
    for i, a in enumerate((k0, k1, km1)):
        o_ref[2 * i] = a[0] * scale
        o_ref[2 * i + 1] = a[1] * scale


def _filter_spectra(z, t, w1, b1, f1, w2, b2, f2, w3, deltas, cmat, smat, p):
    l = z.shape[0]
    c = HYENA_WIDTH
    tc = 256
    nb = c // tc
    full = lambda a: pl.BlockSpec(a.shape, lambda j: (0,) * a.ndim)
    return pl.pallas_call(
        functools.partial(_filter_body, p=p),
        grid=(nb,),
        in_specs=[full(z), full(t), full(w1), full(b1), full(f1), full(w2), full(b2), full(f2),
                  pl.BlockSpec((FILTER_HIDDEN, tc), lambda j: (0, j)),
                  pl.BlockSpec((FILTER_HIDDEN, tc), lambda j: (0, j + nb)),
                  pl.BlockSpec((1, tc), lambda j: (0, j)),
                  full(cmat), full(smat)],
        out_specs=pl.BlockSpec((6, p, tc), lambda j: (0, 0, j)),
        out_shape=jax.ShapeDtypeStruct((6, p, c), F32),
        scratch_shapes=[pltpu.VMEM((l, FILTER_HIDDEN), F32)],
        compiler_params=_params(("arbitrary",), 40 << 20),
        name="filt",
    )(z, t, w1, b1, f1, w2, b2, f2, w3, w3, deltas, cmat, smat)


def _hyena_body(x0_ref, x1_ref, v_ref, cw_ref, cb_ref, ks_ref, hb_ref, c_ref, sf_ref, si_ref, o_ref, *, p):
    cmat = c_ref[...]
    sfwd = sf_ref[...]
    sinv = si_ref[...]
    seq = 2 * p
    row0 = lax.broadcasted_iota(jnp.int32, (p, o_ref.shape[1]), 0) == 0

    def cmul(i, uu):
        kre, kim = ks_ref[2 * i], ks_ref[2 * i + 1]
        ii = kim * uu[1]
        return (kre * uu[0] - jnp.where(row0, 0.0, ii),
                jnp.where(row0, ii, kre * uu[1] + kim * uu[0]))

    def inv(a, b):
        return _bdot(cmat, (a[0] + b[0]).astype(BF16)) + _bdot(sinv, (a[1] + b[1]).astype(BF16))

    def conv(x_ref, i, r0):
        lo, hi = _chunk_bounds(r0, p, seq)
        return _seq_conv3_chunk(x_ref[lo:hi, :].astype(F32), cw_ref[i], cb_ref[i], r0, p, seq)

    us, spec = [], []
    for j in range(2):
        u = conv(x1_ref, 1, j * p) * conv(v_ref, 2, j * p)
        ub = u.astype(BF16)
        us.append(u)
        spec.append((_bdot(cmat, ub), _bdot(sfwd, ub)))
    for j, (ia, ib) in enumerate(((0, 2), (1, 0))):
        y = inv(cmul(ia, spec[0]), cmul(ib, spec[1]))
        x0 = conv(x0_ref, 0, j * p)
        o_ref[j * p:(j + 1) * p, :] = (x0 * (y + us[j] * hb_ref[...])).astype(o_ref.dtype)


def _hyena(proj, conv_w, conv_b, kspec, hbias, cmat, sfwd, sinv, batch, seq, p):
    c = HYENA_WIDTH
    tc = 256
    nb = c // tc
    nseq = 1
    xspec = lambda off: pl.BlockSpec((nseq * seq, tc), lambda j, b: (b, j + off * nb))
    full = lambda a: pl.BlockSpec(a.shape, lambda j, b: (0,) * a.ndim)
    vmem = (nseq * 6 * seq * tc * 2 + 2 * 6 * p * tc * 4 + 6 * p * p * 2 + nseq * 2 * seq * tc * 2
            + nseq * 6 * seq * tc * 4 + (6 << 20))
    return pl.pallas_call(
        functools.partial(_hyena_body, p=p),
        grid=(nb, batch // nseq),
        in_specs=[xspec(0), xspec(1), xspec(2),
                  pl.BlockSpec((3, CONV_WIDTH, tc), lambda j, b: (0, 0, j)),
                  pl.BlockSpec((3, 1, tc), lambda j, b: (0, 0, j)),
                  pl.BlockSpec((6, p, tc), lambda j, b: (0, 0, j)),
                  pl.BlockSpec((1, tc), lambda j, b: (0, j)),
                  full(cmat), full(sfwd), full(sinv)],
        out_specs=pl.BlockSpec((nseq * seq, tc), lambda j, b: (b, j)),
        out_shape=jax.ShapeDtypeStruct((batch * seq, c), BF16),
        compiler_params=_params(("parallel", "parallel"), vmem),
        name="hyena",
    )(proj, proj, proj, conv_w, conv_b, kspec, hbias, cmat, sfwd, sinv)


def _qkv_body(cq_ref, ckv_ref, kpe_ref, rot_ref, cos_ref, sin_ref, gq_ref, gkv_ref, wa_ref, wb_ref, wkv_ref,
              q_ref, k_ref, v_ref, *, rows):
    scale = (QK_NOPE_DIM + QK_ROPE_DIM) ** -0.5 * math.log2(math.e)
    n = cq_ref.shape[0] // rows

    def project(c):
        rs = slice(c * rows, (c + 1) * rows)
        cqn = _rms(cq_ref[rs, :].astype(F32), gq_ref[...]).astype(BF16)
        ckvn = _rms(ckv_ref[rs, :].astype(F32), gkv_ref[...]).astype(BF16)
        return (_bdot(cqn, wa_ref[...]),
                _bdot(cqn, wb_ref[...]),
                _bdot(ckvn, wkv_ref[...]),
                _bdot(kpe_ref[rs, :], rot_ref[...]))

    nxt = project(0)
    for c in range(n):
        qa, qb, kv, kpe_rot = nxt
        if c + 1 < n:
            nxt = project(c + 1)
        rs = slice(c * rows, (c + 1) * rows)
        cos = cos_ref[rs, :]
        sin = sin_ref[rs, :]
        kpe = (kpe_ref[rs, :].astype(F32) * cos + kpe_rot * sin).astype(BF16)
        for h in range(N_HEADS):
            a = h * QK_PAD_DIM
            r = h * V7X_LANES
            q_ref[0, h, rs, 0:128] = (qa[:, a:a + 128] * scale).astype(BF16)
            q_ref[0, h, rs, 128:256] = ((qa[:, a + 128:a + 256] * cos + qb[:, r:r + 128] * sin) * scale).astype(BF16)
            k_ref[0, h, rs, 0:128] = kv[:, a:a + 128].astype(BF16)
            k_ref[0, h, rs, 128:256] = kpe
            v_ref[0, h, rs, :] = kv[:, a + 128:a + 256].astype(BF16)


def _qkv(proj, rot, cos_t, sin_t, gq, gkv, wa, wb, wkv, batch, seq):
    tm = 512
    ns = seq // tm
    full = lambda a: pl.BlockSpec(a.shape, lambda b, i: (0,) * a.ndim)
    col = lambda width, off: pl.BlockSpec((tm, width), lambda b, i: (b * ns + i, off // width))
    hd = lambda w: pl.BlockSpec((1, N_HEADS, tm, w), lambda b, i: (b, 0, i, 0))
    return pl.pallas_call(
        functools.partial(_qkv_body, rows=tm // 2),
        grid=(batch, ns),
        in_specs=[col(Q_LORA_RANK, COL_CQ), col(KV_LORA_RANK, COL_CKV),
                  col(V7X_LANES, COL_KPE), full(rot),
                  pl.BlockSpec((tm, V7X_LANES), lambda b, i: (i, 0)),
                  pl.BlockSpec((tm, V7X_LANES), lambda b, i: (i, 0)),
                  full(gq), full(gkv), full(wa), full(wb), full(wkv)],
        out_specs=[hd(QK_PAD_DIM), hd(QK_PAD_DIM), hd(V_HEAD_DIM)],
        out_shape=[jax.ShapeDtypeStruct((batch, N_HEADS, seq, QK_PAD_DIM), BF16),
                   jax.ShapeDtypeStruct((batch, N_HEADS, seq, QK_PAD_DIM), BF16),
                   jax.ShapeDtypeStruct((batch, N_HEADS, seq, V_HEAD_DIM), BF16)],
        compiler_params=_params(("parallel", "parallel"), 48 << 20),
        name="qkv",
    )(proj, proj, proj, rot, cos_t, sin_t, gq, gkv, wa, wb, wkv)


def _attn_body(q_ref, k_ref, v_ref, o_ref, *, tq):
    heads, seq = q_ref.shape[1], q_ref.shape[2]
    work = [(h, c) for h in range(heads) for c in range(seq // tq)]

    def scores(h, c):
        return lax.dot_general(q_ref[0, h, c * tq:(c + 1) * tq, :], k_ref[0, h], (((1,), (1,)), ((), ())),
                               preferred_element_type=F32)

    s_next = scores(*work[0])
    for i, (h, c) in enumerate(work):
        s = s_next
        if i + 1 < len(work):
            s_next = scores(*work[i + 1])
        m = jnp.max(s, axis=-1, keepdims=True)
        e = jnp.exp2(s - m)
        l = jnp.sum(e, axis=-1, keepdims=True)
        o = _bdot(e.astype(BF16), v_ref[0, h])
        o_ref[0, c * tq:(c + 1) * tq, h * V_HEAD_DIM:(h + 1) * V_HEAD_DIM] = (o / l).astype(o_ref.dtype)


def _attn(q, k, v):
    batch, heads, seq, _ = q.shape
    tq = 256
    hps = 2
    hspec = lambda w: pl.BlockSpec((1, hps, seq, w), lambda b, h: (b, h, 0, 0))
    return pl.pallas_call(
        functools.partial(_attn_body, tq=tq),
        grid=(batch, heads // hps),
        in_specs=[hspec(QK_PAD_DIM), hspec(QK_PAD_DIM), hspec(V_HEAD_DIM)],
        out_specs=pl.BlockSpec((1, seq, hps * V_HEAD_DIM), lambda b, h: (b, 0, h)),
        out_shape=jax.ShapeDtypeStruct((batch, seq, heads * V_HEAD_DIM), BF16),
        compiler_params=_params(("parallel", "parallel"), 48 << 20),
        name="attn",
    )(q, k, v)


def _outproj_body(yh_ref, ya_ref, gh_ref, ga_ref, w_ref, x_ref, gpm_ref, gpf_ref, h_ref, hn_ref, *, rows):
    n = x_ref.shape[0] // rows

    def mix(c):
        rs = slice(c * rows, (c + 1) * rows)
        a = _rms(yh_ref[rs, :].astype(F32), gh_ref[...]).astype(BF16)
        b = _rms(ya_ref[rs, :].astype(F32), ga_ref[...]).astype(BF16)
        return _bdot(a, w_ref[0:HYENA_WIDTH, :]) + _bdot(b, w_ref[HYENA_WIDTH:, :])

    nxt = mix(0)
    for c in range(n):
        mixed = nxt
        if c + 1 < n:
            nxt = mix(c + 1)
        rs = slice(c * rows, (c + 1) * rows)
        h = x_ref[rs, :] + _rms(mixed, gpm_ref[...])
        h_ref[rs, :] = h
        hn_ref[rs, :] = _rms(h, gpf_ref[...]).astype(BF16)


def _outproj(yh, ya, gh, ga, w_out, x2d, gpm, gpf):
    m, d = x2d.shape
    tm = 512
    full = lambda a: pl.BlockSpec(a.shape, lambda i: (0,) * a.ndim)
    rows = lambda w: pl.BlockSpec((tm, w), lambda i: (i, 0))
    vmem = 2 * 2 * tm * HYENA_WIDTH * 2 + 2 * d * d * 2 + 2 * tm * d * (4 + 4 + 2) + 4 * tm * d * 4 + (6 << 20)
    return pl.pallas_call(
        functools.partial(_outproj_body, rows=tm // 2),
        grid=(m // tm,),
        in_specs=[rows(HYENA_WIDTH), rows(ATTN_WIDTH), full(gh), full(ga), full(w_out), rows(d),
                  full(gpm), full(gpf)],
        out_specs=[rows(d), rows(d)],
        out_shape=[jax.ShapeDtypeStruct((m, d), F32), jax.ShapeDtypeStruct((m, d), BF16)],
        compiler_params=_params(("parallel",), vmem),
        name="outproj",
    )(yh, ya, gh, ga, w_out, x2d, gpm, gpf)


def _ffn_up_body(hn_ref, wg_ref, wu_ref, cw_ref, cb_ref, o_ref, wgb_ref, wub_ref, *, rows):
    @pl.when(pl.program_id(1) == 0)
    def _():
        wgb_ref[...] = wg_ref[...].astype(BF16)
        wub_ref[...] = wu_ref[...].astype(BF16)

    seq = hn_ref.shape[0]
    wg, wu = wgb_ref[...], wub_ref[...]
    for r0 in range(0, seq, rows):
        lo, hi = _chunk_bounds(r0, rows, seq)
        g = _seq_conv3_chunk(_bdot(hn_ref[lo:hi, :], wg), cw_ref[...], cb_ref[...], r0, rows, seq)
        u = _bdot(hn_ref[r0:r0 + rows, :], wu)
        gelu = 0.5 * g * (1.0 + jnp.tanh(math.sqrt(2.0 / math.pi) * (g + 0.044715 * (g * g * g))))
        o_ref[r0:r0 + rows, :] = (gelu * u).astype(o_ref.dtype)


def _ffn_up(hn, w_up, conv_w, conv_b, batch, seq):
    d = hn.shape[1]
    tn = 512
    nb = D_FF // tn
    wbf = pltpu.VMEM((d, tn), BF16)
    vmem = (2 * seq * d * 2 + 2 * 2 * d * tn * 4 + 2 * d * tn * 2 + 2 * seq * tn * 2 + 8 * seq * tn * 4
            + (6 << 20))
    return pl.pallas_call(
        functools.partial(_ffn_up_body, rows=512),
        grid=(nb, batch),
        in_specs=[pl.BlockSpec((seq, d), lambda j, b: (b, 0)),
                  pl.BlockSpec((d, tn), lambda j, b: (0, j)),
                  pl.BlockSpec((d, tn), lambda j, b: (0, j + nb)),
                  pl.BlockSpec((CONV_WIDTH, tn), lambda j, b: (0, j)),
                  pl.BlockSpec((1, tn), lambda j, b: (0, j))],
        out_specs=pl.BlockSpec((seq, tn), lambda j, b: (b, j)),
        out_shape=jax.ShapeDtypeStruct((batch * seq, D_FF), BF16),
        scratch_shapes=[wbf, wbf],
        compiler_params=_params(("parallel", "arbitrary"), vmem),
        name="ffn_up",
    )(hn, w_up, w_up, conv_w, conv_b)


def _ffn_down_body(a_ref, w_ref, h_ref, g_ref, o_ref, *, rows):
    for r0 in range(0, a_ref.shape[0], rows):
        rs = slice(r0, r0 + rows)
        o_ref[rs, :] = h_ref[rs, :] + _rms(_bdot(a_ref[rs, :], w_ref[...]), g_ref[...])


def _ffn_down(act, w_down, h, gain):
    m, d = h.shape
    k = act.shape[1]
    tm = 512
    vmem = k * d * 2 + 2 * tm * k * 2 + 2 * 2 * tm * d * 4 + tm * d * 4 + (4 << 20)
    return pl.pallas_call(
        functools.partial(_ffn_down_body, rows=tm // 2),
        grid=(m // tm,),
        in_specs=[pl.BlockSpec((tm, k), lambda i: (i, 0)),
                  pl.BlockSpec((k, d), lambda i: (0, 0), pipeline_mode=pl.Buffered(1)),
                  pl.BlockSpec((tm, d), lambda i: (i, 0)),
                  pl.BlockSpec((1, d), lambda i: (0, 0))],
        out_specs=pl.BlockSpec((tm, d), lambda i: (i, 0)),
        out_shape=jax.ShapeDtypeStruct((m, d), F32),
        compiler_params=_params(("arbitrary",), vmem),
        name="ffn_down",
    )(act, w_down, h, gain)


def _position_features(l):
    t = jnp.linspace(0.0, 1.0, l, dtype=F32)[:, None]
    bands = (FILTER_EMB_DIM - 1) // 2
    w = 2.0 * math.pi * jnp.arange(l, dtype=F32) / l
    f = jnp.linspace(1e-4, bands - 1, bands, dtype=F32)
    ang = w[:, None] * f[None, :]
    return t, jnp.concatenate([t, jnp.cos(ang), -jnp.sin(ang)], axis=-1)


def _layer(h, l, seq, prm):
    (pre_mix_gain, w_in, hyena_conv_w, hyena_conv_b, filt_w1, filt_b1, filt_freq1, filt_w2, filt_b2,
     filt_freq2, filt_w3, hyena_bias, q_norm_gain, w_uq, kv_norm_gain, w_ukv, hyena_out_gain, attn_out_gain,
     w_out, post_mix_gain, pre_ffn_gain, w_up, ffn_conv_w, ffn_conv_b, w_down, post_ffn_gain) = prm
    batch = h.shape[0]
    p = seq // 2
    x2d = h.reshape(batch * seq, D_MODEL)
    row = lambda a: a[l][None, :].astype(F32)

    wt = jnp.swapaxes(w_in[l], 0, 1)
    w_tail = jnp.pad(wt[COL_CQ:].astype(BF16), ((0, PROJ_WIDTH - wt.shape[0]), (0, 0)))
    proj = _inproj(x2d, row(pre_mix_gain), wt[:COL_CQ].astype(BF16), w_tail)

    cmat, sfwd, sinv = _dft_mats(p)
    t, z = _position_features(seq)
    zp = jnp.pad(z, ((0, 0), (0, V7X_LANES - FILTER_EMB_DIM)))
    w1p = jnp.pad(filt_w1[l], ((0, V7X_LANES - FILTER_EMB_DIM), (0, 0)))
    max_decay = math.log(DECAY_TARGET) / FAST_DECAY_PCT
    min_decay = math.log(DECAY_TARGET) / SLOW_DECAY_PCT
    deltas = jnp.abs(jnp.linspace(min_decay, max_decay, HYENA_WIDTH, dtype=F32))[None, :]
    kspec = _filter_spectra(zp, t, w1p, row(filt_b1), row(filt_freq1), filt_w2[l], row(filt_b2),
                            row(filt_freq2), filt_w3[l], deltas, cmat, sfwd, p)
    conv_w = hyena_conv_w[l].reshape(CONV_WIDTH, 3, HYENA_WIDTH).transpose(1, 0, 2)
    conv_b = hyena_conv_b[l].reshape(3, 1, HYENA_WIDTH)
    y_hyena = _hyena(proj, conv_w, conv_b, kspec, row(hyena_bias), cmat, sfwd, sinv, batch, seq, p)

    pos = jnp.arange(seq, dtype=F32)
    inv_freq = 1.0 / (ROPE_THETA ** (jnp.arange(0, QK_ROPE_DIM, 2, dtype=F32) / QK_ROPE_DIM))
    ang = pos[:, None] * inv_freq[None, :]
    ang = jnp.concatenate([ang, ang], axis=-1)
    lpad = ((0, 0), (0, V7X_LANES - QK_ROPE_DIM))
    cos_t = jnp.pad(jnp.cos(ang), lpad)
    sin_t = jnp.pad(jnp.sin(ang), lpad)
    dqk = QK_NOPE_DIM + QK_ROPE_DIM
    wq = w_uq[l].reshape(Q_LORA_RANK, N_HEADS, dqk)
    wa = jnp.pad(wq, ((0, 0), (0, 0), (0, QK_PAD_DIM - dqk))).reshape(Q_LORA_RANK, N_HEADS * QK_PAD_DIM)
    wq_pe = wq[:, :, QK_NOPE_DIM:]
    wq_rot = jnp.concatenate([-wq_pe[..., QK_ROPE_DIM // 2:], wq_pe[..., :QK_ROPE_DIM // 2]], axis=-1)
    wb = jnp.pad(wq_rot, ((0, 0), (0, 0), (0, V7X_LANES - QK_ROPE_DIM))).reshape(Q_LORA_RANK, N_HEADS * V7X_LANES)
    ri = jnp.arange(V7X_LANES, dtype=jnp.int32)[:, None]
    rj = jnp.arange(V7X_LANES, dtype=jnp.int32)[None, :]
    half = QK_ROPE_DIM // 2
    rot = (jnp.where((ri == rj + half) & (rj < half), -1.0, 0.0)
           + jnp.where((ri == rj - half) & (rj >= half) & (rj < QK_ROPE_DIM), 1.0, 0.0)).astype(BF16)
    q, k, v = _qkv(proj, rot, cos_t, sin_t, row(q_norm_gain), row(kv_norm_gain), wa.astype(BF16),
                   wb.astype(BF16), w_ukv[l].astype(BF16), batch, seq)
    y_attn = _attn(q, k, v).reshape(batch * seq, ATTN_WIDTH)

    h2d, hn = _outproj(y_hyena, y_attn, row(hyena_out_gain), row(attn_out_gain), w_out[l].astype(BF16), x2d,
                       row(post_mix_gain), row(pre_ffn_gain))

    act = _ffn_up(hn, w_up[l], ffn_conv_w[l], row(ffn_conv_b), batch, seq)
    out = _ffn_down(act, w_down[l].astype(BF16), h2d, row(post_ffn_gain))
    return out.reshape(batch, seq, D_MODEL)


def kernel(x, pre_mix_gain, w_in, hyena_conv_w, hyena_conv_b, filt_w1, filt_b1, filt_freq1, filt_w2, filt_b2,
           filt_freq2, filt_w3, hyena_bias, q_norm_gain, w_uq, kv_norm_gain, w_ukv, hyena_out_gain,
           attn_out_gain, w_out, post_mix_gain, pre_ffn_gain, w_up, ffn_conv_w, ffn_conv_b, w_down,
           post_ffn_gain):
    prm = (pre_mix_gain, w_in, hyena_conv_w, hyena_conv_b, filt_w1, filt_b1, filt_freq1, filt_w2, filt_b2,
           filt_freq2, filt_w3, hyena_bias, q_norm_gain, w_uq, kv_norm_gain, w_ukv, hyena_out_gain,
           attn_out_gain, w_out, post_mix_gain, pre_ffn_gain, w_up, ffn_conv_w, ffn_conv_b, w_down,
           post_ffn_gain)
    seq = x.shape[1]
    h = x
    for l in range(w_in.shape[0]):
        h = _layer(h, l, seq, prm)
    return h
```

```python
import functools
import math

import jax
import jax.numpy as jnp
from jax import lax
from jax.experimental import pallas as pl
from jax.experimental.pallas import tpu as pltpu

F32 = jnp.float32
BF16 = jnp.bfloat16

D_MODEL = 2048
HYENA_WIDTH = 1024
CONV_WIDTH = 3
FILTER_EMB_DIM = 33
FILTER_HIDDEN = 64
DECAY_TARGET = 1e-2
FAST_DECAY_PCT = 0.3
SLOW_DECAY_PCT = 1.5
DECAY_SHIFT = 0.05
N_HEADS = 8
QK_NOPE_DIM = 128
QK_ROPE_DIM = 64
V_HEAD_DIM = 128
Q_LORA_RANK = 512
KV_LORA_RANK = 256
ROPE_THETA = 10000.0
ATTN_WIDTH = N_HEADS * V_HEAD_DIM
D_FF = 5632
NORM_EPS = 1e-6

V7X_VMEM_BYTES = 64 * 1024 * 1024
V7X_LANES = 128
V7X_MXU_DIM = 256
V7X_SUBLANES = 8
HALO_BF16 = 2 * V7X_SUBLANES

QK_PAD_DIM = V7X_MXU_DIM
PROJ_WIDTH = 4096
COL_CQ = 3 * HYENA_WIDTH
COL_CKV = COL_CQ + Q_LORA_RANK
COL_KPE = COL_CKV + KV_LORA_RANK


def _params(semantics, vmem_bytes, flags=None):
    return pltpu.CompilerParams(dimension_semantics=semantics, flags=flags,
                                vmem_limit_bytes=min(int(vmem_bytes), V7X_VMEM_BYTES - (4 << 20)))


def _rms(x, gain):
    return x * lax.rsqrt(jnp.mean(x * x, axis=-1, keepdims=True) + NORM_EPS) * gain


def _bdot(a, b):
    return jnp.dot(a, b, preferred_element_type=F32)


def _chunk_bounds(r0, rows, seq):
    return max(r0 - HALO_BF16, 0), min(r0 + rows + HALO_BF16, seq)


def _seq_conv3_chunk(xe, w, b, r0, rows, seq):
    lo, hi = _chunk_bounds(r0, rows, seq)
    n = hi - lo
    prev = pltpu.roll(xe, 1, 0)
    nxt = pltpu.roll(xe, n - 1, 0)
    if lo == 0 or hi == seq:
        row = lax.broadcasted_iota(jnp.int32, xe.shape, 0)
        if lo == 0:
            prev = jnp.where(row == 0, 0.0, prev)
        if hi == seq:
            nxt = jnp.where(row == n - 1, 0.0, nxt)
    return (prev * w[0:1] + xe * w[1:2] + nxt * w[2:3] + b)[r0 - lo:r0 - lo + rows]


def _inproj_body(x_ref, g_ref, wm_ref, wt_ref, o_ref, *, n_main, rows):
    j = pl.program_id(1)

    def project(w_ref):
        w = w_ref[...]
        for r0 in range(0, x_ref.shape[0], rows):
            xn = _rms(x_ref[r0:r0 + rows, :], g_ref[...]).astype(BF16)
            o_ref[r0:r0 + rows, :] = lax.dot_general(xn, w, (((1,), (1,)), ((), ())),
                                                     preferred_element_type=F32).astype(o_ref.dtype)

    @pl.when(j < n_main)
    def _():
        project(wm_ref)

    @pl.when(j == n_main)
    def _():
        project(wt_ref)


def _inproj(x2d, gain, w_main, w_tail):
    m, d = x2d.shape
    tm, tn = 1024, w_tail.shape[0]
    n_main = w_main.shape[0] // tn
    vmem = 2 * tm * d * 4 + tm * d * 2 + 4 * d * tn * 2 + 2 * tm * tn * 2 + tm * tn * 4 + (8 << 20)
    return pl.pallas_call(
        functools.partial(_inproj_body, n_main=n_main, rows=256),
        grid=(m // tm, n_main + 1),
        in_specs=[pl.BlockSpec((tm, d), lambda i, j: (i, 0)),
                  pl.BlockSpec((1, d), lambda i, j: (0, 0)),
                  pl.BlockSpec((tn, d), lambda i, j: (jnp.minimum(j, n_main - 1), 0)),
                  pl.BlockSpec((tn, d), lambda i, j: (0, 0))],
        out_specs=pl.BlockSpec((tm, tn), lambda i, j: (i, j)),
        out_shape=jax.ShapeDtypeStruct((m, (n_main + 1) * tn), BF16),
        compiler_params=_params(("parallel", "arbitrary"), vmem),
        name="inproj",
    )(x2d, gain, w_main, w_tail)


def _dft_mats(p):
    r = 32
    assert p == r * r
    idx = jnp.arange(p, dtype=jnp.int32)
    sub = jnp.arange(r, dtype=jnp.int32)
    ang_hi = ((idx[:, None] * (r * sub)[None, :]) % (2 * p)).astype(F32) * (math.pi / p)
    ang_lo = ((idx[:, None] * sub[None, :]) % (2 * p)).astype(F32) * (math.pi / p)
    ch, sh = jnp.cos(ang_hi)[:, :, None], jnp.sin(ang_hi)[:, :, None]
    cl, sl = jnp.cos(ang_lo)[:, None, :], jnp.sin(ang_lo)[:, None, :]
    c = (ch * cl - sh * sl).reshape(p, p)
    s = -(sh * cl + ch * sl).reshape(p, p)
    alt = jnp.where(idx % 2 == 0, 1.0, -1.0).astype(F32)
    s_fwd = jnp.where(idx[:, None] == 0, alt[None, :], s)
    s_inv = jnp.where(idx[None, :] == 0, alt[:, None], s)
    return c.astype(BF16), s_fwd.astype(BF16), s_inv.astype(BF16)


def _filter_body(z_ref, t_ref, w1_ref, b1_ref, f1_ref, w2_ref, b2_ref, f2_ref, w3f_ref, w3b_ref,
                 dl_ref, c_ref, s_ref, o_ref, hid_ref, *, p):
    hp = lax.Precision.HIGHEST

    @pl.when(pl.program_id(0) == 0)
    def _():
        h1 = jnp.sin(f1_ref[...] * (jnp.dot(z_ref[...], w1_ref[...], precision=hp,
                                            preferred_element_type=F32) + b1_ref[...]))
        hid_ref[...] = jnp.sin(f2_ref[...] * (jnp.dot(h1, w2_ref[...], precision=hp,
                                                      preferred_element_type=F32) + b2_ref[...]))

    h = hid_ref[...]
    win =jnp.exp(-t_ref[...] * dl_ref[...]) + DECAY_SHIFT
    hf = jnp.dot(h, w3f_ref[...], precision=hp, preferred_element_type=F32) * win
    hb = jnp.dot(h, w3b_ref[...], precision=hp, preferred_element_type=F32) * win
    lrow = lax.broadcasted_iota(jnp.int32, hb.shape, 0)
    hb = jnp.where(lrow == 0, 0.0, hb)

    cmat = c_ref[...]
    smat = s_ref[...]

    def fwd(x):
        hi = x.astype(BF16)
        lo = (x - hi.astype(F32)).astype(BF16)
        return (_bdot(cmat, hi) + _bdot(cmat, lo), _bdot(smat, hi) + _bdot(smat, lo))

    row = lax.broadcasted_iota(jnp.int32, (p, hf.shape[1]), 0)
    row0 = row == 0
    sigma = jnp.where(row % 2 == 1, -1.0, 1.0)

    def conj(a):
        return a[0], jnp.where(row0, a[1], -a[1])

    af0, af1 = fwd(hf[:p]), fwd(hf[p:])
    ab0, ab1 = fwd(hb[:p]), fwd(hb[p:])
    cb0 = conj(ab0)
    k0 = (af0[0] + cb0[0], af0[1] + cb0[1])
    k1 = (af1[0] + sigma * af0[0], af1[1] + sigma * af0[1])
    km1 = conj((ab1[0] + sigma * ab0[0], ab1[1] + sigma * ab0[1]))
    scale = jnp.where(row0, 0.5 / p, 1.0 / p)
    for i, a in enumerate((k0, k1, km1)):
        o_ref[2 * i] = a[0] * scale
        o_ref[2 * i + 1] = a[1] * scale


def _filter_spectra(z, t, w1, b1, f1, w2, b2, f2, w3, deltas, cmat, smat, p):
    l = z.shape[0]
    c = HYENA_WIDTH
    tc = 256
    nb = c // tc
    full = lambda a: pl.BlockSpec(a.shape, lambda j: (0,) * a.ndim)
    return pl.pallas_call(
        functools.partial(_filter_body, p=p),
        grid=(nb,),
        in_specs=[full(z), full(t), full(w1), full(b1), full(f1), full(w2), full(b2), full(f2),
                  pl.BlockSpec((FILTER_HIDDEN, tc), lambda j: (0, j)),
                  pl.BlockSpec((FILTER_HIDDEN, tc), lambda j: (0, j + nb)),
                  pl.BlockSpec((1, tc), lambda j: (0, j)),
                  full(cmat), full(smat)],
        out_specs=pl.BlockSpec((6, p, tc), lambda j: (0, 0, j)),
        out_shape=jax.ShapeDtypeStruct((6, p, c), F32),
        scratch_shapes=[pltpu.VMEM((l, FILTER_HIDDEN), F32)],
        compiler_params=_params(("arbitrary",), 40 << 20),
        name="filt",
    )(z, t, w1, b1, f1, w2, b2, f2, w3, w3, deltas, cmat, smat)


def _hyena_body(x0_ref, x1_ref, v_ref, cw_ref, cb_ref, ks_ref, hb_ref, c_ref, sf_ref, si_ref, o_ref, *, p):
    cmat = c_ref[...]
    sfwd = sf_ref[...]
    sinv = si_ref[...]
    seq = 2 * p
    row0 = lax.broadcasted_iota(jnp.int32, (p, o_ref.shape[1]), 0) == 0

    def cmul(i, uu):
        kre, kim = ks_ref[2 * i], ks_ref[2 * i + 1]
        ii = kim * uu[1]
        return (kre * uu[0] - jnp.where(row0, 0.0, ii),
                jnp.where(row0, ii, kre * uu[1] + kim * uu[0]))

    def inv(a, b):
        return _bdot(cmat, (a[0] + b[0]).astype(BF16)) + _bdot(sinv, (a[1] + b[1]).astype(BF16))

    def conv(x_ref, i, r0):
        lo, hi = _chunk_bounds(r0, p, seq)
        return _seq_conv3_chunk(x_ref[lo:hi, :].astype(F32), cw_ref[i], cb_ref[i], r0, p, seq)

    us, spec = [], []
    for j in range(2):
        u = conv(x1_ref, 1, j * p) * conv(v_ref, 2, j * p)
        ub = u.astype(BF16)
        us.append(u)
        spec.append((_bdot(cmat, ub), _bdot(sfwd, ub)))
    for j, (ia, ib) in enumerate(((0, 2), (1, 0))):
        y = inv(cmul(ia, spec[0]), cmul(ib, spec[1]))
        x0 = conv(x0_ref, 0, j * p)
        o_ref[j * p:(j + 1) * p, :] = (x0 * (y + us[j] * hb_ref[...])).astype(o_ref.dtype)


def _hyena(proj, conv_w, conv_b, kspec, hbias, cmat, sfwd, sinv, batch, seq, p):
    c = HYENA_WIDTH
    tc = 256
    nb = c // tc
    nseq = 1
    xspec = lambda off: pl.BlockSpec((nseq * seq, tc), lambda j, b: (b, j + off * nb))
    full = lambda a: pl.BlockSpec(a.shape, lambda j, b: (0,) * a.ndim)
    vmem = (nseq * 6 * seq * tc * 2 + 2 * 6 * p * tc * 4 + 6 * p * p * 2 + nseq * 2 * seq * tc * 2
            + nseq * 6 * seq * tc * 4 + (6 << 20))
    return pl.pallas_call(
        functools.partial(_hyena_body, p=p),
        grid=(nb, batch // nseq),
        in_specs=[xspec(0), xspec(1), xspec(2),
                  pl.BlockSpec((3, CONV_WIDTH, tc), lambda j, b: (0, 0, j)),
                  pl.BlockSpec((3, 1, tc), lambda j, b: (0, 0, j)),
                  pl.BlockSpec((6, p, tc), lambda j, b: (0, 0, j)),
                  pl.BlockSpec((1, tc), lambda j, b: (0, j)),
                  full(cmat), full(sfwd), full(sinv)],
        out_specs=pl.BlockSpec((nseq * seq, tc), lambda j, b: (b, j)),
        out_shape=jax.ShapeDtypeStruct((batch * seq, c), BF16),
        compiler_params=_params(("parallel", "parallel"), vmem),
        name="hyena",
    )(proj, proj, proj, conv_w, conv_b, kspec, hbias, cmat, sfwd, sinv)


def _rotate_half_lanes(x):
    half = QK_ROPE_DIM // 2
    lane = lax.broadcasted_iota(jnp.int32, x.shape, 1)
    return jnp.where(lane < half, -pltpu.roll(x, V7X_LANES - half, 1), pltpu.roll(x, half, 1))


def _qkv_body(cq_ref, ckv_ref, kpe_ref, cos_ref, sin_ref, gq_ref, gkv_ref, wa_ref, wkv_ref,
              q_ref, k_ref, v_ref, *, rows):
    scale = (QK_NOPE_DIM + QK_ROPE_DIM) ** -0.5 * math.log2(math.e)
    n = cq_ref.shape[0] // rows

    def project(c):
        rs = slice(c * rows, (c + 1) * rows)
        cqn = _rms(cq_ref[rs, :].astype(F32), gq_ref[...]).astype(BF16)
        ckvn = _rms(ckv_ref[rs, :].astype(F32), gkv_ref[...]).astype(BF16)
        return (_bdot(cqn, wa_ref[...]),
                _bdot(ckvn, wkv_ref[...]))

    nxt = project(0)
    for c in range(n):
        qa, kv = nxt
        if c + 1 < n:
            nxt = project(c + 1)
        rs = slice(c * rows, (c + 1) * rows)
        cos = cos_ref[rs, :]
        sin = sin_ref[rs, :]
        kpe = kpe_ref[rs, :].astype(F32)
        kpe = (kpe * cos + _rotate_half_lanes(kpe) * sin).astype(BF16)
        for h in range(N_HEADS):
            a = h * QK_PAD_DIM
            qpe = qa[:, a + 128:a + 256]
            q_ref[0, h, rs, 0:128] = (qa[:, a:a + 128] * scale).astype(BF16)
            q_ref[0, h, rs, 128:256] = ((qpe * cos + _rotate_half_lanes(qpe) * sin) * scale).astype(BF16)
            k_ref[0, h, rs, 0:128] = kv[:, a:a + 128].astype(BF16)
            k_ref[0, h, rs, 128:256] = kpe
            v_ref[0, h, rs, :] = kv[:, a + 128:a + 256].astype(BF16)


def _qkv(proj, cos_t, sin_t, gq, gkv, wa, wkv, batch, seq):
    tm = 512
    ns = seq // tm
    full = lambda a: pl.BlockSpec(a.shape, lambda b, i: (0,) * a.ndim)
    col = lambda width, off: pl.BlockSpec((tm, width), lambda b, i: (b * ns + i, off // width))
    hd = lambda w: pl.BlockSpec((1, N_HEADS, tm, w), lambda b, i: (b, 0, i, 0))
    return pl.pallas_call(
        functools.partial(_qkv_body, rows=tm // 2),
        grid=(batch, ns),
        in_specs=[col(Q_LORA_RANK, COL_CQ), col(KV_LORA_RANK, COL_CKV), col(V7X_LANES, COL_KPE),
                  pl.BlockSpec((tm, V7X_LANES), lambda b, i: (i, 0)),
                  pl.BlockSpec((tm, V7X_LANES), lambda b, i: (i, 0)),
                  full(gq), full(gkv), full(wa), full(wkv)],
        out_specs=[hd(QK_PAD_DIM), hd(QK_PAD_DIM), hd(V_HEAD_DIM)],
        out_shape=[jax.ShapeDtypeStruct((batch, N_HEADS, seq, QK_PAD_DIM), BF16),
                   jax.ShapeDtypeStruct((batch, N_HEADS, seq, QK_PAD_DIM), BF16),
                   jax.ShapeDtypeStruct((batch, N_HEADS, seq, V_HEAD_DIM), BF16)],
        compiler_params=_params(("parallel", "parallel"), 48 << 20),
        name="qkv",
    )(proj, proj, proj, cos_t, sin_t, gq, gkv, wa, wkv)


def _attn_body(q_ref, k_ref, v_ref, o_ref, *, tq):
    heads, seq = q_ref.shape[1], q_ref.shape[2]
    work = [(h, c) for h in range(heads) for c in range(seq // tq)]

    def scores(h, c):
        return lax.dot_general(q_ref[0, h, c * tq:(c + 1) * tq, :], k_ref[0, h], (((1,), (1,)), ((), ())),
                               preferred_element_type=F32)

    s_next = scores(*work[0])
    for i, (h, c) in enumerate(work):
        s = s_next
        if i + 1 < len(work):
            s_next = scores(*work[i + 1])
        m = jnp.max(s, axis=-1, keepdims=True)
        e = jnp.exp2(s - m)
        l = jnp.sum(e, axis=-1, keepdims=True)
        o = _bdot(e.astype(BF16), v_ref[0, h])
        o_ref[0, c * tq:(c + 1) * tq, h * V_HEAD_DIM:(h + 1) * V_HEAD_DIM] = (o / l).astype(o_ref.dtype)


def _attn(q, k, v):
    batch, heads, seq, _ = q.shape
    tq = 256
    hps = 2
    hspec = lambda w: pl.BlockSpec((1, hps, seq, w), lambda b, h: (b, h, 0, 0))
    return pl.pallas_call(
        functools.partial(_attn_body, tq=tq),
        grid=(batch, heads // hps),
        in_specs=[hspec(QK_PAD_DIM), hspec(QK_PAD_DIM), hspec(V_HEAD_DIM)],
        out_specs=pl.BlockSpec((1, seq, hps * V_HEAD_DIM), lambda b, h: (b, 0, h)),
        out_shape=jax.ShapeDtypeStruct((batch, seq, heads * V_HEAD_DIM), BF16),
        compiler_params=_params(("parallel", "parallel"), 48 << 20),
        name="attn",
    )(q, k, v)


def _outproj_body(yh_ref, ya_ref, gh_ref, ga_ref, w_ref, x_ref, gpm_ref, gpf_ref, wd_ref, h_ref, hn_ref, wdb_ref,
                  *, rows):
    wdb_ref[...] = wd_ref[...].astype(BF16)
    n = x_ref.shape[0] // rows

    def mix(c):
        rs = slice(c * rows, (c + 1) * rows)
        a = _rms(yh_ref[rs, :].astype(F32), gh_ref[...]).astype(BF16)
        b = _rms(ya_ref[rs, :].astype(F32), ga_ref[...]).astype(BF16)
        return _bdot(a, w_ref[0:HYENA_WIDTH, :]) + _bdot(b, w_ref[HYENA_WIDTH:, :])

    nxt = mix(0)
    for c in range(n):
        mixed = nxt
        if c + 1 < n:
            nxt = mix(c + 1)
        rs = slice(c * rows, (c + 1) * rows)
        h = x_ref[rs, :] + _rms(mixed, gpm_ref[...])
        h_ref[rs, :] = h
        hn_ref[rs, :] = _rms(h, gpf_ref[...]).astype(BF16)


def _outproj(yh, ya, gh, ga, w_out, x2d, gpm, gpf, w_down):
    m, d = x2d.shape
    tm = 512
    full = lambda a: pl.BlockSpec(a.shape, lambda i: (0,) * a.ndim)
    rows = lambda w: pl.BlockSpec((tm, w), lambda i: (i, 0))
    wd_rows = pl.BlockSpec((w_down.shape[0] // (m // tm), d), lambda i: (i, 0))
    vmem = 2 * 2 * tm * HYENA_WIDTH * 2 + 2 * d * d * 2 + 2 * tm * d * (4 + 4 + 2) + 4 * tm * d * 4 + (6 << 20)
    return pl.pallas_call(
        functools.partial(_outproj_body, rows=tm // 2),
        grid=(m // tm,),
        in_specs=[rows(HYENA_WIDTH), rows(ATTN_WIDTH), full(gh), full(ga), full(w_out), rows(d),
                  full(gpm), full(gpf), wd_rows],
        out_specs=[rows(d), rows(d), wd_rows],
        out_shape=[jax.ShapeDtypeStruct((m, d), F32), jax.ShapeDtypeStruct((m, d), BF16),
                   jax.ShapeDtypeStruct(w_down.shape, BF16)],
        compiler_params=_params(("parallel",), vmem),
        name="outproj",
    )(yh, ya, gh, ga, w_out, x2d, gpm, gpf, w_down)


def _ffn_up_body(hn_ref, wg_ref, wu_ref, cw_ref, cb_ref, o_ref, wgb_ref, wub_ref, *, rows):
    @pl.when(pl.program_id(1) == 0)
    def _():
        wgb_ref[...] = wg_ref[...].astype(BF16)
        wub_ref[...] = wu_ref[...].astype(BF16)

    seq = hn_ref.shape[0]
    wg, wu = wgb_ref[...], wub_ref[...]
    for r0 in range(0, seq, rows):
        lo, hi = _chunk_bounds(r0, rows, seq)
        g = _seq_conv3_chunk(_bdot(hn_ref[lo:hi, :], wg), cw_ref[...], cb_ref[...], r0, rows, seq)
        u = _bdot(hn_ref[r0:r0 + rows, :], wu)
        gelu = 0.5 * g * (1.0 + jnp.tanh(math.sqrt(2.0 / math.pi) * (g + 0.044715 * (g * g * g))))
        o_ref[r0:r0 + rows, :] = (gelu * u).astype(o_ref.dtype)


def _ffn_up(hn, w_up, conv_w, conv_b, batch, seq):
    d = hn.shape[1]
    tn = 512
    nb = D_FF // tn
    wbf = pltpu.VMEM((d, tn), BF16)
    vmem = (2 * seq * d * 2 + 2 * 2 * d * tn * 4 + 2 * d * tn * 2 + 2 * seq * tn * 2 + 8 * seq * tn * 4
            + (6 << 20))
    return pl.pallas_call(
        functools.partial(_ffn_up_body, rows=512),
        grid=(nb, batch),
        in_specs=[pl.BlockSpec((seq, d), lambda j, b: (b, 0)),
                  pl.BlockSpec((d, tn), lambda j, b: (0, j)),
                  pl.BlockSpec((d, tn), lambda j, b: (0, j + nb)),
                  pl.BlockSpec((CONV_WIDTH, tn), lambda j, b: (0, j)),
                  pl.BlockSpec((1, tn), lambda j, b: (0, j))],
        out_specs=pl.BlockSpec((seq, tn), lambda j, b: (b, j)),
        out_shape=jax.ShapeDtypeStruct((batch * seq, D_FF), BF16),
        scratch_shapes=[wbf, wbf],
        compiler_params=_params(("parallel", "arbitrary"), vmem),
        name="ffn_up",
    )(hn, w_up, w_up, conv_w, conv_b)


def _ffn_down_body(a_ref, w_ref, h_ref, g_ref, o_ref, *, rows):
    for r0 in range(0, a_ref.shape[0], rows):
        rs = slice(r0, r0 + rows)
        o_ref[rs, :] = h_ref[rs, :] + _rms(_bdot(a_ref[rs, :], w_ref[...]), g_ref[...])


def _ffn_down(act, w_down, h, gain):
    m, d = h.shape
    k = act.shape[1]
    tm = 512
    vmem = k * d * 2 + 2 * tm * k * 2 + 2 * 2 * tm * d * 4 + tm * d * 4 + (4 << 20)
    return pl.pallas_call(
        functools.partial(_ffn_down_body, rows=tm // 2),
        grid=(m // tm,),
        in_specs=[pl.BlockSpec((tm, k), lambda i: (i, 0)),
                  pl.BlockSpec((k, d), lambda i: (0, 0), pipeline_mode=pl.Buffered(1)),
                  pl.BlockSpec((tm, d), lambda i: (i, 0)),
                  pl.BlockSpec((1, d), lambda i: (0, 0))],
        out_specs=pl.BlockSpec((tm, d), lambda i: (i, 0)),
        out_shape=jax.ShapeDtypeStruct((m, d), F32),
        compiler_params=_params(("arbitrary",), vmem),
        name="ffn_down",
    )(act, w_down, h, gain)


def _position_features(l):
    t = jnp.linspace(0.0, 1.0, l, dtype=F32)[:, None]
    bands = (FILTER_EMB_DIM - 1) // 2
    w = 2.0 * math.pi * jnp.arange(l, dtype=F32) / l
    f = jnp.linspace(1e-4, bands - 1, bands, dtype=F32)
    ang = w[:, None] * f[None, :]
    return t, jnp.concatenate([t, jnp.cos(ang), -jnp.sin(ang)], axis=-1)


def _layer(h, l, seq, prm):
    (pre_mix_gain, w_in, hyena_conv_w, hyena_conv_b, filt_w1, filt_b1, filt_freq1, filt_w2, filt_b2,
     filt_freq2, filt_w3, hyena_bias, q_norm_gain, w_uq, kv_norm_gain, w_ukv, hyena_out_gain, attn_out_gain,
     w_out, post_mix_gain, pre_ffn_gain, w_up, ffn_conv_w, ffn_conv_b, w_down, post_ffn_gain) = prm
    batch = h.shape[0]
    p = seq // 2
    x2d = h.reshape(batch * seq, D_MODEL)
    row = lambda a: a[l][None, :].astype(F32)

    wt = jnp.swapaxes(w_in[l], 0, 1)
    w_tail = jnp.pad(wt[COL_CQ:].astype(BF16), ((0, PROJ_WIDTH - wt.shape[0]), (0, 0)))
    proj = _inproj(x2d, row(pre_mix_gain), wt[:COL_CQ].astype(BF16), w_tail)

    cmat, sfwd, sinv = _dft_mats(p)
    t, z = _position_features(seq)
    zp = jnp.pad(z, ((0, 0), (0, V7X_LANES - FILTER_EMB_DIM)))
    w1p = jnp.pad(filt_w1[l], ((0, V7X_LANES - FILTER_EMB_DIM), (0, 0)))
    max_decay = math.log(DECAY_TARGET) / FAST_DECAY_PCT
    min_decay = math.log(DECAY_TARGET) / SLOW_DECAY_PCT
    deltas = jnp.abs(jnp.linspace(min_decay, max_decay, HYENA_WIDTH, dtype=F32))[None, :]
    kspec = _filter_spectra(zp, t, w1p, row(filt_b1), row(filt_freq1), filt_w2[l], row(filt_b2),
                            row(filt_freq2), filt_w3[l], deltas, cmat, sfwd, p)
    conv_w = hyena_conv_w[l].reshape(CONV_WIDTH, 3, HYENA_WIDTH).transpose(1, 0, 2)
    conv_b = hyena_conv_b[l].reshape(3, 1, HYENA_WIDTH)
    y_hyena = _hyena(proj, conv_w, conv_b, kspec, row(hyena_bias), cmat, sfwd, sinv, batch, seq, p)

    pos = jnp.arange(seq, dtype=F32)
    inv_freq = 1.0 / (ROPE_THETA ** (jnp.arange(0, QK_ROPE_DIM, 2, dtype=F32) / QK_ROPE_DIM))
    ang = pos[:, None] * inv_freq[None, :]
    ang = jnp.concatenate([ang, ang], axis=-1)
    lpad = ((0, 0), (0, V7X_LANES - QK_ROPE_DIM))
    cos_t = jnp.pad(jnp.cos(ang), lpad)
    sin_t = jnp.pad(jnp.sin(ang), lpad)
    dqk = QK_NOPE_DIM + QK_ROPE_DIM
    wq = w_uq[l].reshape(Q_LORA_RANK, N_HEADS, dqk)
    wa = jnp.pad(wq, ((0, 0), (0, 0), (0, QK_PAD_DIM - dqk))).reshape(Q_LORA_RANK, N_HEADS * QK_PAD_DIM)
    q, k, v = _qkv(proj, cos_t, sin_t, row(q_norm_gain), row(kv_norm_gain), wa.astype(BF16),
                   w_ukv[l].astype(BF16), batch, seq)
    y_attn = _attn(q, k, v).reshape(batch * seq, ATTN_WIDTH)

    h2d, hn, w_down_bf = _outproj(y_hyena, y_attn, row(hyena_out_gain), row(attn_out_gain), w_out[l].astype(BF16),
                                  x2d, row(post_mix_gain), row(pre_ffn_gain), w_down[l])

    act = _ffn_up(hn, w_up[l], ffn_conv_w[l], row(ffn_conv_b), batch, seq)
    out = _ffn_down(act, w_down_bf, h2d, row(post_ffn_gain))
    return out.reshape(batch, seq, D_MODEL)


def kernel(x, pre_mix_gain, w_in, hyena_conv_w, hyena_conv_b, filt_w1, filt_b1, filt_freq1, filt_w2, filt_b2,
           filt_freq2, filt_w3, hyena_bias, q_norm_gain, w_uq, kv_norm_gain, w_ukv, hyena_out_gain,
           attn_out_gain, w_out, post_mix_gain, pre_ffn_gain, w_up, ffn_conv_w, ffn_conv_b, w_down,
           post_ffn_gain):
    prm = (pre_mix_gain, w_in, hyena_conv_w, hyena_conv_b, filt_w1, filt_b1, filt_freq1, filt_w2, filt_b2,
           filt_freq2, filt_w3, hyena_bias, q_norm_gain, w_uq, kv_norm_gain, w_ukv, hyena_out_gain,
           attn_out_gain, w_out, post_mix_gain, pre_ffn_gain, w_up, ffn_conv_w, ffn_conv_b, w_down,
           post_ffn_gain)
    seq = x.shape[1]
    h = x
    for l in range(w_in.shape[0]):
        h = _layer(h, l, seq, prm)
    return h
```

```python
import functools
import math

import jax
import jax.numpy as jnp
from jax import lax
from jax.experimental import pallas as pl
from jax.experimental.pallas import tpu as pltpu

F32 = jnp.float32
BF16 = jnp.bfloat16

D_MODEL = 2048
HYENA_WIDTH = 1024
CONV_WIDTH = 3
FILTER_EMB_DIM = 33
FILTER_HIDDEN = 64
DECAY_TARGET = 1e-2
FAST_DECAY_PCT = 0.3
SLOW_DECAY_PCT = 1.5
DECAY_SHIFT = 0.05
N_HEADS = 8
QK_NOPE_DIM = 128
QK_ROPE_DIM = 64
V_HEAD_DIM = 128
Q_LORA_RANK = 512
KV_LORA_RANK = 256
ROPE_THETA = 10000.0
ATTN_WIDTH = N_HEADS * V_HEAD_DIM
D_FF = 5632
NORM_EPS = 1e-6

V7X_VMEM_BYTES = 64 * 1024 * 1024
V7X_LANES = 128
V7X_MXU_DIM = 256
V7X_SUBLANES = 8
HALO_BF16 = 2 * V7X_SUBLANES
VMEM_RESERVE_BYTES = 4 * 1024 * 1024
VMEM_TEMP_BYTES = 6 * 1024 * 1024

QK_PAD_DIM = V7X_MXU_DIM
PROJ_WIDTH = 4096
COL_CQ = 3 * HYENA_WIDTH
COL_CKV = COL_CQ + Q_LORA_RANK
COL_KPE = COL_CKV + KV_LORA_RANK


def _params(semantics, block_bytes):
    limit = min(int(block_bytes) + VMEM_TEMP_BYTES, V7X_VMEM_BYTES - VMEM_RESERVE_BYTES)
    return pltpu.CompilerParams(dimension_semantics=semantics, vmem_limit_bytes=limit)


def _rms(x, gain):
    return x * lax.rsqrt(jnp.mean(x * x, axis=-1, keepdims=True) + NORM_EPS) * gain


def _bdot(a, b):
    return jnp.dot(a, b, preferred_element_type=F32)


def _chunk_bounds(r0, rows, seq):
    return max(r0 - HALO_BF16, 0), min(r0 + rows + HALO_BF16, seq)


def _seq_conv3_chunk(xe, w, b, r0, rows, seq):
    lo, hi = _chunk_bounds(r0, rows, seq)
    n = hi - lo
    prev = pltpu.roll(xe, 1, 0)
    nxt = pltpu.roll(xe, n - 1, 0)
    if lo == 0 or hi == seq:
        row = lax.broadcasted_iota(jnp.int32, xe.shape, 0)
        if lo == 0:
            prev = jnp.where(row == 0, 0.0, prev)
        if hi == seq:
            nxt = jnp.where(row == n - 1, 0.0, nxt)
    return (prev * w[0:1] + xe * w[1:2] + nxt * w[2:3] + b)[r0 - lo:r0 - lo + rows]


def _inproj_body(x_ref, g_ref, wm_ref, wt_ref, o_ref, *, n_main, rows):
    j = pl.program_id(1)

    def project(w_ref):
        w = w_ref[...]
        for r0 in range(0, x_ref.shape[0], rows):
            xn = _rms(x_ref[r0:r0 + rows, :], g_ref[...]).astype(BF16)
            o_ref[r0:r0 + rows, :] = lax.dot_general(xn, w, (((1,), (1,)), ((), ())),
                                                     preferred_element_type=F32).astype(o_ref.dtype)

    @pl.when(j < n_main)
    def _():
        project(wm_ref)

    @pl.when(j == n_main)
    def _():
        project(wt_ref)


def _inproj(x2d, gain, w_main, w_tail):
    m, d = x2d.shape
    tm, tn = 1024, w_tail.shape[0]
    n_main = w_main.shape[0] // tn
    vmem = 2 * tm * d * 4 + tm * d * 2 + 4 * d * tn * 2 + 2 * tm * tn * 2 + tm * tn * 4
    return pl.pallas_call(
        functools.partial(_inproj_body, n_main=n_main, rows=256),
        grid=(m // tm, n_main + 1),
        in_specs=[pl.BlockSpec((tm, d), lambda i, j: (i, 0)),
                  pl.BlockSpec((1, d), lambda i, j: (0, 0)),
                  pl.BlockSpec((tn, d), lambda i, j: (jnp.minimum(j, n_main - 1), 0)),
                  pl.BlockSpec((tn, d), lambda i, j: (0, 0))],
        out_specs=pl.BlockSpec((tm, tn), lambda i, j: (i, j)),
        out_shape=jax.ShapeDtypeStruct((m, (n_main + 1) * tn), BF16),
        compiler_params=_params(("parallel", "arbitrary"), vmem),
        name="inproj",
    )(x2d, gain, w_main, w_tail)


def _dft_mats(p):
    r = 32
    assert p == r * r
    idx = jnp.arange(p, dtype=jnp.int32)
    sub = jnp.arange(r, dtype=jnp.int32)
    ang_hi = ((idx[:, None] * (r * sub)[None, :]) % (2 * p)).astype(F32) * (math.pi / p)
    ang_lo = ((idx[:, None] * sub[None, :]) % (2 * p)).astype(F32) * (math.pi / p)
    ch, sh = jnp.cos(ang_hi)[:, :, None], jnp.sin(ang_hi)[:, :, None]
    cl, sl = jnp.cos(ang_lo)[:, None, :], jnp.sin(ang_lo)[:, None, :]
    c = (ch * cl - sh * sl).reshape(p, p)
    s = -(sh * cl + ch * sl).reshape(p, p)
    alt = jnp.where(idx % 2 == 0, 1.0, -1.0).astype(F32)
    s_fwd = jnp.where(idx[:, None] == 0, alt[None, :], s)
    s_inv = jnp.where(idx[None, :] == 0, alt[:, None], s)
    return c.astype(BF16), s_fwd.astype(BF16), s_inv.astype(BF16)


def _filter_body(z_ref, t_ref, w1_ref, b1_ref, f1_ref, w2_ref, b2_ref, f2_ref, w3f_ref, w3b_ref,
                 dl_ref, c_ref, s_ref, o_ref, hid_ref, *, p):
    hp = lax.Precision.HIGHEST

    @pl.when(pl.program_id(0) == 0)
    def _():
        h1 = jnp.sin(f1_ref[...] * (jnp.dot(z_ref[...], w1_ref[...], precision=hp,
                                            preferred_element_type=F32) + b1_ref[...]))
        hid_ref[...] = jnp.sin(f2_ref[...] * (jnp.dot(h1, w2_ref[...], precision=hp,
                                                      preferred_element_type=F32) + b2_ref[...]))

    h = hid_ref[...]
    win = jnp.exp(-t_ref[...] * dl_ref[...]) + DECAY_SHIFT
    hf = jnp.dot(h, w3f_ref[...], precision=hp, preferred_element_type=F32) * win
    hb = jnp.dot(h, w3b_ref[...], precision=hp, preferred_element_type=F32) * win
    lrow = lax.broadcasted_iota(jnp.int32, hb.shape, 0)
    hb = jnp.where(lrow == 0, 0.0, hb)

    cmat = c_ref[...]
    smat = s_ref[...]

    def fwd(x):
        hi = x.astype(BF16)
        lo = (x - hi.astype(F32)).astype(BF16)
        return (_bdot(cmat, hi) + _bdot(cmat, lo), _bdot(smat, hi) + _bdot(smat, lo))

    row = lax.broadcasted_iota(jnp.int32, (p, hf.shape[1]), 0)
    row0 = row == 0
    sigma = jnp.where(row % 2 == 1, -1.0, 1.0)

    def conj(a):
        return a[0], jnp.where(row0, a[1], -a[1])

    af0, af1 = fwd(hf[:p]), fwd(hf[p:])
    ab0, ab1 = fwd(hb[:p]), fwd(hb[p:])
    cb0 = conj(ab0)
    k0 = (af0[0] + cb0[0], af0[1] + cb0[1])
    k1 = (af1[0] + sigma * af0[0], af1[1] + sigma * af0[1])
    km1 = conj((ab1[0] + sigma * ab0[0], ab1[1] + sigma * ab0[1]))
    scale = jnp.where(row0, 0.5 / p, 1.0 / p)
    for i, a in enumerate((k0, k1, km1)):
        o_ref[2 * i] = a[0] * scale
        o_ref[2 * i + 1] = a[1] * scale


def _filter_spectra(z, t, w1, b1, f1, w2, b2, f2, w3, deltas, cmat, smat, p):
    l = z.shape[0]
    c = HYENA_WIDTH
    tc = 256
    nb = c // tc
    full = lambda a: pl.BlockSpec(a.shape, lambda j: (0,) * a.ndim)
    return pl.pallas_call(
        functools.partial(_filter_body, p=p),
        grid=(nb,),
        in_specs=[full(z), full(t), full(w1), full(b1), full(f1), full(w2), full(b2), full(f2),
                  pl.BlockSpec((FILTER_HIDDEN, tc), lambda j: (0, j)),
                  pl.BlockSpec((FILTER_HIDDEN, tc), lambda j: (0, j + nb)),
                  pl.BlockSpec((1, tc), lambda j: (0, j)),
                  full(cmat), full(smat)],
        out_specs=pl.BlockSpec((6, p, tc), lambda j: (0, 0, j)),
        out_shape=jax.ShapeDtypeStruct((6, p, c), F32),
        scratch_shapes=[pltpu.VMEM((l, FILTER_HIDDEN), F32)],
        compiler_params=_params(("arbitrary",), 2 * 6 * p * tc * 4 + 2 * 2 * p * p * 2 + 10 * l * tc * 4),
        name="filt",
    )(z, t, w1, b1, f1, w2, b2, f2, w3, w3, deltas, cmat, smat)


def _hyena_body(x0_ref, x1_ref, v_ref, cw_ref, cb_ref, ks_ref, hb_ref, c_ref, sf_ref, si_ref, o_ref, *, p):
    cmat = c_ref[...]
    sfwd = sf_ref[...]
    sinv = si_ref[...]
    seq = 2 * p
    row0 = lax.broadcasted_iota(jnp.int32, (p, o_ref.shape[1]), 0) == 0

    def cmul(i, uu):
        kre, kim = ks_ref[2 * i], ks_ref[2 * i + 1]
        ii = kim * uu[1]
        return (kre * uu[0] - jnp.where(row0, 0.0, ii),
                jnp.where(row0, ii, kre * uu[1] + kim * uu[0]))

    def inv(a, b):
        return _bdot(cmat, (a[0] + b[0]).astype(BF16)) + _bdot(sinv, (a[1] + b[1]).astype(BF16))

    def conv(x_ref, i, r0):
        lo, hi = _chunk_bounds(r0, p, seq)
        return _seq_conv3_chunk(x_ref[lo:hi, :].astype(F32), cw_ref[i], cb_ref[i], r0, p, seq)

    us, spec = [], []
    for j in range(2):
        u = conv(x1_ref, 1, j * p) * conv(v_ref, 2, j * p)
        ub = u.astype(BF16)
        us.append(u)
        spec.append((_bdot(cmat, ub), _bdot(sfwd, ub)))
    for j, (ia, ib) in enumerate(((0, 2), (1, 0))):
        y = inv(cmul(ia, spec[0]), cmul(ib, spec[1]))
        x0 = conv(x0_ref, 0, j * p)
        o_ref[j * p:(j + 1) * p, :] = (x0 * (y + us[j] * hb_ref[...])).astype(o_ref.dtype)


def _hyena(proj, conv_w, conv_b, kspec, hbias, cmat, sfwd, sinv, batch, seq, p):
    c = HYENA_WIDTH
    tc = 256
    nb = c // tc
    xspec = lambda off: pl.BlockSpec((seq, tc), lambda j, b: (b, j + off * nb))
    full = lambda a: pl.BlockSpec(a.shape, lambda j, b: (0,) * a.ndim)
    vmem = (6 * seq * tc * 2 + 2 * 6 * p * tc * 4 + 6 * p * p * 2 + 2 * seq * tc * 2
            + 6 * seq * tc * 4)
    return pl.pallas_call(
        functools.partial(_hyena_body, p=p),
        grid=(nb, batch),
        in_specs=[xspec(0), xspec(1), xspec(2),
                  pl.BlockSpec((3, CONV_WIDTH, tc), lambda j, b: (0, 0, j)),
                  pl.BlockSpec((3, 1, tc), lambda j, b: (0, 0, j)),
                  pl.BlockSpec((6, p, tc), lambda j, b: (0, 0, j)),
                  pl.BlockSpec((1, tc), lambda j, b: (0, j)),
                  full(cmat), full(sfwd), full(sinv)],
        out_specs=pl.BlockSpec((seq, tc), lambda j, b: (b, j)),
        out_shape=jax.ShapeDtypeStruct((batch * seq, c), BF16),
        compiler_params=_params(("parallel", "parallel"), vmem),
        name="hyena",
    )(proj, proj, proj, conv_w, conv_b, kspec, hbias, cmat, sfwd, sinv)


def _rotate_half_lanes(x):
    half = QK_ROPE_DIM // 2
    lane = lax.broadcasted_iota(jnp.int32, x.shape, 1)
    return jnp.where(lane < half, -pltpu.roll(x, V7X_LANES - half, 1), pltpu.roll(x, half, 1))


def _qkv_body(cq_ref, ckv_ref, kpe_ref, cos_ref, sin_ref, gq_ref, gkv_ref, wa_ref, wkv_ref,
              q_ref, k_ref, v_ref, *, rows):
    scale = (QK_NOPE_DIM + QK_ROPE_DIM) ** -0.5 * math.log2(math.e)
    n = cq_ref.shape[0] // rows

    def project(c):
        rs = slice(c * rows, (c + 1) * rows)
        cqn = _rms(cq_ref[rs, :].astype(F32), gq_ref[...]).astype(BF16)
        ckvn = _rms(ckv_ref[rs, :].astype(F32), gkv_ref[...]).astype(BF16)
        return (_bdot(cqn, wa_ref[...]),
                _bdot(ckvn, wkv_ref[...]))

    nxt = project(0)
    for c in range(n):
        qa, kv = nxt
        if c + 1 < n:
            nxt = project(c + 1)
        rs = slice(c * rows, (c + 1) * rows)
        cos = cos_ref[rs, :]
        sin = sin_ref[rs, :]
        kpe = kpe_ref[rs, :].astype(F32)
        kpe = (kpe * cos + _rotate_half_lanes(kpe) * sin).astype(BF16)
        for h in range(N_HEADS):
            a = h * QK_PAD_DIM
            qpe = qa[:, a + 128:a + 256]
            q_ref[0, h, rs, 0:128] = (qa[:, a:a + 128] * scale).astype(BF16)
            q_ref[0, h, rs, 128:256] = ((qpe * cos + _rotate_half_lanes(qpe) * sin) * scale).astype(BF16)
            k_ref[0, h, rs, 0:128] = kv[:, a:a + 128].astype(BF16)
            k_ref[0, h, rs, 128:256] = kpe
            v_ref[0, h, rs, :] = kv[:, a + 128:a + 256].astype(BF16)


def _qkv(proj, cos_t, sin_t, gq, gkv, wa, wkv, batch, seq):
    tm = 512
    ns = seq // tm
    full = lambda a: pl.BlockSpec(a.shape, lambda b, i: (0,) * a.ndim)
    col = lambda width, off: pl.BlockSpec((tm, width), lambda b, i: (b * ns + i, off // width))
    hd = lambda w: pl.BlockSpec((1, N_HEADS, tm, w), lambda b, i: (b, 0, i, 0))
    vmem = (2 * tm * (Q_LORA_RANK + KV_LORA_RANK + V7X_LANES) * 2 + 4 * tm * V7X_LANES * 4 + 2 * (wa.size + wkv.size) * 2
            + 2 * N_HEADS * tm * (2 * QK_PAD_DIM + V_HEAD_DIM) * 2 + 4 * tm * N_HEADS * QK_PAD_DIM * 4)
    return pl.pallas_call(
        functools.partial(_qkv_body, rows=tm // 2),
        grid=(batch, ns),
        in_specs=[col(Q_LORA_RANK, COL_CQ), col(KV_LORA_RANK, COL_CKV), col(V7X_LANES, COL_KPE),
                  pl.BlockSpec((tm, V7X_LANES), lambda b, i: (i, 0)),
                  pl.BlockSpec((tm, V7X_LANES), lambda b, i: (i, 0)),
                  full(gq), full(gkv), full(wa), full(wkv)],
        out_specs=[hd(QK_PAD_DIM), hd(QK_PAD_DIM), hd(V_HEAD_DIM)],
        out_shape=[jax.ShapeDtypeStruct((batch, N_HEADS, seq, QK_PAD_DIM), BF16),
                   jax.ShapeDtypeStruct((batch, N_HEADS, seq, QK_PAD_DIM), BF16),
                   jax.ShapeDtypeStruct((batch, N_HEADS, seq, V_HEAD_DIM), BF16)],
        compiler_params=_params(("parallel", "parallel"), vmem),
        name="qkv",
    )(proj, proj, proj, cos_t, sin_t, gq, gkv, wa, wkv)


def _attn_body(q_ref, k_ref, v_ref, o_ref, *, tq):
    heads, seq = q_ref.shape[1], q_ref.shape[2]
    work = [(h, c) for h in range(heads) for c in range(seq // tq)]

    def scores(h, c):
        return lax.dot_general(q_ref[0, h, c * tq:(c + 1) * tq, :], k_ref[0, h], (((1,), (1,)), ((), ())),
                               preferred_element_type=F32)

    s_next = scores(*work[0])
    for i, (h, c) in enumerate(work):
        s = s_next
        if i + 1 < len(work):
            s_next = scores(*work[i + 1])
        m = jnp.max(s, axis=-1, keepdims=True)
        e = jnp.exp2(s - m)
        l = jnp.sum(e, axis=-1, keepdims=True)
        o = _bdot(e.astype(BF16), v_ref[0, h])
        o_ref[0, c * tq:(c + 1) * tq, h * V_HEAD_DIM:(h + 1) * V_HEAD_DIM] = (o / l).astype(o_ref.dtype)


def _attn(q, k, v):
    batch, heads, seq, _ = q.shape
    tq = 256
    hps = 2
    hspec = lambda w: pl.BlockSpec((1, hps, seq, w), lambda b, h: (b, h, 0, 0))
    vmem = 2 * hps * seq * (2 * QK_PAD_DIM + 2 * V_HEAD_DIM) * 2 + 12 * tq * seq * 4
    return pl.pallas_call(
        functools.partial(_attn_body, tq=tq),
        grid=(batch, heads // hps),
        in_specs=[hspec(QK_PAD_DIM), hspec(QK_PAD_DIM), hspec(V_HEAD_DIM)],
        out_specs=pl.BlockSpec((1, seq, hps * V_HEAD_DIM), lambda b, h: (b, 0, h)),
        out_shape=jax.ShapeDtypeStruct((batch, seq, heads * V_HEAD_DIM), BF16),
        compiler_params=_params(("parallel", "parallel"), vmem),
        name="attn",
    )(q, k, v)


def _outproj_body(yh_ref, ya_ref, gh_ref, ga_ref, w_ref, x_ref, gpm_ref, gpf_ref, wd_ref, h_ref, hn_ref, wdb_ref,
                  *, rows):
    wdb_ref[...] = wd_ref[...].astype(BF16)
    n = x_ref.shape[0] // rows

    def mix(c):
        rs = slice(c * rows, (c + 1) * rows)
        a = _rms(yh_ref[rs, :].astype(F32), gh_ref[...]).astype(BF16)
        b = _rms(ya_ref[rs, :].astype(F32), ga_ref[...]).astype(BF16)
        return _bdot(a, w_ref[0:HYENA_WIDTH, :]) + _bdot(b, w_ref[HYENA_WIDTH:, :])

    nxt = mix(0)
    for c in range(n):
        mixed = nxt
        if c + 1 < n:
            nxt = mix(c + 1)
        rs = slice(c * rows, (c + 1) * rows)
        h = x_ref[rs, :] + _rms(mixed, gpm_ref[...])
        h_ref[rs, :] = h
        hn_ref[rs, :] = _rms(h, gpf_ref[...]).astype(BF16)


def _outproj(yh, ya, gh, ga, w_out, x2d, gpm, gpf, w_down):
    m, d = x2d.shape
    tm = 512
    full = lambda a: pl.BlockSpec(a.shape, lambda i: (0,) * a.ndim)
    rows = lambda w: pl.BlockSpec((tm, w), lambda i: (i, 0))
    wd_rows = pl.BlockSpec((w_down.shape[0] // (m // tm), d), lambda i: (i, 0))
    vmem = 2 * 2 * tm * HYENA_WIDTH * 2 + 2 * d * d * 2 + 2 * tm * d * (4 + 4 + 2) + 4 * tm * d * 4
    return pl.pallas_call(
        functools.partial(_outproj_body, rows=tm // 2),
        grid=(m // tm,),
        in_specs=[rows(HYENA_WIDTH), rows(ATTN_WIDTH), full(gh), full(ga), full(w_out), rows(d),
                  full(gpm), full(gpf), wd_rows],
        out_specs=[rows(d), rows(d), wd_rows],
        out_shape=[jax.ShapeDtypeStruct((m, d), F32), jax.ShapeDtypeStruct((m, d), BF16),
                   jax.ShapeDtypeStruct(w_down.shape, BF16)],
        compiler_params=_params(("parallel",), vmem),
        name="outproj",
    )(yh, ya, gh, ga, w_out, x2d, gpm, gpf, w_down)


def _ffn_up_body(hn_ref, wg_ref, wu_ref, cw_ref, cb_ref, o_ref, wgb_ref, wub_ref, *, rows):
    @pl.when(pl.program_id(1) == 0)
    def _():
        wgb_ref[...] = wg_ref[...].astype(BF16)
        wub_ref[...] = wu_ref[...].astype(BF16)

    seq = hn_ref.shape[0]
    wg, wu = wgb_ref[...], wub_ref[...]
    for r0 in range(0, seq, rows):
        lo, hi = _chunk_bounds(r0, rows, seq)
        g = _seq_conv3_chunk(_bdot(hn_ref[lo:hi, :], wg), cw_ref[...], cb_ref[...], r0, rows, seq)
        u = _bdot(hn_ref[r0:r0 + rows, :], wu)
        gelu = 0.5 * g * (1.0 + jnp.tanh(math.sqrt(2.0 / math.pi) * (g + 0.044715 * (g * g * g))))
        o_ref[r0:r0 + rows, :] = (gelu * u).astype(o_ref.dtype)


def _ffn_up(hn, w_up, conv_w, conv_b, batch, seq):
    d = hn.shape[1]
    tn = 512
    nb = D_FF // tn
    wbf = pltpu.VMEM((d, tn), BF16)
    vmem = 2 * seq * d * 2 + 2 * 2 * d * tn * 4 + 2 * d * tn * 2 + 2 * seq * tn * 2 + 8 * seq * tn * 4
    return pl.pallas_call(
        functools.partial(_ffn_up_body, rows=1024),
        grid=(nb, batch),
        in_specs=[pl.BlockSpec((seq, d), lambda j, b: (b, 0)),
                  pl.BlockSpec((d, tn), lambda j, b: (0, j)),
                  pl.BlockSpec((d, tn), lambda j, b: (0, j + nb)),
                  pl.BlockSpec((CONV_WIDTH, tn), lambda j, b: (0, j)),
                  pl.BlockSpec((1, tn), lambda j, b: (0, j))],
        out_specs=pl.BlockSpec((seq, tn), lambda j, b: (b, j)),
        out_shape=jax.ShapeDtypeStruct((batch * seq, D_FF), BF16),
        scratch_shapes=[wbf, wbf],
        compiler_params=_params(("parallel", "arbitrary"), vmem),
        name="ffn_up",
    )(hn, w_up, w_up, conv_w, conv_b)


def _ffn_down_body(a_ref, w_ref, h_ref, g_ref, o_ref, *, rows):
    for r0 in range(0, a_ref.shape[0], rows):
        rs = slice(r0, r0 + rows)
        o_ref[rs, :] = h_ref[rs, :] + _rms(_bdot(a_ref[rs, :], w_ref[...]), g_ref[...])


def _ffn_down(act, w_down, h, gain):
    m, d = h.shape
    k = act.shape[1]
    tm = 512
    vmem = k * d * 2 + 2 * tm * k * 2 + 2 * 2 * tm * d * 4 + tm * d * 4
    return pl.pallas_call(
        functools.partial(_ffn_down_body, rows=tm // 2),
        grid=(m // tm,),
        in_specs=[pl.BlockSpec((tm, k), lambda i: (i, 0)),
                  pl.BlockSpec((k, d), lambda i: (0, 0), pipeline_mode=pl.Buffered(1)),
                  pl.BlockSpec((tm, d), lambda i: (i, 0)),
                  pl.BlockSpec((1, d), lambda i: (0, 0))],
        out_specs=pl.BlockSpec((tm, d), lambda i: (i, 0)),
        out_shape=jax.ShapeDtypeStruct((m, d), F32),
        compiler_params=_params(("arbitrary",), vmem),
        name="ffn_down",
    )(act, w_down, h, gain)


def _position_features(l):
    t = jnp.linspace(0.0, 1.0, l, dtype=F32)[:, None]
    bands = (FILTER_EMB_DIM - 1) // 2
    w = 2.0 * math.pi * jnp.arange(l, dtype=F32) / l
    f = jnp.linspace(1e-4, bands - 1, bands, dtype=F32)
    ang = w[:, None] * f[None, :]
    return t, jnp.concatenate([t, jnp.cos(ang), -jnp.sin(ang)], axis=-1)


def _layer(h, l, seq, prm):
    (pre_mix_gain, w_in, hyena_conv_w, hyena_conv_b, filt_w1, filt_b1, filt_freq1, filt_w2, filt_b2,
     filt_freq2, filt_w3, hyena_bias, q_norm_gain, w_uq, kv_norm_gain, w_ukv, hyena_out_gain, attn_out_gain,
     w_out, post_mix_gain, pre_ffn_gain, w_up, ffn_conv_w, ffn_conv_b, w_down, post_ffn_gain) = prm
    batch = h.shape[0]
    p = seq // 2
    x2d = h.reshape(batch * seq, D_MODEL)
    row = lambda a: a[l][None, :].astype(F32)

    wt = jnp.swapaxes(w_in[l], 0, 1)
    w_tail = jnp.pad(wt[COL_CQ:].astype(BF16), ((0, PROJ_WIDTH - wt.shape[0]), (0, 0)))
    proj = _inproj(x2d, row(pre_mix_gain), wt[:COL_CQ].astype(BF16), w_tail)

    cmat, sfwd, sinv = _dft_mats(p)
    t, z = _position_features(seq)
    zp = jnp.pad(z, ((0, 0), (0, V7X_LANES - FILTER_EMB_DIM)))
    w1p = jnp.pad(filt_w1[l], ((0, V7X_LANES - FILTER_EMB_DIM), (0, 0)))
    max_decay = math.log(DECAY_TARGET) / FAST_DECAY_PCT
    min_decay = math.log(DECAY_TARGET) / SLOW_DECAY_PCT
    deltas = jnp.abs(jnp.linspace(min_decay, max_decay, HYENA_WIDTH, dtype=F32))[None, :]
    kspec = _filter_spectra(zp, t, w1p, row(filt_b1), row(filt_freq1), filt_w2[l], row(filt_b2),
                            row(filt_freq2), filt_w3[l], deltas, cmat, sfwd, p)
    conv_w = hyena_conv_w[l].reshape(CONV_WIDTH, 3, HYENA_WIDTH).transpose(1, 0, 2)
    conv_b = hyena_conv_b[l].reshape(3, 1, HYENA_WIDTH)
    y_hyena = _hyena(proj, conv_w, conv_b, kspec, row(hyena_bias), cmat, sfwd, sinv, batch, seq, p)

    pos = jnp.arange(seq, dtype=F32)
    inv_freq = 1.0 / (ROPE_THETA ** (jnp.arange(0, QK_ROPE_DIM, 2, dtype=F32) / QK_ROPE_DIM))
    ang = pos[:, None] * inv_freq[None, :]
    ang = jnp.concatenate([ang, ang], axis=-1)
    lpad = ((0, 0), (0, V7X_LANES - QK_ROPE_DIM))
    cos_t = jnp.pad(jnp.cos(ang), lpad)
    sin_t = jnp.pad(jnp.sin(ang), lpad)
    dqk = QK_NOPE_DIM + QK_ROPE_DIM
    wq = w_uq[l].reshape(Q_LORA_RANK, N_HEADS, dqk)
    wa = jnp.pad(wq, ((0, 0), (0, 0), (0, QK_PAD_DIM - dqk))).reshape(Q_LORA_RANK, N_HEADS * QK_PAD_DIM)
    q, k, v = _qkv(proj, cos_t, sin_t, row(q_norm_gain), row(kv_norm_gain), wa.astype(BF16),
                   w_ukv[l].astype(BF16), batch, seq)
    y_attn = _attn(q, k, v).reshape(batch * seq, ATTN_WIDTH)

    h2d, hn, w_down_bf = _outproj(y_hyena, y_attn, row(hyena_out_gain), row(attn_out_gain), w_out[l].astype(BF16),
                                  x2d, row(post_mix_gain), row(pre_ffn_gain), w_down[l])

    act = _ffn_up(hn, w_up[l], ffn_conv_w[l], row(ffn_conv_b), batch, seq)
    out = _ffn_down(act, w_down_bf, h2d, row(post_ffn_gain))
    return out.reshape(batch, seq, D_MODEL)


def kernel(x, pre_mix_gain, w_in, hyena_conv_w, hyena_conv_b, filt_w1, filt_b1, filt_freq1, filt_w2, filt_b2,
           filt_freq2, filt_w3, hyena_bias, q_norm_gain, w_uq, kv_norm_gain, w_ukv, hyena_out_gain,
           attn_out_gain, w_out, post_mix_gain, pre_ffn_gain, w_up, ffn_conv_w, ffn_conv_b, w_down,
           post_ffn_gain):
    prm = (pre_mix_gain, w_in, hyena_conv_w, hyena_conv_b, filt_w1, filt_b1, filt_freq1, filt_w2, filt_b2,
           filt_freq2, filt_w3, hyena_bias, q_norm_gain, w_uq, kv_norm_gain, w_ukv, hyena_out_gain,
           attn_out_gain, w_out, post_mix_gain, pre_ffn_gain, w_up, ffn_conv_w, ffn_conv_b, w_down,
           post_ffn_gain)
    seq = x.shape[1]
    h = x
    for l in range(w_in.shape[0]):
        h = _layer(h, l, seq, prm)
    return h
```

```python
import functools
import math

import jax
import jax.numpy as jnp
from jax import lax
from jax.experimental import pallas as pl
from jax.experimental.pallas import tpu as pltpu

F32 = jnp.float32
BF16 = jnp.bfloat16

D_MODEL = 2048
HYENA_WIDTH = 1024
CONV_WIDTH = 3
FILTER_EMB_DIM = 33
FILTER_HIDDEN = 64
DECAY_TARGET = 1e-2
FAST_DECAY_PCT = 0.3
SLOW_DECAY_PCT = 1.5
DECAY_SHIFT = 0.05
N_HEADS = 8
QK_NOPE_DIM = 128
QK_ROPE_DIM = 64
V_HEAD_DIM = 128
Q_LORA_RANK = 512
KV_LORA_RANK = 256
ROPE_THETA = 10000.0
ATTN_WIDTH = N_HEADS * V_HEAD_DIM
D_FF = 5632
NORM_EPS = 1e-6

V7X_VMEM_BYTES = 64 * 1024 * 1024
V7X_LANES = 128
V7X_MXU_DIM = 256
V7X_SUBLANES = 8
HALO_BF16 = 2 * V7X_SUBLANES
VMEM_RESERVE_BYTES = 4 * 1024 * 1024
VMEM_TEMP_BYTES = 6 * 1024 * 1024

QK_PAD_DIM = V7X_MXU_DIM
PROJ_WIDTH = 4096
COL_CQ = 3 * HYENA_WIDTH
COL_CKV = COL_CQ + Q_LORA_RANK
COL_KPE = COL_CKV + KV_LORA_RANK


def _params(semantics, block_bytes):
    limit = min(int(block_bytes) + VMEM_TEMP_BYTES, V7X_VMEM_BYTES - VMEM_RESERVE_BYTES)
    return pltpu.CompilerParams(dimension_semantics=semantics, vmem_limit_bytes=limit)


def _rms(x, gain):
    return x * lax.rsqrt(jnp.mean(x * x, axis=-1, keepdims=True) + NORM_EPS) * gain


def _bdot(a, b):
    return jnp.dot(a, b, preferred_element_type=F32)


def _chunk_bounds(r0, rows, seq):
    return max(r0 - HALO_BF16, 0), min(r0 + rows + HALO_BF16, seq)


def _seq_conv3_chunk(xe, w, b, r0, rows, seq):
    lo, hi = _chunk_bounds(r0, rows, seq)
    n = hi - lo
    prev = pltpu.roll(xe, 1, 0)
    nxt = pltpu.roll(xe, n - 1, 0)
    if lo == 0 or hi == seq:
        row = lax.broadcasted_iota(jnp.int32, xe.shape, 0)
        if lo == 0:
            prev = jnp.where(row == 0, 0.0, prev)
        if hi == seq:
            nxt = jnp.where(row == n - 1, 0.0, nxt)
    return (prev * w[0:1] + xe * w[1:2] + nxt * w[2:3] + b)[r0 - lo:r0 - lo + rows]


def _inproj_body(x_ref, g_ref, wm_ref, wt_ref, o_ref, *, n_main, rows):
    j = pl.program_id(1)

    def project(w_ref):
        w = w_ref[...]
        for r0 in range(0, x_ref.shape[0], rows):
            xn = _rms(x_ref[r0:r0 + rows, :], g_ref[...]).astype(BF16)
            o_ref[r0:r0 + rows, :] = lax.dot_general(xn, w, (((1,), (1,)), ((), ())),
                                                     preferred_element_type=F32).astype(o_ref.dtype)

    @pl.when(j < n_main)
    def _():
        project(wm_ref)

    @pl.when(j == n_main)
    def _():
        project(wt_ref)


def _inproj(x2d, gain, w_all, w_tail, n_main_rows):
    m, d = x2d.shape
    tm, tn = 1024, w_tail.shape[0]
    n_main = n_main_rows // tn
    vmem = 2 * tm * d * 4 + tm * d * 2 + 4 * d * tn * 2 + 2 * tm * tn * 2 + tm * tn * 4
    return pl.pallas_call(
        functools.partial(_inproj_body, n_main=n_main, rows=256),
        grid=(m // tm, n_main + 1),
        in_specs=[pl.BlockSpec((tm, d), lambda i, j: (i, 0)),
                  pl.BlockSpec((1, d), lambda i, j: (0, 0)),
                  pl.BlockSpec((tn, d), lambda i, j: (jnp.minimum(j, n_main - 1), 0)),
                  pl.BlockSpec((tn, d), lambda i, j: (0, 0))],
        out_specs=pl.BlockSpec((tm, tn), lambda i, j: (i, j)),
        out_shape=jax.ShapeDtypeStruct((m, (n_main + 1) * tn), BF16),
        compiler_params=_params(("parallel", "arbitrary"), vmem),
        name="inproj",
    )(x2d, gain, w_all, w_tail)


def _dft_mats(p):
    r = 32
    assert p == r * r
    idx = jnp.arange(p, dtype=jnp.int32)
    sub = jnp.arange(r, dtype=jnp.int32)
    ang_hi = ((idx[:, None] * (r * sub)[None, :]) % (2 * p)).astype(F32) * (math.pi / p)
    ang_lo = ((idx[:, None] * sub[None, :]) % (2 * p)).astype(F32) * (math.pi / p)
    ch, sh = jnp.cos(ang_hi)[:, :, None], jnp.sin(ang_hi)[:, :, None]
    cl, sl = jnp.cos(ang_lo)[:, None, :], jnp.sin(ang_lo)[:, None, :]
    c = (ch * cl - sh * sl).reshape(p, p)
    s = -(sh * cl + ch * sl).reshape(p, p)
    alt = jnp.where(idx % 2 == 0, 1.0, -1.0).astype(F32)
    s_fwd = jnp.where(idx[:, None] == 0, alt[None, :], s)
    s_inv = jnp.where(idx[None, :] == 0, alt[:, None], s)
    return c.astype(BF16), s_fwd.astype(BF16), s_inv.astype(BF16)


def _filter_body(z_ref, t_ref, w1_ref, b1_ref, f1_ref, w2_ref, b2_ref, f2_ref, w3f_ref, w3b_ref,
                 dl_ref, c_ref, s_ref, o_ref, hid_ref, *, p):
    hp = lax.Precision.HIGHEST

    @pl.when(pl.program_id(0) == 0)
    def _():
        h1 = jnp.sin(f1_ref[...] * (jnp.dot(z_ref[...], w1_ref[...], precision=hp,
                                            preferred_element_type=F32) + b1_ref[...]))
        hid_ref[...] = jnp.sin(f2_ref[...] * (jnp.dot(h1, w2_ref[...], precision=hp,
                                                      preferred_element_type=F32) + b2_ref[...]))

    h = hid_ref[...]
    win = jnp.exp(-t_ref[...] * dl_ref[...]) + DECAY_SHIFT
    hf = jnp.dot(h, w3f_ref[...], precision=hp, preferred_element_type=F32) * win
    hb = jnp.dot(h, w3b_ref[...], precision=hp, preferred_element_type=F32) * win
    lrow = lax.broadcasted_iota(jnp.int32, hb.shape, 0)
    hb = jnp.where(lrow == 0, 0.0, hb)

    cmat = c_ref[...]
    smat = s_ref[...]

    def fwd(x):
        hi = x.astype(BF16)
        lo = (x - hi.astype(F32)).astype(BF16)
        return (_bdot(cmat, hi) + _bdot(cmat, lo), _bdot(smat, hi) + _bdot(smat, lo))

    row = lax.broadcasted_iota(jnp.int32, (p, hf.shape[1]), 0)
    row0 = row == 0
    sigma = jnp.where(row % 2 == 1, -1.0, 1.0)

    def conj(a):
        return a[0], jnp.where(row0, a[1], -a[1])

    af0, af1 = fwd(hf[:p]), fwd(hf[p:])
    ab0, ab1 = fwd(hb[:p]), fwd(hb[p:])
    cb0 = conj(ab0)
    k0 = (af0[0] + cb0[0], af0[1] + cb0[1])
    k1 = (af1[0] + sigma * af0[0], af1[1] + sigma * af0[1])
    km1 = conj((ab1[0] + sigma * ab0[0], ab1[1] + sigma * ab0[1]))
    scale = jnp.where(row0, 0.5 / p, 1.0 / p)
    for i, a in enumerate((k0, k1, km1)):
        o_ref[2 * i] = a[0] * scale
        o_ref[2 * i + 1] = a[1] * scale


def _filter_spectra(z, t, w1, b1, f1, w2, b2, f2, w3, deltas, cmat, smat, p):
    l = z.shape[0]
    c = HYENA_WIDTH
    tc = 256
    nb = c // tc
    full = lambda a: pl.BlockSpec(a.shape, lambda j: (0,) * a.ndim)
    return pl.pallas_call(
        functools.partial(_filter_body, p=p),
        grid=(nb,),
        in_specs=[full(z), full(t), full(w1), full(b1), full(f1), full(w2), full(b2), full(f2),
                  pl.BlockSpec((FILTER_HIDDEN, tc), lambda j: (0, j)),
                  pl.BlockSpec((FILTER_HIDDEN, tc), lambda j: (0, j + nb)),
                  pl.BlockSpec((1, tc), lambda j: (0, j)),
                  full(cmat), full(smat)],
        out_specs=pl.BlockSpec((6, p, tc), lambda j: (0, 0, j)),
        out_shape=jax.ShapeDtypeStruct((6, p, c), F32),
        scratch_shapes=[pltpu.VMEM((l, FILTER_HIDDEN), F32)],
        compiler_params=_params(("arbitrary",), 2 * 6 * p * tc * 4 + 2 * 2 * p * p * 2 + 10 * l * tc * 4),
        name="filt",
    )(z, t, w1, b1, f1, w2, b2, f2, w3, w3, deltas, cmat, smat)


def _hyena_body(x0_ref, x1_ref, v_ref, cw_ref, cb_ref, ks_ref, hb_ref, c_ref, sf_ref, si_ref, o_ref, *, p):
    cmat = c_ref[...]
    sfwd = sf_ref[...]
    sinv = si_ref[...]
    seq = 2 * p
    row0 = lax.broadcasted_iota(jnp.int32, (p, o_ref.shape[1]), 0) == 0

    def cmul(i, uu):
        kre, kim = ks_ref[2 * i], ks_ref[2 * i + 1]
        ii = kim * uu[1]
        return (kre * uu[0] - jnp.where(row0, 0.0, ii),
                jnp.where(row0, ii, kre * uu[1] + kim * uu[0]))

    def inv(a, b):
        return _bdot(cmat, (a[0] + b[0]).astype(BF16)) + _bdot(sinv, (a[1] + b[1]).astype(BF16))

    def conv(x_ref, i, r0):
        lo, hi = _chunk_bounds(r0, p, seq)
        return _seq_conv3_chunk(x_ref[lo:hi, :].astype(F32), cw_ref[i], cb_ref[i], r0, p, seq)

    us, spec = [], []
    for j in range(2):
        u = conv(x1_ref, 1, j * p) * conv(v_ref, 2, j * p)
        ub = u.astype(BF16)
        us.append(u)
        spec.append((_bdot(cmat, ub), _bdot(sfwd, ub)))
    for j, (ia, ib) in enumerate(((0, 2), (1, 0))):
        y = inv(cmul(ia, spec[0]), cmul(ib, spec[1]))
        x0 = conv(x0_ref, 0, j * p)
        o_ref[j * p:(j + 1) * p, :] = (x0 * (y + us[j] * hb_ref[...])).astype(o_ref.dtype)


def _hyena(proj, conv_w, conv_b, kspec, hbias, cmat, sfwd, sinv, batch, seq, p):
    c = HYENA_WIDTH
    tc = 256
    nb = c // tc
    xspec = lambda off: pl.BlockSpec((seq, tc), lambda j, b: (b, j + off * nb))
    full = lambda a: pl.BlockSpec(a.shape, lambda j, b: (0,) * a.ndim)
    vmem = (6 * seq * tc * 2 + 2 * 6 * p * tc * 4 + 6 * p * p * 2 + 2 * seq * tc * 2
            + 6 * seq * tc * 4)
    return pl.pallas_call(
        functools.partial(_hyena_body, p=p),
        grid=(nb, batch),
        in_specs=[xspec(0), xspec(1), xspec(2),
                  pl.BlockSpec((3, CONV_WIDTH, tc), lambda j, b: (0, 0, j)),
                  pl.BlockSpec((3, 1, tc), lambda j, b: (0, 0, j)),
                  pl.BlockSpec((6, p, tc), lambda j, b: (0, 0, j)),
                  pl.BlockSpec((1, tc), lambda j, b: (0, j)),
                  full(cmat), full(sfwd), full(sinv)],
        out_specs=pl.BlockSpec((seq, tc), lambda j, b: (b, j)),
        out_shape=jax.ShapeDtypeStruct((batch * seq, c), BF16),
        compiler_params=_params(("parallel", "parallel"), vmem),
        name="hyena",
    )(proj, proj, proj, conv_w, conv_b, kspec, hbias, cmat, sfwd, sinv)


def _rotate_half_lanes(x):
    half = QK_ROPE_DIM // 2
    lane = lax.broadcasted_iota(jnp.int32, x.shape, 1)
    return jnp.where(lane < half, -pltpu.roll(x, V7X_LANES - half, 1), pltpu.roll(x, half, 1))


def _qkv_body(cq_ref, ckv_ref, kpe_ref, cos_ref, sin_ref, gq_ref, gkv_ref, wa_ref, wkv_ref, wo_ref,
              q_ref, k_ref, v_ref, wob_ref, *, rows):
    wob_ref[...] = wo_ref[...].astype(BF16)
    scale = (QK_NOPE_DIM + QK_ROPE_DIM) ** -0.5 * math.log2(math.e)
    n = cq_ref.shape[0] // rows

    def project(c):
        rs = slice(c * rows, (c + 1) * rows)
        cqn = _rms(cq_ref[rs, :].astype(F32), gq_ref[...]).astype(BF16)
        ckvn = _rms(ckv_ref[rs, :].astype(F32), gkv_ref[...]).astype(BF16)
        return (_bdot(cqn, wa_ref[...]),
                _bdot(ckvn, wkv_ref[...]))

    nxt = project(0)
    for c in range(n):
        qa, kv = nxt
        if c + 1 < n:
            nxt = project(c + 1)
        rs = slice(c * rows, (c + 1) * rows)
        cos = cos_ref[rs, :]
        sin = sin_ref[rs, :]
        kpe = kpe_ref[rs, :].astype(F32)
        kpe = (kpe * cos + _rotate_half_lanes(kpe) * sin).astype(BF16)
        for h in range(N_HEADS):
            a = h * QK_PAD_DIM
            qpe = qa[:, a + 128:a + 256]
            q_ref[0, h, rs, 0:128] = (qa[:, a:a + 128] * scale).astype(BF16)
            q_ref[0, h, rs, 128:256] = ((qpe * cos + _rotate_half_lanes(qpe) * sin) * scale).astype(BF16)
            k_ref[0, h, rs, 0:128] = kv[:, a:a + 128].astype(BF16)
            k_ref[0, h, rs, 128:256] = kpe
            v_ref[0, h, rs, :] = kv[:, a + 128:a + 256].astype(BF16)


def _qkv(proj, cos_t, sin_t, gq, gkv, wa, wkv, w_out, batch, seq):
    tm = 512
    ns = seq // tm
    wo_rows = pl.BlockSpec((w_out.shape[0] // (batch * ns), w_out.shape[1]), lambda b, i: (b * ns + i, 0))
    full = lambda a: pl.BlockSpec(a.shape, lambda b, i: (0,) * a.ndim)
    col = lambda width, off: pl.BlockSpec((tm, width), lambda b, i: (b * ns + i, off // width))
    hd = lambda w: pl.BlockSpec((1, N_HEADS, tm, w), lambda b, i: (b, 0, i, 0))
    vmem = (2 * tm * (Q_LORA_RANK + KV_LORA_RANK + V7X_LANES) * 2 + 4 * tm * V7X_LANES * 4 + 2 * (wa.size + wkv.size) * 2
            + 2 * N_HEADS * tm * (2 * QK_PAD_DIM + V_HEAD_DIM) * 2 + 4 * tm * N_HEADS * QK_PAD_DIM * 4)
    return pl.pallas_call(
        functools.partial(_qkv_body, rows=tm // 2),
        grid=(batch, ns),
        in_specs=[col(Q_LORA_RANK, COL_CQ), col(KV_LORA_RANK, COL_CKV), col(V7X_LANES, COL_KPE),
                  pl.BlockSpec((tm, V7X_LANES), lambda b, i: (i, 0)),
                  pl.BlockSpec((tm, V7X_LANES), lambda b, i: (i, 0)),
                  full(gq), full(gkv), full(wa), full(wkv), wo_rows],
        out_specs=[hd(QK_PAD_DIM), hd(QK_PAD_DIM), hd(V_HEAD_DIM), wo_rows],
        out_shape=[jax.ShapeDtypeStruct((batch, N_HEADS, seq, QK_PAD_DIM), BF16),
                   jax.ShapeDtypeStruct((batch, N_HEADS, seq, QK_PAD_DIM), BF16),
                   jax.ShapeDtypeStruct((batch, N_HEADS, seq, V_HEAD_DIM), BF16),
                   jax.ShapeDtypeStruct(w_out.shape, BF16)],
        compiler_params=_params(("parallel", "parallel"), vmem),
        name="qkv",
    )(proj, proj, proj, cos_t, sin_t, gq, gkv, wa, wkv, w_out)


def _attn_body(q_ref, k_ref, v_ref, o_ref, *, tq):
    heads, seq = q_ref.shape[1], q_ref.shape[2]
    work = [(h, c) for h in range(heads) for c in range(seq // tq)]

    def scores(h, c):
        return lax.dot_general(q_ref[0, h, c * tq:(c + 1) * tq, :], k_ref[0, h], (((1,), (1,)), ((), ())),
                               preferred_element_type=F32)

    s_next = scores(*work[0])
    for i, (h, c) in enumerate(work):
        s = s_next
        if i + 1 < len(work):
            s_next = scores(*work[i + 1])
        m = jnp.max(s, axis=-1, keepdims=True)
        e = jnp.exp2(s - m)
        l = jnp.sum(e, axis=-1, keepdims=True)
        o = _bdot(e.astype(BF16), v_ref[0, h])
        o_ref[0, c * tq:(c + 1) * tq, h * V_HEAD_DIM:(h + 1) * V_HEAD_DIM] = (o / l).astype(o_ref.dtype)


def _attn(q, k, v):
    batch, heads, seq, _ = q.shape
    tq = 256
    hps = 2
    hspec = lambda w: pl.BlockSpec((1, hps, seq, w), lambda b, h: (b, h, 0, 0))
    vmem = 2 * hps * seq * (2 * QK_PAD_DIM + 2 * V_HEAD_DIM) * 2 + 12 * tq * seq * 4
    return pl.pallas_call(
        functools.partial(_attn_body, tq=tq),
        grid=(batch, heads // hps),
        in_specs=[hspec(QK_PAD_DIM), hspec(QK_PAD_DIM), hspec(V_HEAD_DIM)],
        out_specs=pl.BlockSpec((1, seq, hps * V_HEAD_DIM), lambda b, h: (b, 0, h)),
        out_shape=jax.ShapeDtypeStruct((batch, seq, heads * V_HEAD_DIM), BF16),
        compiler_params=_params(("parallel", "parallel"), vmem),
        name="attn",
    )(q, k, v)


def _outproj_body(yh_ref, ya_ref, gh_ref, ga_ref, w_ref, x_ref, gpm_ref, gpf_ref, wd_ref, h_ref, hn_ref, wdb_ref,
                  *, rows):
    wdb_ref[...] = wd_ref[...].astype(BF16)
    n = x_ref.shape[0] // rows

    def mix(c):
        rs = slice(c * rows, (c + 1) * rows)
        a = _rms(yh_ref[rs, :].astype(F32), gh_ref[...]).astype(BF16)
        b = _rms(ya_ref[rs, :].astype(F32), ga_ref[...]).astype(BF16)
        return _bdot(a, w_ref[0:HYENA_WIDTH, :]) + _bdot(b, w_ref[HYENA_WIDTH:, :])

    nxt = mix(0)
    for c in range(n):
        mixed = nxt
        if c + 1 < n:
            nxt = mix(c + 1)
        rs = slice(c * rows, (c + 1) * rows)
        h = x_ref[rs, :] + _rms(mixed, gpm_ref[...])
        h_ref[rs, :] = h
        hn_ref[rs, :] = _rms(h, gpf_ref[...]).astype(BF16)


def _outproj(yh, ya, gh, ga, w_out, x2d, gpm, gpf, w_down):
    m, d = x2d.shape
    tm = 512
    full = lambda a: pl.BlockSpec(a.shape, lambda i: (0,) * a.ndim)
    rows = lambda w: pl.BlockSpec((tm, w), lambda i: (i, 0))
    wd_rows = pl.BlockSpec((w_down.shape[0] // (m // tm), d), lambda i: (i, 0))
    vmem = 2 * 2 * tm * HYENA_WIDTH * 2 + 2 * d * d * 2 + 2 * tm * d * (4 + 4 + 2) + 4 * tm * d * 4
    return pl.pallas_call(
        functools.partial(_outproj_body, rows=tm // 2),
        grid=(m // tm,),
        in_specs=[rows(HYENA_WIDTH), rows(ATTN_WIDTH), full(gh), full(ga), full(w_out), rows(d),
                  full(gpm), full(gpf), wd_rows],
        out_specs=[rows(d), rows(d), wd_rows],
        out_shape=[jax.ShapeDtypeStruct((m, d), F32), jax.ShapeDtypeStruct((m, d), BF16),
                   jax.ShapeDtypeStruct(w_down.shape, BF16)],
        compiler_params=_params(("parallel",), vmem),
        name="outproj",
    )(yh, ya, gh, ga, w_out, x2d, gpm, gpf, w_down)


def _ffn_up_body(hn_ref, wg_ref, wu_ref, cw_ref, cb_ref, o_ref, wgb_ref, wub_ref, *, chunks):
    @pl.when(pl.program_id(1) == 0)
    def _():
        wgb_ref[...] = wg_ref[...].astype(BF16)
        wub_ref[...] = wu_ref[...].astype(BF16)

    seq = hn_ref.shape[0]
    wg, wu = wgb_ref[...], wub_ref[...]
    assert sum(chunks) == seq
    starts = [sum(chunks[:i]) for i in range(len(chunks))]
    for r0, rows in zip(starts, chunks):
        lo, hi = _chunk_bounds(r0, rows, seq)
        g = _seq_conv3_chunk(_bdot(hn_ref[lo:hi, :], wg), cw_ref[...], cb_ref[...], r0, rows, seq)
        u = _bdot(hn_ref[r0:r0 + rows, :], wu)
        gelu = 0.5 * g * (1.0 + jnp.tanh(math.sqrt(2.0 / math.pi) * (g + 0.044715 * (g * g * g))))
        o_ref[r0:r0 + rows, :] = (gelu * u).astype(o_ref.dtype)


def _ffn_up(hn, w_up, conv_w, conv_b, batch, seq):
    d = hn.shape[1]
    tn = 512
    nb = D_FF // tn
    wbf = pltpu.VMEM((d, tn), BF16)
    vmem = 2 * seq * d * 2 + 2 * 2 * d * tn * 4 + 2 * d * tn * 2 + 2 * seq * tn * 2 + 8 * seq * tn * 4
    return pl.pallas_call(
        functools.partial(_ffn_up_body, chunks=(seq // 2, seq // 2)),
        grid=(nb, batch),
        in_specs=[pl.BlockSpec((seq, d), lambda j, b: (b, 0)),
                  pl.BlockSpec((d, tn), lambda j, b: (0, j)),
                  pl.BlockSpec((d, tn), lambda j, b: (0, j + nb)),
                  pl.BlockSpec((CONV_WIDTH, tn), lambda j, b: (0, j)),
                  pl.BlockSpec((1, tn), lambda j, b: (0, j))],
        out_specs=pl.BlockSpec((seq, tn), lambda j, b: (b, j)),
        out_shape=jax.ShapeDtypeStruct((batch * seq, D_FF), BF16),
        scratch_shapes=[wbf, wbf],
        compiler_params=_params(("parallel", "arbitrary"), vmem),
        name="ffn_up",
    )(hn, w_up, w_up, conv_w, conv_b)


def _ffn_down_body(a_ref, w_ref, h_ref, g_ref, o_ref, *, rows):
    for r0 in range(0, a_ref.shape[0], rows):
        rs = slice(r0, r0 + rows)
        o_ref[rs, :] = h_ref[rs, :] + _rms(_bdot(a_ref[rs, :], w_ref[...]), g_ref[...])


def _ffn_down(act, w_down, h, gain):
    m, d = h.shape
    k = act.shape[1]
    tm = 512
    vmem = k * d * 2 + 2 * tm * k * 2 + 2 * 2 * tm * d * 4 + tm * d * 4
    return pl.pallas_call(
        functools.partial(_ffn_down_body, rows=tm // 2),
        grid=(m // tm,),
        in_specs=[pl.BlockSpec((tm, k), lambda i: (i, 0)),
                  pl.BlockSpec((k, d), lambda i: (0, 0), pipeline_mode=pl.Buffered(1)),
                  pl.BlockSpec((tm, d), lambda i: (i, 0)),
                  pl.BlockSpec((1, d), lambda i: (0, 0))],
        out_specs=pl.BlockSpec((tm, d), lambda i: (i, 0)),
        out_shape=jax.ShapeDtypeStruct((m, d), F32),
        compiler_params=_params(("arbitrary",), vmem),
        name="ffn_down",
    )(act, w_down, h, gain)


def _position_features(l):
    t = jnp.linspace(0.0, 1.0, l, dtype=F32)[:, None]
    bands = (FILTER_EMB_DIM - 1) // 2
    w = 2.0 * math.pi * jnp.arange(l, dtype=F32) / l
    f = jnp.linspace(1e-4, bands - 1, bands, dtype=F32)
    ang = w[:, None] * f[None, :]
    return t, jnp.concatenate([t, jnp.cos(ang), -jnp.sin(ang)], axis=-1)


def _layer(h, l, seq, prm):
    (pre_mix_gain, w_in, hyena_conv_w, hyena_conv_b, filt_w1, filt_b1, filt_freq1, filt_w2, filt_b2,
     filt_freq2, filt_w3, hyena_bias, q_norm_gain, w_uq, kv_norm_gain, w_ukv, hyena_out_gain, attn_out_gain,
     w_out, post_mix_gain, pre_ffn_gain, w_up, ffn_conv_w, ffn_conv_b, w_down, post_ffn_gain) = prm
    batch = h.shape[0]
    p = seq // 2
    x2d = h.reshape(batch * seq, D_MODEL)
    row = lambda a: a[l][None, :].astype(F32)

    wt = jnp.swapaxes(w_in[l], 0, 1).astype(BF16)
    w_tail = jnp.pad(wt[COL_CQ:], ((0, PROJ_WIDTH - wt.shape[0]), (0, 0)))
    proj = _inproj(x2d, row(pre_mix_gain), wt, w_tail, COL_CQ)

    cmat, sfwd, sinv = _dft_mats(p)
    t, z = _position_features(seq)
    zp = jnp.pad(z, ((0, 0), (0, V7X_LANES - FILTER_EMB_DIM)))
    w1p = jnp.pad(filt_w1[l], ((0, V7X_LANES - FILTER_EMB_DIM), (0, 0)))
    max_decay = math.log(DECAY_TARGET) / FAST_DECAY_PCT
    min_decay = math.log(DECAY_TARGET) / SLOW_DECAY_PCT
    deltas = jnp.abs(jnp.linspace(min_decay, max_decay, HYENA_WIDTH, dtype=F32))[None, :]
    kspec = _filter_spectra(zp, t, w1p, row(filt_b1), row(filt_freq1), filt_w2[l], row(filt_b2),
                            row(filt_freq2), filt_w3[l], deltas, cmat, sfwd, p)
    conv_w = hyena_conv_w[l].reshape(CONV_WIDTH, 3, HYENA_WIDTH).transpose(1, 0, 2)
    conv_b = hyena_conv_b[l].reshape(3, 1, HYENA_WIDTH)
    y_hyena = _hyena(proj, conv_w, conv_b, kspec, row(hyena_bias), cmat, sfwd, sinv, batch, seq, p)

    pos = jnp.arange(seq, dtype=F32)
    inv_freq = 1.0 / (ROPE_THETA ** (jnp.arange(0, QK_ROPE_DIM, 2, dtype=F32) / QK_ROPE_DIM))
    ang = pos[:, None] * inv_freq[None, :]
    ang = jnp.concatenate([ang, ang], axis=-1)
    lpad = ((0, 0), (0, V7X_LANES - QK_ROPE_DIM))
    cos_t = jnp.pad(jnp.cos(ang), lpad)
    sin_t = jnp.pad(jnp.sin(ang), lpad)
    dqk = QK_NOPE_DIM + QK_ROPE_DIM
    wq = w_uq[l].reshape(Q_LORA_RANK, N_HEADS, dqk)
    wa = jnp.pad(wq, ((0, 0), (0, 0), (0, QK_PAD_DIM - dqk))).reshape(Q_LORA_RANK, N_HEADS * QK_PAD_DIM)
    q, k, v, w_out_bf = _qkv(proj, cos_t, sin_t, row(q_norm_gain), row(kv_norm_gain), wa.astype(BF16),
                             w_ukv[l].astype(BF16), w_out[l], batch, seq)
    y_attn = _attn(q, k, v).reshape(batch * seq, ATTN_WIDTH)

    h2d, hn, w_down_bf = _outproj(y_hyena, y_attn, row(hyena_out_gain), row(attn_out_gain), w_out_bf,
                                  x2d, row(post_mix_gain), row(pre_ffn_gain), w_down[l])

    act = _ffn_up(hn, w_up[l], ffn_conv_w[l], row(ffn_conv_b), batch, seq)
    out = _ffn_down(act, w_down_bf, h2d, row(post_ffn_gain))
    return out.reshape(batch, seq, D_MODEL)


def kernel(x, pre_mix_gain, w_in, hyena_conv_w, hyena_conv_b, filt_w1, filt_b1, filt_freq1, filt_w2, filt_b2,
           filt_freq2, filt_w3, hyena_bias, q_norm_gain, w_uq, kv_norm_gain, w_ukv, hyena_out_gain,
           attn_out_gain, w_out, post_mix_gain, pre_ffn_gain, w_up, ffn_conv_w, ffn_conv_b, w_down,
           post_ffn_gain):
    prm = (pre_mix_gain, w_in, hyena_conv_w, hyena_conv_b, filt_w1, filt_b1, filt_freq1, filt_w2, filt_b2,
           filt_freq2, filt_w3, hyena_bias, q_norm_gain, w_uq, kv_norm_gain, w_ukv, hyena_out_gain,
           attn_out_gain, w_out, post_mix_gain, pre_ffn_gain, w_up, ffn_conv_w, ffn_conv_b, w_down,
           post_ffn_gain)
    seq = x.shape[1]
    h = x
    for l in range(w_in.shape[0]):
        h = _layer(h, l, seq, prm)
    return h
```

```python
import functools
import math

import jax
import jax.numpy as jnp
from jax import lax
from jax.experimental import pallas as pl
from jax.experimental.pallas import tpu as pltpu

F32 = jnp.float32
BF16 = jnp.bfloat16

D_MODEL = 2048
HYENA_WIDTH = 1024
CONV_WIDTH = 3
FILTER_EMB_DIM = 33
FILTER_HIDDEN = 64
DECAY_TARGET = 1e-2
FAST_DECAY_PCT = 0.3
SLOW_DECAY_PCT = 1.5
DECAY_SHIFT = 0.05
N_HEADS = 8
QK_NOPE_DIM = 128
QK_ROPE_DIM = 64
V_HEAD_DIM = 128
Q_LORA_RANK = 512
KV_LORA_RANK = 256
ROPE_THETA = 10000.0
ATTN_WIDTH = N_HEADS * V_HEAD_DIM
D_FF = 5632
NORM_EPS = 1e-6

V7X_VMEM_BYTES = 64 * 1024 * 1024
V7X_LANES = 128
V7X_MXU_DIM = 256
V7X_SUBLANES = 8
HALO_BF16 = 2 * V7X_SUBLANES
VMEM_RESERVE_BYTES = 4 * 1024 * 1024
VMEM_TEMP_BYTES = 6 * 1024 * 1024

QK_PAD_DIM = V7X_MXU_DIM
PROJ_WIDTH = 4096
COL_CQ = 3 * HYENA_WIDTH
COL_CKV = COL_CQ + Q_LORA_RANK
COL_KPE = COL_CKV + KV_LORA_RANK


def _params(semantics, block_bytes):
    limit = min(int(block_bytes) + VMEM_TEMP_BYTES, V7X_VMEM_BYTES - VMEM_RESERVE_BYTES)
    return pltpu.CompilerParams(dimension_semantics=semantics, vmem_limit_bytes=limit)


def _rms(x, gain):
    return x * lax.rsqrt(jnp.mean(x * x, axis=-1, keepdims=True) + NORM_EPS) * gain


def _bdot(a, b):
    return jnp.dot(a, b, preferred_element_type=F32)


def _dot3(a, b):
    a_hi, b_hi = a.astype(BF16), b.astype(BF16)
    a_lo, b_lo = (a - a_hi.astype(F32)).astype(BF16), (b - b_hi.astype(F32)).astype(BF16)
    return _bdot(a_hi, b_hi) + (_bdot(a_hi, b_lo) + _bdot(a_lo, b_hi))


def _chunk_bounds(r0, rows, seq):
    return max(r0 - HALO_BF16, 0), min(r0 + rows + HALO_BF16, seq)


def _seq_conv3_chunk(xe, w, b, r0, rows, seq):
    lo, hi = _chunk_bounds(r0, rows, seq)
    n = hi - lo
    prev = pltpu.roll(xe, 1, 0)
    nxt = pltpu.roll(xe, n - 1, 0)
    if lo == 0 or hi == seq:
        row = lax.broadcasted_iota(jnp.int32, xe.shape, 0)
        if lo == 0:
            prev = jnp.where(row == 0, 0.0, prev)
        if hi == seq:
            nxt = jnp.where(row == n - 1, 0.0, nxt)
    return (prev * w[0:1] + xe * w[1:2] + nxt * w[2:3] + b)[r0 - lo:r0 - lo + rows]


def _inproj_body(x_ref, g_ref, wm_ref, wt_ref, o_ref, *, n_main, rows):
    j = pl.program_id(1)

    def project(w_ref):
        w = w_ref[...]
        for r0 in range(0, x_ref.shape[0], rows):
            xn = _rms(x_ref[r0:r0 + rows, :], g_ref[...]).astype(BF16)
            o_ref[r0:r0 + rows, :] = lax.dot_general(xn, w, (((1,), (1,)), ((), ())),
                                                     preferred_element_type=F32).astype(o_ref.dtype)

    @pl.when(j < n_main)
    def _():
        project(wm_ref)

    @pl.when(j == n_main)
    def _():
        project(wt_ref)


def _inproj(x2d, gain, w_main, w_tail):
    m, d = x2d.shape
    tm, tn = 1024, w_tail.shape[0]
    n_main = w_main.shape[0] // tn
    vmem = 2 * tm * d * 4 + tm * d * 2 + 4 * d * tn * 2 + 2 * tm * tn * 2 + tm * tn * 4
    return pl.pallas_call(
        functools.partial(_inproj_body, n_main=n_main, rows=256),
        grid=(m // tm, n_main + 1),
        in_specs=[pl.BlockSpec((tm, d), lambda i, j: (i, 0)),
                  pl.BlockSpec((1, d), lambda i, j: (0, 0)),
                  pl.BlockSpec((tn, d), lambda i, j: (jnp.minimum(j, n_main - 1), 0)),
                  pl.BlockSpec((tn, d), lambda i, j: (0, 0))],
        out_specs=pl.BlockSpec((tm, tn), lambda i, j: (i, j)),
        out_shape=jax.ShapeDtypeStruct((m, (n_main + 1) * tn), BF16),
        compiler_params=_params(("parallel", "arbitrary"), vmem),
        name="inproj",
    )(x2d, gain, w_main, w_tail)


def _dft_mats(p):
    r = 32
    assert p == r * r
    idx = jnp.arange(p, dtype=jnp.int32)
    sub = jnp.arange(r, dtype=jnp.int32)
    ang_hi = ((idx[:, None] * (r * sub)[None, :]) % (2 * p)).astype(F32) * (math.pi / p)
    ang_lo = ((idx[:, None] * sub[None, :]) % (2 * p)).astype(F32) * (math.pi / p)
    ch, sh = jnp.cos(ang_hi)[:, :, None], jnp.sin(ang_hi)[:, :, None]
    cl, sl = jnp.cos(ang_lo)[:, None, :], jnp.sin(ang_lo)[:, None, :]
    c = (ch * cl - sh * sl).reshape(p, p)
    s = -(sh * cl + ch * sl).reshape(p, p)
    alt = jnp.where(idx % 2 == 0, 1.0, -1.0).astype(F32)
    s_fwd = jnp.where(idx[:, None] == 0, alt[None, :], s)
    s_inv = jnp.where(idx[None, :] == 0, alt[:, None], s)
    return c.astype(BF16), s_fwd.astype(BF16), s_inv.astype(BF16)


def _filter_body(z_ref, t_ref, w1_ref, b1_ref, f1_ref, w2_ref, b2_ref, f2_ref, w3f_ref, w3b_ref,
                 dl_ref, c_ref, s_ref, o_ref, hid_ref, *, p):
    @pl.when(pl.program_id(0) == 0)
    def _():
        h1 = jnp.sin(f1_ref[...] * (_dot3(z_ref[...], w1_ref[...]) + b1_ref[...]))
        hid_ref[...] = jnp.sin(f2_ref[...] * (_dot3(h1, w2_ref[...]) + b2_ref[...]))

    h = hid_ref[...]
    win = jnp.exp(-t_ref[...] * dl_ref[...]) + DECAY_SHIFT
    hf = _dot3(h, w3f_ref[...]) * win
    hb = _dot3(h, w3b_ref[...]) * win
    lrow = lax.broadcasted_iota(jnp.int32, hb.shape, 0)
    hb = jnp.where(lrow == 0, 0.0, hb)

    cmat = c_ref[...]
    smat = s_ref[...]

    def fwd(x):
        xb = x.astype(BF16)
        return _bdot(cmat, xb), _bdot(smat, xb)

    row = lax.broadcasted_iota(jnp.int32, (p, hf.shape[1]), 0)
    row0 = row == 0
    sigma = jnp.where(row % 2 == 1, -1.0, 1.0)

    def conj(a):
        return a[0], jnp.where(row0, a[1], -a[1])

    af0, af1 = fwd(hf[:p]), fwd(hf[p:])
    ab0, ab1 = fwd(hb[:p]), fwd(hb[p:])
    cb0 = conj(ab0)
    k0 = (af0[0] + cb0[0], af0[1] + cb0[1])
    k1 = (af1[0] + sigma * af0[0], af1[1] + sigma * af0[1])
    km1 = conj((ab1[0] + sigma * ab0[0], ab1[1] + sigma * ab0[1]))
    scale = jnp.where(row0, 0.5 / p, 1.0 / p)
    for i, a in enumerate((k0, k1, km1)):
        o_ref[2 * i] = a[0] * scale
        o_ref[2 * i + 1] = a[1] * scale


def _filter_spectra(z, t, w1, b1, f1, w2, b2, f2, w3, deltas, cmat, smat, p):
    l = z.shape[0]
    c = HYENA_WIDTH
    tc = 256
    nb = c // tc
    full = lambda a: pl.BlockSpec(a.shape, lambda j: (0,) * a.ndim)
    return pl.pallas_call(
        functools.partial(_filter_body, p=p),
        grid=(nb,),
        in_specs=[full(z), full(t), full(w1), full(b1), full(f1), full(w2), full(b2), full(f2),
                  pl.BlockSpec((FILTER_HIDDEN, tc), lambda j: (0, j)),
                  pl.BlockSpec((FILTER_HIDDEN, tc), lambda j: (0, j + nb)),
                  pl.BlockSpec((1, tc), lambda j: (0, j)),
                  full(cmat), full(smat)],
        out_specs=pl.BlockSpec((6, p, tc), lambda j: (0, 0, j)),
        out_shape=jax.ShapeDtypeStruct((6, p, c), F32),
        scratch_shapes=[pltpu.VMEM((l, FILTER_HIDDEN), F32)],
        compiler_params=_params(("arbitrary",), 2 * 6 * p * tc * 4 + 2 * 2 * p * p * 2 + 10 * l * tc * 4),
        name="filt",
    )(z, t, w1, b1, f1, w2, b2, f2, w3, w3, deltas, cmat, smat)


def _hyena_body(x0_ref, x1_ref, v_ref, cw_ref, cb_ref, ks_ref, hb_ref, c_ref, sf_ref, si_ref, o_ref, *, p):
    cmat = c_ref[...]
    sfwd = sf_ref[...]
    sinv = si_ref[...]
    seq = 2 * p
    row0 = lax.broadcasted_iota(jnp.int32, (p, o_ref.shape[1]), 0) == 0

    def cmul(i, uu):
        kre, kim = ks_ref[2 * i], ks_ref[2 * i + 1]
        ii = kim * uu[1]
        return (kre * uu[0] - jnp.where(row0, 0.0, ii),
                jnp.where(row0, ii, kre * uu[1] + kim * uu[0]))

    def inv(a, b):
        return _bdot(cmat, (a[0] + b[0]).astype(BF16)) + _bdot(sinv, (a[1] + b[1]).astype(BF16))

    def conv(x_ref, i, r0):
        lo, hi = _chunk_bounds(r0, p, seq)
        return _seq_conv3_chunk(x_ref[lo:hi, :].astype(F32), cw_ref[i], cb_ref[i], r0, p, seq)

    us, spec = [], []
    for j in range(2):
        u = conv(x1_ref, 1, j * p) * conv(v_ref, 2, j * p)
        ub = u.astype(BF16)
        us.append(u)
        spec.append((_bdot(cmat, ub), _bdot(sfwd, ub)))
    for j, (ia, ib) in enumerate(((0, 2), (1, 0))):
        y = inv(cmul(ia, spec[0]), cmul(ib, spec[1]))
        x0 = conv(x0_ref, 0, j * p)
        o_ref[j * p:(j + 1) * p, :] = (x0 * (y + us[j] * hb_ref[...])).astype(o_ref.dtype)


def _hyena(proj, conv_w, conv_b, kspec, hbias, cmat, sfwd, sinv, batch, seq, p):
    c = HYENA_WIDTH
    tc = 256
    nb = c // tc
    xspec = lambda off: pl.BlockSpec((seq, tc), lambda j, b: (b, j + off * nb))
    full = lambda a: pl.BlockSpec(a.shape, lambda j, b: (0,) * a.ndim)
    vmem = (6 * seq * tc * 2 + 2 * 6 * p * tc * 4 + 6 * p * p * 2 + 2 * seq * tc * 2
            + 6 * seq * tc * 4)
    return pl.pallas_call(
        functools.partial(_hyena_body, p=p),
        grid=(nb, batch),
        in_specs=[xspec(0), xspec(1), xspec(2),
                  pl.BlockSpec((3, CONV_WIDTH, tc), lambda j, b: (0, 0, j)),
                  pl.BlockSpec((3, 1, tc), lambda j, b: (0, 0, j)),
                  pl.BlockSpec((6, p, tc), lambda j, b: (0, 0, j)),
                  pl.BlockSpec((1, tc), lambda j, b: (0, j)),
                  full(cmat), full(sfwd), full(sinv)],
        out_specs=pl.BlockSpec((seq, tc), lambda j, b: (b, j)),
        out_shape=jax.ShapeDtypeStruct((batch * seq, c), BF16),
        compiler_params=_params(("parallel", "parallel"), vmem),
        name="hyena",
    )(proj, proj, proj, conv_w, conv_b, kspec, hbias, cmat, sfwd, sinv)


def _rotate_half_lanes(x):
    half = QK_ROPE_DIM // 2
    lane = lax.broadcasted_iota(jnp.int32, x.shape, 1)
    return jnp.where(lane < half, -pltpu.roll(x, V7X_LANES - half, 1), pltpu.roll(x, half, 1))


def _qkv_body(cq_ref, ckv_ref, kpe_ref, cos_ref, sin_ref, gq_ref, gkv_ref, wa_ref, wkv_ref,
              q_ref, k_ref, v_ref, *, rows):
    scale = (QK_NOPE_DIM + QK_ROPE_DIM) ** -0.5 * math.log2(math.e)
    n = cq_ref.shape[0] // rows

    def project(c):
        rs = slice(c * rows, (c + 1) * rows)
        cqn = _rms(cq_ref[rs, :].astype(F32), gq_ref[...]).astype(BF16)
        ckvn = _rms(ckv_ref[rs, :].astype(F32), gkv_ref[...]).astype(BF16)
        return (_bdot(cqn, wa_ref[...]),
                _bdot(ckvn, wkv_ref[...]))

    nxt = project(0)
    for c in range(n):
        qa, kv = nxt
        if c + 1 < n:
            nxt = project(c + 1)
        rs = slice(c * rows, (c + 1) * rows)
        cos = cos_ref[rs, :]
        sin = sin_ref[rs, :]
        kpe = kpe_ref[rs, :].astype(F32)
        kpe = (kpe * cos + _rotate_half_lanes(kpe) * sin).astype(BF16)
        for h in range(N_HEADS):
            a = h * QK_PAD_DIM
            qpe = qa[:, a + 128:a + 256]
            q_ref[0, h, rs, 0:128] = (qa[:, a:a + 128] * scale).astype(BF16)
            q_ref[0, h, rs, 128:256] = ((qpe * cos + _rotate_half_lanes(qpe) * sin) * scale).astype(BF16)
            k_ref[0, h, rs, 0:128] = kv[:, a:a + 128].astype(BF16)
            k_ref[0, h, rs, 128:256] = kpe
            v_ref[0, h, rs, :] = kv[:, a + 128:a + 256].astype(BF16)


def _qkv(proj, cos_t, sin_t, gq, gkv, wa, wkv, batch, seq):
    tm = 512
    ns = seq // tm
    full = lambda a: pl.BlockSpec(a.shape, lambda b, i: (0,) * a.ndim)
    col = lambda width, off: pl.BlockSpec((tm, width), lambda b, i: (b * ns + i, off // width))
    hd = lambda w: pl.BlockSpec((1, N_HEADS, tm, w), lambda b, i: (b, 0, i, 0))
    vmem = (2 * tm * (Q_LORA_RANK + KV_LORA_RANK + V7X_LANES) * 2 + 4 * tm * V7X_LANES * 4 + 2 * (wa.size + wkv.size) * 2
            + 2 * N_HEADS * tm * (2 * QK_PAD_DIM + V_HEAD_DIM) * 2 + 4 * tm * N_HEADS * QK_PAD_DIM * 4)
    return pl.pallas_call(
        functools.partial(_qkv_body, rows=tm // 2),
        grid=(batch, ns),
        in_specs=[col(Q_LORA_RANK, COL_CQ), col(KV_LORA_RANK, COL_CKV), col(V7X_LANES, COL_KPE),
                  pl.BlockSpec((tm, V7X_LANES), lambda b, i: (i, 0)),
                  pl.BlockSpec((tm, V7X_LANES), lambda b, i: (i, 0)),
                  full(gq), full(gkv), full(wa), full(wkv)],
        out_specs=[hd(QK_PAD_DIM), hd(QK_PAD_DIM), hd(V_HEAD_DIM)],
        out_shape=[jax.ShapeDtypeStruct((batch, N_HEADS, seq, QK_PAD_DIM), BF16),
                   jax.ShapeDtypeStruct((batch, N_HEADS, seq, QK_PAD_DIM), BF16),
                   jax.ShapeDtypeStruct((batch, N_HEADS, seq, V_HEAD_DIM), BF16)],
        compiler_params=_params(("parallel", "parallel"), vmem),
        name="qkv",
    )(proj, proj, proj, cos_t, sin_t, gq, gkv, wa, wkv)


def _attn_body(q_ref, k_ref, v_ref, o_ref, *, tq):
    heads, seq = q_ref.shape[1], q_ref.shape[2]
    work = [(h, c) for h in range(heads) for c in range(seq // tq)]

    def scores(h, c):
        return lax.dot_general(q_ref[0, h, c * tq:(c + 1) * tq, :], k_ref[0, h], (((1,), (1,)), ((), ())),
                               preferred_element_type=F32)

    s_next = scores(*work[0])
    for i, (h, c) in enumerate(work):
        s = s_next
        if i + 1 < len(work):
            s_next = scores(*work[i + 1])
        m = jnp.max(s, axis=-1, keepdims=True)
        e = jnp.exp2(s - m)
        l = jnp.sum(e, axis=-1, keepdims=True)
        o = _bdot(e.astype(BF16), v_ref[0, h])
        o_ref[0, c * tq:(c + 1) * tq, h * V_HEAD_DIM:(h + 1) * V_HEAD_DIM] = (o / l).astype(o_ref.dtype)


def _attn(q, k, v):
    batch, heads, seq, _ = q.shape
    tq = 256
    hps = 2
    hspec = lambda w: pl.BlockSpec((1, hps, seq, w), lambda b, h: (b, h, 0, 0))
    vmem = 2 * hps * seq * (2 * QK_PAD_DIM + 2 * V_HEAD_DIM) * 2 + 12 * tq * seq * 4
    return pl.pallas_call(
        functools.partial(_attn_body, tq=tq),
        grid=(batch, heads // hps),
        in_specs=[hspec(QK_PAD_DIM), hspec(QK_PAD_DIM), hspec(V_HEAD_DIM)],
        out_specs=pl.BlockSpec((1, seq, hps * V_HEAD_DIM), lambda b, h: (b, 0, h)),
        out_shape=jax.ShapeDtypeStruct((batch, seq, heads * V_HEAD_DIM), BF16),
        compiler_params=_params(("parallel", "parallel"), vmem),
        name="attn",
    )(q, k, v)


def _outproj_body(yh_ref, ya_ref, gh_ref, ga_ref, w_ref, x_ref, gpm_ref, gpf_ref, wd_ref, h_ref, hn_ref, wdb_ref,
                  *, rows):
    wdb_ref[...] = wd_ref[...].astype(BF16)
    n = x_ref.shape[0] // rows

    def mix(c):
        rs = slice(c * rows, (c + 1) * rows)
        a = _rms(yh_ref[rs, :].astype(F32), gh_ref[...]).astype(BF16)
        b = _rms(ya_ref[rs, :].astype(F32), ga_ref[...]).astype(BF16)
        return _bdot(a, w_ref[0:HYENA_WIDTH, :]) + _bdot(b, w_ref[HYENA_WIDTH:, :])

    nxt = mix(0)
    for c in range(n):
        mixed = nxt
        if c + 1 < n:
            nxt = mix(c + 1)
        rs = slice(c * rows, (c + 1) * rows)
        h = x_ref[rs, :] + _rms(mixed, gpm_ref[...])
        h_ref[rs, :] = h
        hn_ref[rs, :] = _rms(h, gpf_ref[...]).astype(BF16)


def _outproj(yh, ya, gh, ga, w_out, x2d, gpm, gpf, w_down):
    m, d = x2d.shape
    tm = 512
    full = lambda a: pl.BlockSpec(a.shape, lambda i: (0,) * a.ndim)
    rows = lambda w: pl.BlockSpec((tm, w), lambda i: (i, 0))
    wd_rows = pl.BlockSpec((w_down.shape[0] // (m // tm), d), lambda i: (i, 0))
    vmem = 2 * 2 * tm * HYENA_WIDTH * 2 + 2 * d * d * 2 + 2 * tm * d * (4 + 4 + 2) + 4 * tm * d * 4
    return pl.pallas_call(
        functools.partial(_outproj_body, rows=tm // 2),
        grid=(m // tm,),
        in_specs=[rows(HYENA_WIDTH), rows(ATTN_WIDTH), full(gh), full(ga), full(w_out), rows(d),
                  full(gpm), full(gpf), wd_rows],
        out_specs=[rows(d), rows(d), wd_rows],
        out_shape=[jax.ShapeDtypeStruct((m, d), F32), jax.ShapeDtypeStruct((m, d), BF16),
                   jax.ShapeDtypeStruct(w_down.shape, BF16)],
        compiler_params=_params(("parallel",), vmem),
        name="outproj",
    )(yh, ya, gh, ga, w_out, x2d, gpm, gpf, w_down)


def _ffn_up_body(hn_ref, wg_ref, wu_ref, cw_ref, cb_ref, o_ref, wgb_ref, wub_ref, *, rows):
    @pl.when(pl.program_id(1) == 0)
    def _():
        wgb_ref[...] = wg_ref[...].astype(BF16)
        wub_ref[...] = wu_ref[...].astype(BF16)

    seq = hn_ref.shape[0]
    wg, wu = wgb_ref[...], wub_ref[...]
    for r0 in range(0, seq, rows):
        lo, hi = _chunk_bounds(r0, rows, seq)
        g = _seq_conv3_chunk(_bdot(hn_ref[lo:hi, :], wg), cw_ref[...], cb_ref[...], r0, rows, seq)
        u = _bdot(hn_ref[r0:r0 + rows, :], wu)
        gelu = 0.5 * g * (1.0 + jnp.tanh(math.sqrt(2.0 / math.pi) * (g + 0.044715 * (g * g * g))))
        o_ref[r0:r0 + rows, :] = (gelu * u).astype(o_ref.dtype)


def _ffn_up(hn, w_up, conv_w, conv_b, batch, seq):
    d = hn.shape[1]
    tn = 512
    nb = D_FF // tn
    wbf = pltpu.VMEM((d, tn), BF16)
    vmem = 2 * seq * d * 2 + 2 * 2 * d * tn * 4 + 2 * d * tn * 2 + 2 * seq * tn * 2 + 8 * seq * tn * 4
    return pl.pallas_call(
        functools.partial(_ffn_up_body, rows=1024),
        grid=(nb, batch),
        in_specs=[pl.BlockSpec((seq, d), lambda j, b: (b, 0)),
                  pl.BlockSpec((d, tn), lambda j, b: (0, j)),
                  pl.BlockSpec((d, tn), lambda j, b: (0, j + nb)),
                  pl.BlockSpec((CONV_WIDTH, tn), lambda j, b: (0, j)),
                  pl.BlockSpec((1, tn), lambda j, b: (0, j))],
        out_specs=pl.BlockSpec((seq, tn), lambda j, b: (b, j)),
        out_shape=jax.ShapeDtypeStruct((batch * seq, D_FF), BF16),
        scratch_shapes=[wbf, wbf],
        compiler_params=_params(("parallel", "arbitrary"), vmem),
        name="ffn_up",
    )(hn, w_up, w_up, conv_w, conv_b)


def _ffn_down_body(a_ref, w_ref, h_ref, g_ref, o_ref, *, rows):
    for r0 in range(0, a_ref.shape[0], rows):
        rs = slice(r0, r0 + rows)
        o_ref[rs, :] = h_ref[rs, :] + _rms(_bdot(a_ref[rs, :], w_ref[...]), g_ref[...])


def _ffn_down(act, w_down, h, gain):
    m, d = h.shape
    k = act.shape[1]
    tm = 512
    vmem = k * d * 2 + 2 * tm * k * 2 + 2 * 2 * tm * d * 4 + tm * d * 4
    return pl.pallas_call(
        functools.partial(_ffn_down_body, rows=tm // 2),
        grid=(m // tm,),
        in_specs=[pl.BlockSpec((tm, k), lambda i: (i, 0)),
                  pl.BlockSpec((k, d), lambda i: (0, 0), pipeline_mode=pl.Buffered(1)),
                  pl.BlockSpec((tm, d), lambda i: (i, 0)),
                  pl.BlockSpec((1, d), lambda i: (0, 0))],
        out_specs=pl.BlockSpec((tm, d), lambda i: (i, 0)),
        out_shape=jax.ShapeDtypeStruct((m, d), F32),
        compiler_params=_params(("arbitrary",), vmem),
        name="ffn_down",
    )(act, w_down, h, gain)


def _position_features(l):
    t = jnp.linspace(0.0, 1.0, l, dtype=F32)[:, None]
    bands = (FILTER_EMB_DIM - 1) // 2
    w = 2.0 * math.pi * jnp.arange(l, dtype=F32) / l
    f = jnp.linspace(1e-4, bands - 1, bands, dtype=F32)
    ang = w[:, None] * f[None, :]
    return t, jnp.concatenate([t, jnp.cos(ang), -jnp.sin(ang)], axis=-1)


def _layer(h, l, seq, prm):
    (pre_mix_gain, w_in, hyena_conv_w, hyena_conv_b, filt_w1, filt_b1, filt_freq1, filt_w2, filt_b2,
     filt_freq2, filt_w3, hyena_bias, q_norm_gain, w_uq, kv_norm_gain, w_ukv, hyena_out_gain, attn_out_gain,
     w_out, post_mix_gain, pre_ffn_gain, w_up, ffn_conv_w, ffn_conv_b, w_down, post_ffn_gain) = prm
    batch = h.shape[0]
    p = seq // 2
    x2d = h.reshape(batch * seq, D_MODEL)
    row = lambda a: a[l][None, :].astype(F32)

    wt = jnp.swapaxes(w_in[l], 0, 1)
    w_tail = jnp.pad(wt[COL_CQ:].astype(BF16), ((0, PROJ_WIDTH - wt.shape[0]), (0, 0)))
    proj = _inproj(x2d, row(pre_mix_gain), wt[:COL_CQ].astype(BF16), w_tail)

    cmat, sfwd, sinv = _dft_mats(p)
    t, z = _position_features(seq)
    zp = jnp.pad(z, ((0, 0), (0, V7X_LANES - FILTER_EMB_DIM)))
    w1p = jnp.pad(filt_w1[l], ((0, V7X_LANES - FILTER_EMB_DIM), (0, 0)))
    max_decay = math.log(DECAY_TARGET) / FAST_DECAY_PCT
    min_decay = math.log(DECAY_TARGET) / SLOW_DECAY_PCT
    deltas = jnp.abs(jnp.linspace(min_decay, max_decay, HYENA_WIDTH, dtype=F32))[None, :]
    kspec = _filter_spectra(zp, t, w1p, row(filt_b1), row(filt_freq1), filt_w2[l], row(filt_b2),
                            row(filt_freq2), filt_w3[l], deltas, cmat, sfwd, p)
    conv_w = hyena_conv_w[l].reshape(CONV_WIDTH, 3, HYENA_WIDTH).transpose(1, 0, 2)
    conv_b = hyena_conv_b[l].reshape(3, 1, HYENA_WIDTH)
    y_hyena = _hyena(proj, conv_w, conv_b, kspec, row(hyena_bias), cmat, sfwd, sinv, batch, seq, p)

    pos = jnp.arange(seq, dtype=F32)
    inv_freq = 1.0 / (ROPE_THETA ** (jnp.arange(0, QK_ROPE_DIM, 2, dtype=F32) / QK_ROPE_DIM))
    ang = pos[:, None] * inv_freq[None, :]
    ang = jnp.concatenate([ang, ang], axis=-1)
    lpad = ((0, 0), (0, V7X_LANES - QK_ROPE_DIM))
    cos_t = jnp.pad(jnp.cos(ang), lpad)
    sin_t = jnp.pad(jnp.sin(ang), lpad)
    dqk = QK_NOPE_DIM + QK_ROPE_DIM
    wq = w_uq[l].reshape(Q_LORA_RANK, N_HEADS, dqk)
    wa = jnp.pad(wq, ((0, 0), (0, 0), (0, QK_PAD_DIM - dqk))).reshape(Q_LORA_RANK, N_HEADS * QK_PAD_DIM)
    q, k, v = _qkv(proj, cos_t, sin_t, row(q_norm_gain), row(kv_norm_gain), wa.astype(BF16),
                   w_ukv[l].astype(BF16), batch, seq)
    y_attn = _attn(q, k, v).reshape(batch * seq, ATTN_WIDTH)

    h2d, hn, w_down_bf = _outproj(y_hyena, y_attn, row(hyena_out_gain), row(attn_out_gain), w_out[l].astype(BF16),
                                  x2d, row(post_mix_gain), row(pre_ffn_gain), w_down[l])

    act = _ffn_up(hn, w_up[l], ffn_conv_w[l], row(ffn_conv_b), batch, seq)
    out = _ffn_down(act, w_down_bf, h2d, row(post_ffn_gain))
    return out.reshape(batch, seq, D_MODEL)


def kernel(x, pre_mix_gain, w_in, hyena_conv_w, hyena_conv_b, filt_w1, filt_b1, filt_freq1, filt_w2, filt_b2,
           filt_freq2, filt_w3, hyena_bias, q_norm_gain, w_uq, kv_norm_gain, w_ukv, hyena_out_gain,
           attn_out_gain, w_out, post_mix_gain, pre_ffn_gain, w_up, ffn_conv_w, ffn_conv_b, w_down,
           post_ffn_gain):
    prm = (pre_mix_gain, w_in, hyena_conv_w, hyena_conv_b, filt_w1, filt_b1, filt_freq1, filt_w2, filt_b2,
           filt_freq2, filt_w3, hyena_bias, q_norm_gain, w_uq, kv_norm_gain, w_ukv, hyena_out_gain,
           attn_out_gain, w_out, post_mix_gain, pre_ffn_gain, w_up, ffn_conv_w, ffn_conv_b, w_down,
           post_ffn_gain)
    seq = x.shape[1]
    h = x
    for l in range(w_in.shape[0]):
        h = _layer(h, l, seq, prm)
    return h
```

```python
import functools
import math

import jax
import jax.numpy as jnp
from jax import lax
from jax.experimental import pallas as pl
from jax.experimental.pallas import tpu as pltpu

F32 = jnp.float32
BF16 = jnp.bfloat16

D_MODEL = 2048
HYENA_WIDTH = 1024
CONV_WIDTH = 3
FILTER_EMB_DIM = 33
FILTER_HIDDEN = 64
DECAY_TARGET = 1e-2
FAST_DECAY_PCT = 0.3
SLOW_DECAY_PCT = 1.5
DECAY_SHIFT = 0.05
N_HEADS = 8
QK_NOPE_DIM = 128
QK_ROPE_DIM = 64
V_HEAD_DIM = 128
Q_LORA_RANK = 512
KV_LORA_RANK = 256
ROPE_THETA = 10000.0
ATTN_WIDTH = N_HEADS * V_HEAD_DIM
D_FF = 5632
NORM_EPS = 1e-6

V7X_VMEM_BYTES = 64 * 1024 * 1024
V7X_LANES = 128
V7X_MXU_DIM = 256
V7X_SUBLANES = 8
HALO_BF16 = 2 * V7X_SUBLANES
VMEM_RESERVE_BYTES = 4 * 1024 * 1024
VMEM_TEMP_BYTES = 6 * 1024 * 1024

QK_PAD_DIM = V7X_MXU_DIM
PROJ_WIDTH = 4096
COL_CQ = 3 * HYENA_WIDTH
COL_CKV = COL_CQ + Q_LORA_RANK
COL_KPE = COL_CKV + KV_LORA_RANK


def _params(semantics, block_bytes):
    limit = min(int(block_bytes) + VMEM_TEMP_BYTES, V7X_VMEM_BYTES - VMEM_RESERVE_BYTES)
    return pltpu.CompilerParams(dimension_semantics=semantics, vmem_limit_bytes=limit)


def _rms(x, gain):
    return x * lax.rsqrt(jnp.mean(x * x, axis=-1, keepdims=True) + NORM_EPS) * gain


def _bdot(a, b):
    return jnp.dot(a, b, preferred_element_type=F32)


def _dot3(a, b):
    a_hi, b_hi = a.astype(BF16), b.astype(BF16)
    a_lo, b_lo = (a - a_hi.astype(F32)).astype(BF16), (b - b_hi.astype(F32)).astype(BF16)
    return _bdot(a_hi, b_hi) + (_bdot(a_hi, b_lo) + _bdot(a_lo, b_hi))


def _chunk_bounds(r0, rows, seq):
    return max(r0 - HALO_BF16, 0), min(r0 + rows + HALO_BF16, seq)


def _seq_conv3_chunk(xe, w, b, r0, rows, seq):
    lo, hi = _chunk_bounds(r0, rows, seq)
    n = hi - lo
    prev = pltpu.roll(xe, 1, 0)
    nxt = pltpu.roll(xe, n - 1, 0)
    if lo == 0 or hi == seq:
        row = lax.broadcasted_iota(jnp.int32, xe.shape, 0)
        if lo == 0:
            prev = jnp.where(row == 0, 0.0, prev)
        if hi == seq:
            nxt = jnp.where(row == n - 1, 0.0, nxt)
    return (prev * w[0:1] + xe * w[1:2] + nxt * w[2:3] + b)[r0 - lo:r0 - lo + rows]


def _inproj_body(x_ref, g_ref, wm_ref, wt_ref, o_ref, *, n_main, rows):
    j = pl.program_id(1)

    def project(w_ref):
        w = w_ref[...]
        for r0 in range(0, x_ref.shape[0], rows):
            xn = _rms(x_ref[r0:r0 + rows, :], g_ref[...]).astype(BF16)
            o_ref[r0:r0 + rows, :] = lax.dot_general(xn, w, (((1,), (1,)), ((), ())),
                                                     preferred_element_type=F32).astype(o_ref.dtype)

    @pl.when(j < n_main)
    def _():
        project(wm_ref)

    @pl.when(j == n_main)
    def _():
        project(wt_ref)


def _inproj(x2d, gain, w_main, w_tail):
    m, d = x2d.shape
    tm, tn = 1024, w_tail.shape[0]
    n_main = w_main.shape[0] // tn
    vmem = 2 * tm * d * 4 + tm * d * 2 + 4 * d * tn * 2 + 2 * tm * tn * 2 + tm * tn * 4
    return pl.pallas_call(
        functools.partial(_inproj_body, n_main=n_main, rows=256),
        grid=(m // tm, n_main + 1),
        in_specs=[pl.BlockSpec((tm, d), lambda i, j: (i, 0)),
                  pl.BlockSpec((1, d), lambda i, j: (0, 0)),
                  pl.BlockSpec((tn, d), lambda i, j: (jnp.minimum(j, n_main - 1), 0)),
                  pl.BlockSpec((tn, d), lambda i, j: (0, 0))],
        out_specs=pl.BlockSpec((tm, tn), lambda i, j: (i, j)),
        out_shape=jax.ShapeDtypeStruct((m, (n_main + 1) * tn), BF16),
        compiler_params=_params(("parallel", "arbitrary"), vmem),
        name="inproj",
    )(x2d, gain, w_main, w_tail)


def _dft_mats(p):
    r = 32
    assert p == r * r
    idx = jnp.arange(p, dtype=jnp.int32)
    sub = jnp.arange(r, dtype=jnp.int32)
    ang_hi = ((idx[:, None] * (r * sub)[None, :]) % (2 * p)).astype(F32) * (math.pi / p)
    ang_lo = ((idx[:, None] * sub[None, :]) % (2 * p)).astype(F32) * (math.pi / p)
    ch, sh = jnp.cos(ang_hi)[:, :, None], jnp.sin(ang_hi)[:, :, None]
    cl, sl = jnp.cos(ang_lo)[:, None, :], jnp.sin(ang_lo)[:, None, :]
    c = (ch * cl - sh * sl).reshape(p, p)
    s = -(sh * cl + ch * sl).reshape(p, p)
    alt = jnp.where(idx % 2 == 0, 1.0, -1.0).astype(F32)
    s_fwd = jnp.where(idx[:, None] == 0, alt[None, :], s)
    s_inv = jnp.where(idx[None, :] == 0, alt[:, None], s)
    return c.astype(BF16), s_fwd.astype(BF16), s_inv.astype(BF16)


def _filter_body(z_ref, t_ref, w1_ref, b1_ref, f1_ref, w2_ref, b2_ref, f2_ref, w3f_ref, w3b_ref,
                 dl_ref, c_ref, s_ref, o_ref, hid_ref, *, p):
    @pl.when(pl.program_id(0) == 0)
    def _():
        h1 = jnp.sin(f1_ref[...] * (_dot3(z_ref[...], w1_ref[...]) + b1_ref[...]))
        hid_ref[...] = jnp.sin(f2_ref[...] * (_dot3(h1, w2_ref[...]) + b2_ref[...]))

    h = hid_ref[...]
    win = jnp.exp(-t_ref[...] * dl_ref[...]) + DECAY_SHIFT
    hf = _dot3(h, w3f_ref[...]) * win
    hb = _dot3(h, w3b_ref[...]) * win
    lrow = lax.broadcasted_iota(jnp.int32, hb.shape, 0)
    hb = jnp.where(lrow == 0, 0.0, hb)

    cmat = c_ref[...]
    smat = s_ref[...]

    def fwd(x):
        xb = x.astype(BF16)
        return _bdot(cmat, xb), _bdot(smat, xb)

    row = lax.broadcasted_iota(jnp.int32, (p, hf.shape[1]), 0)
    row0 = row == 0
    sigma = jnp.where(row % 2 == 1, -1.0, 1.0)

    def conj(a):
        return a[0], jnp.where(row0, a[1], -a[1])

    af0, af1 = fwd(hf[:p]), fwd(hf[p:])
    ab0, ab1 = fwd(hb[:p]), fwd(hb[p:])
    cb0 = conj(ab0)
    k0 = (af0[0] + cb0[0], af0[1] + cb0[1])
    k1 = (af1[0] + sigma * af0[0], af1[1] + sigma * af0[1])
    km1 = conj((ab1[0] + sigma * ab0[0], ab1[1] + sigma * ab0[1]))
    scale = jnp.where(row0, 0.5 / p, 1.0 / p)
    for i, a in enumerate((k0, k1, km1)):
        o_ref[2 * i] = a[0] * scale
        o_ref[2 * i + 1] = a[1] * scale


def _filter_spectra(z, t, w1, b1, f1, w2, b2, f2, w3, deltas, cmat, smat, p):
    l = z.shape[0]
    c = HYENA_WIDTH
    tc = 256
    nb = c // tc
    full = lambda a: pl.BlockSpec(a.shape, lambda j: (0,) * a.ndim)
    return pl.pallas_call(
        functools.partial(_filter_body, p=p),
        grid=(nb,),
        in_specs=[full(z), full(t), full(w1), full(b1), full(f1), full(w2), full(b2), full(f2),
                  pl.BlockSpec((FILTER_HIDDEN, tc), lambda j: (0, j)),
                  pl.BlockSpec((FILTER_HIDDEN, tc), lambda j: (0, j + nb)),
                  pl.BlockSpec((1, tc), lambda j: (0, j)),
                  full(cmat), full(smat)],
        out_specs=pl.BlockSpec((6, p, tc), lambda j: (0, 0, j)),
        out_shape=jax.ShapeDtypeStruct((6, p, c), F32),
        scratch_shapes=[pltpu.VMEM((l, FILTER_HIDDEN), F32)],
        compiler_params=_params(("arbitrary",), 2 * 6 * p * tc * 4 + 2 * 2 * p * p * 2 + 10 * l * tc * 4),
        name="filt",
    )(z, t, w1, b1, f1, w2, b2, f2, w3, w3, deltas, cmat, smat)


def _hyena_body(x0_ref, x1_ref, v_ref, cw_ref, cb_ref, ks_ref, hb_ref, c_ref, sf_ref, si_ref, o_ref, *, p):
    cmat = c_ref[...]
    sfwd = sf_ref[...]
    sinv = si_ref[...]
    seq = 2 * p
    row0 = lax.broadcasted_iota(jnp.int32, (p, o_ref.shape[1]), 0) == 0

    def cmul(i, uu):
        kre, kim = ks_ref[2 * i], ks_ref[2 * i + 1]
        ii = kim * uu[1]
        return (kre * uu[0] - jnp.where(row0, 0.0, ii),
                jnp.where(row0, ii, kre * uu[1] + kim * uu[0]))

    def inv(a, b):
        return _bdot(cmat, (a[0] + b[0]).astype(BF16)) + _bdot(sinv, (a[1] + b[1]).astype(BF16))

    def conv(x_ref, i, r0):
        lo, hi = _chunk_bounds(r0, p, seq)
        return _seq_conv3_chunk(x_ref[lo:hi, :].astype(F32), cw_ref[i], cb_ref[i], r0, p, seq)

    us, spec = [], []
    for j in range(2):
        u = conv(x1_ref, 1, j * p) * conv(v_ref, 2, j * p)
        ub = u.astype(BF16)
        us.append(u)
        spec.append((_bdot(cmat, ub), _bdot(sfwd, ub)))
    for j, (ia, ib) in enumerate(((0, 2), (1, 0))):
        y = inv(cmul(ia, spec[0]), cmul(ib, spec[1]))
        x0 = conv(x0_ref, 0, j * p)
        o_ref[j * p:(j + 1) * p, :] = (x0 * (y + us[j] * hb_ref[...])).astype(o_ref.dtype)


def _hyena(proj, conv_w, conv_b, kspec, hbias, cmat, sfwd, sinv, batch, seq, p):
    c = HYENA_WIDTH
    tc = 256
    nb = c // tc
    xspec = lambda off: pl.BlockSpec((seq, tc), lambda j, b: (b, j + off * nb))
    full = lambda a: pl.BlockSpec(a.shape, lambda j, b: (0,) * a.ndim)
    vmem = (6 * seq * tc * 2 + 2 * 6 * p * tc * 4 + 6 * p * p * 2 + 2 * seq * tc * 2
            + 6 * seq * tc * 4)
    return pl.pallas_call(
        functools.partial(_hyena_body, p=p),
        grid=(nb, batch),
        in_specs=[xspec(0), xspec(1), xspec(2),
                  pl.BlockSpec((3, CONV_WIDTH, tc), lambda j, b: (0, 0, j)),
                  pl.BlockSpec((3, 1, tc), lambda j, b: (0, 0, j)),
                  pl.BlockSpec((6, p, tc), lambda j, b: (0, 0, j)),
                  pl.BlockSpec((1, tc), lambda j, b: (0, j)),
                  full(cmat), full(sfwd), full(sinv)],
        out_specs=pl.BlockSpec((seq, tc), lambda j, b: (b, j)),
        out_shape=jax.ShapeDtypeStruct((batch * seq, c), BF16),
        compiler_params=_params(("parallel", "parallel"), vmem),
        name="hyena",
    )(proj, proj, proj, conv_w, conv_b, kspec, hbias, cmat, sfwd, sinv)


def _rotate_half_lanes(x):
    half = QK_ROPE_DIM // 2
    lane = lax.broadcasted_iota(jnp.int32, x.shape, 1)
    return jnp.where(lane < half, -pltpu.roll(x, V7X_LANES - half, 1), pltpu.roll(x, half, 1))


def _qkv_body(cq_ref, ckv_ref, kpe_ref, cos_ref, sin_ref, gq_ref, gkv_ref, wa_ref, wkv_ref,
              q_ref, k_ref, kro_ref, v_ref, *, rows):
    scale = (QK_NOPE_DIM + QK_ROPE_DIM) ** -0.5 * math.log2(math.e)
    n = cq_ref.shape[0] // rows

    def project(c):
        rs = slice(c * rows, (c + 1) * rows)
        cqn = _rms(cq_ref[rs, :].astype(F32), gq_ref[...]).astype(BF16)
        ckvn = _rms(ckv_ref[rs, :].astype(F32), gkv_ref[...]).astype(BF16)
        return (_bdot(cqn, wa_ref[...]),
                _bdot(ckvn, wkv_ref[...]))

    nxt = project(0)
    for c in range(n):
        qa, kv = nxt
        if c + 1 < n:
            nxt = project(c + 1)
        rs = slice(c * rows, (c + 1) * rows)
        cos = cos_ref[rs, :]
        sin = sin_ref[rs, :]
        kpe = kpe_ref[rs, :].astype(F32)
        kro_ref[0, rs, :] = (kpe * cos + _rotate_half_lanes(kpe) * sin).astype(BF16)
        for h in range(N_HEADS):
            a = h * QK_PAD_DIM
            qpe = qa[:, a + 128:a + 256]
            q_ref[0, h, rs, 0:128] = (qa[:, a:a + 128] * scale).astype(BF16)
            q_ref[0, h, rs, 128:256] = ((qpe * cos + _rotate_half_lanes(qpe) * sin) * scale).astype(BF16)
            k_ref[0, h, rs, :] = kv[:, a:a + 128].astype(BF16)
            v_ref[0, h, rs, :] = kv[:, a + 128:a + 256].astype(BF16)


def _qkv(proj, cos_t, sin_t, gq, gkv, wa, wkv, batch, seq):
    tm = 512
    ns = seq // tm
    full = lambda a: pl.BlockSpec(a.shape, lambda b, i: (0,) * a.ndim)
    col = lambda width, off: pl.BlockSpec((tm, width), lambda b, i: (b * ns + i, off // width))
    hd = lambda w: pl.BlockSpec((1, N_HEADS, tm, w), lambda b, i: (b, 0, i, 0))
    vmem = (2 * tm * (Q_LORA_RANK + KV_LORA_RANK + V7X_LANES) * 2 + 4 * tm * V7X_LANES * 4 + 2 * (wa.size + wkv.size) * 2
            + 2 * N_HEADS * tm * (2 * QK_PAD_DIM + V_HEAD_DIM) * 2 + 4 * tm * N_HEADS * QK_PAD_DIM * 4)
    return pl.pallas_call(
        functools.partial(_qkv_body, rows=tm // 2),
        grid=(batch, ns),
        in_specs=[col(Q_LORA_RANK, COL_CQ), col(KV_LORA_RANK, COL_CKV), col(V7X_LANES, COL_KPE),
                  pl.BlockSpec((tm, V7X_LANES), lambda b, i: (i, 0)),
                  pl.BlockSpec((tm, V7X_LANES), lambda b, i: (i, 0)),
                  full(gq), full(gkv), full(wa), full(wkv)],
        out_specs=[hd(QK_PAD_DIM), hd(QK_NOPE_DIM), pl.BlockSpec((1, tm, V7X_LANES), lambda b, i: (b, i, 0)),
                   hd(V_HEAD_DIM)],
        out_shape=[jax.ShapeDtypeStruct((batch, N_HEADS, seq, QK_PAD_DIM), BF16),
                   jax.ShapeDtypeStruct((batch, N_HEADS, seq, QK_NOPE_DIM), BF16),
                   jax.ShapeDtypeStruct((batch, seq, V7X_LANES), BF16),
                   jax.ShapeDtypeStruct((batch, N_HEADS, seq, V_HEAD_DIM), BF16)],
        compiler_params=_params(("parallel", "parallel"), vmem),
        name="qkv",
    )(proj, proj, proj, cos_t, sin_t, gq, gkv, wa, wkv)


def _attn_body(q_ref, k_ref, kro_ref, v_ref, o_ref, *, tq):
    heads, seq = q_ref.shape[1], q_ref.shape[2]
    work = [(h, c) for h in range(heads) for c in range(seq // tq)]
    keys = [jnp.concatenate([k_ref[0, h], kro_ref[0]], axis=1) for h in range(heads)]

    def scores(h, c):
        return lax.dot_general(q_ref[0, h, c * tq:(c + 1) * tq, :], keys[h], (((1,), (1,)), ((), ())),
                               preferred_element_type=F32)

    s_next = scores(*work[0])
    for i, (h, c) in enumerate(work):
        s = s_next
        if i + 1 < len(work):
            s_next = scores(*work[i + 1])
        m = jnp.max(s, axis=-1, keepdims=True)
        e = jnp.exp2(s - m)
        l = jnp.sum(e, axis=-1, keepdims=True)
        o = _bdot(e.astype(BF16), v_ref[0, h])
        o_ref[0, c * tq:(c + 1) * tq, h * V_HEAD_DIM:(h + 1) * V_HEAD_DIM] = (o / l).astype(o_ref.dtype)


def _attn(q, k, kro, v):
    batch, heads, seq, _ = q.shape
    tq = 256
    hps = 2
    hspec = lambda w: pl.BlockSpec((1, hps, seq, w), lambda b, h: (b, h, 0, 0))
    vmem = 2 * hps * seq * (2 * QK_PAD_DIM + 2 * V_HEAD_DIM) * 2 + 12 * tq * seq * 4
    return pl.pallas_call(
        functools.partial(_attn_body, tq=tq),
        grid=(batch, heads // hps),
        in_specs=[hspec(QK_PAD_DIM), hspec(QK_NOPE_DIM), pl.BlockSpec((1, seq, V7X_LANES), lambda b, h: (b, 0, 0)),
                  hspec(V_HEAD_DIM)],
        out_specs=pl.BlockSpec((1, seq, hps * V_HEAD_DIM), lambda b, h: (b, 0, h)),
        out_shape=jax.ShapeDtypeStruct((batch, seq, heads * V_HEAD_DIM), BF16),
        compiler_params=_params(("parallel", "parallel"), vmem),
        name="attn",
    )(q, k, kro, v)


def _outproj_body(yh_ref, ya_ref, gh_ref, ga_ref, w_ref, x_ref, gpm_ref, gpf_ref, wd_ref, h_ref, hn_ref, wdb_ref,
                  *, rows):
    wdb_ref[...] = wd_ref[...].astype(BF16)
    n = x_ref.shape[0] // rows

    def mix(c):
        rs = slice(c * rows, (c + 1) * rows)
        a = _rms(yh_ref[rs, :].astype(F32), gh_ref[...]).astype(BF16)
        b = _rms(ya_ref[rs, :].astype(F32), ga_ref[...]).astype(BF16)
        return _bdot(a, w_ref[0:HYENA_WIDTH, :]) + _bdot(b, w_ref[HYENA_WIDTH:, :])

    nxt = mix(0)
    for c in range(n):
        mixed = nxt
        if c + 1 < n:
            nxt = mix(c + 1)
        rs = slice(c * rows, (c + 1) * rows)
        h = x_ref[rs, :] + _rms(mixed, gpm_ref[...])
        h_ref[rs, :] = h
        hn_ref[rs, :] = _rms(h, gpf_ref[...]).astype(BF16)


def _outproj(yh, ya, gh, ga, w_out, x2d, gpm, gpf, w_down):
    m, d = x2d.shape
    tm = 512
    full = lambda a: pl.BlockSpec(a.shape, lambda i: (0,) * a.ndim)
    rows = lambda w: pl.BlockSpec((tm, w), lambda i: (i, 0))
    wd_rows = pl.BlockSpec((w_down.shape[0] // (m // tm), d), lambda i: (i, 0))
    vmem = 2 * 2 * tm * HYENA_WIDTH * 2 + 2 * d * d * 2 + 2 * tm * d * (4 + 4 + 2) + 4 * tm * d * 4
    return pl.pallas_call(
        functools.partial(_outproj_body, rows=tm // 2),
        grid=(m // tm,),
        in_specs=[rows(HYENA_WIDTH), rows(ATTN_WIDTH), full(gh), full(ga), full(w_out), rows(d),
                  full(gpm), full(gpf), wd_rows],
        out_specs=[rows(d), rows(d), wd_rows],
        out_shape=[jax.ShapeDtypeStruct((m, d), F32), jax.ShapeDtypeStruct((m, d), BF16),
                   jax.ShapeDtypeStruct(w_down.shape, BF16)],
        compiler_params=_params(("parallel",), vmem),
        name="outproj",
    )(yh, ya, gh, ga, w_out, x2d, gpm, gpf, w_down)


def _ffn_up_body(hn_ref, wg_ref, wu_ref, cw_ref, cb_ref, o_ref, wgb_ref, wub_ref, *, rows):
    @pl.when(pl.program_id(1) == 0)
    def _():
        wgb_ref[...] = wg_ref[...].astype(BF16)
        wub_ref[...] = wu_ref[...].astype(BF16)

    seq = hn_ref.shape[0]
    wg, wu = wgb_ref[...], wub_ref[...]
    for r0 in range(0, seq, rows):
        lo, hi = _chunk_bounds(r0, rows, seq)
        g = _seq_conv3_chunk(_bdot(hn_ref[lo:hi, :], wg), cw_ref[...], cb_ref[...], r0, rows, seq)
        u = _bdot(hn_ref[r0:r0 + rows, :], wu)
        gelu = 0.5 * g * (1.0 + jnp.tanh(math.sqrt(2.0 / math.pi) * (g + 0.044715 * (g * g * g))))
        o_ref[r0:r0 + rows, :] = (gelu * u).astype(o_ref.dtype)


def _ffn_up(hn, w_up, conv_w, conv_b, batch, seq):
    d = hn.shape[1]
    tn = 512
    nb = D_FF // tn
    wbf = pltpu.VMEM((d, tn), BF16)
    vmem = 2 * seq * d * 2 + 2 * 2 * d * tn * 4 + 2 * d * tn * 2 + 2 * seq * tn * 2 + 8 * seq * tn * 4
    return pl.pallas_call(
        functools.partial(_ffn_up_body, rows=1024),
        grid=(nb, batch),
        in_specs=[pl.BlockSpec((seq, d), lambda j, b: (b, 0)),
                  pl.BlockSpec((d, tn), lambda j, b: (0, j)),
                  pl.BlockSpec((d, tn), lambda j, b: (0, j + nb)),
                  pl.BlockSpec((CONV_WIDTH, tn), lambda j, b: (0, j)),
                  pl.BlockSpec((1, tn), lambda j, b: (0, j))],
        out_specs=pl.BlockSpec((seq, tn), lambda j, b: (b, j)),
        out_shape=jax.ShapeDtypeStruct((batch * seq, D_FF), BF16),
        scratch_shapes=[wbf, wbf],
        compiler_params=_params(("parallel", "arbitrary"), vmem),
        name="ffn_up",
    )(hn, w_up, w_up, conv_w, conv_b)


def _ffn_down_body(a_ref, w_ref, h_ref, g_ref, o_ref, *, rows):
    for r0 in range(0, a_ref.shape[0], rows):
        rs = slice(r0, r0 + rows)
        o_ref[rs, :] = h_ref[rs, :] + _rms(_bdot(a_ref[rs, :], w_ref[...]), g_ref[...])


def _ffn_down(act, w_down, h, gain):
    m, d = h.shape
    k = act.shape[1]
    tm = 512
    vmem = k * d * 2 + 2 * tm * k * 2 + 2 * 2 * tm * d * 4 + tm * d * 4
    return pl.pallas_call(
        functools.partial(_ffn_down_body, rows=tm // 2),
        grid=(m // tm,),
        in_specs=[pl.BlockSpec((tm, k), lambda i: (i, 0)),
                  pl.BlockSpec((k, d), lambda i: (0, 0), pipeline_mode=pl.Buffered(1)),
                  pl.BlockSpec((tm, d), lambda i: (i, 0)),
                  pl.BlockSpec((1, d), lambda i: (0, 0))],
        out_specs=pl.BlockSpec((tm, d), lambda i: (i, 0)),
        out_shape=jax.ShapeDtypeStruct((m, d), F32),
        compiler_params=_params(("arbitrary",), vmem),
        name="ffn_down",
    )(act, w_down, h, gain)


def _position_features(l):
    t = jnp.linspace(0.0, 1.0, l, dtype=F32)[:, None]
    bands = (FILTER_EMB_DIM - 1) // 2
    w = 2.0 * math.pi * jnp.arange(l, dtype=F32) / l
    f = jnp.linspace(1e-4, bands - 1, bands, dtype=F32)
    ang = w[:, None] * f[None, :]
    return t, jnp.concatenate([t, jnp.cos(ang), -jnp.sin(ang)], axis=-1)


def _layer(h, l, seq, prm):
    (pre_mix_gain, w_in, hyena_conv_w, hyena_conv_b, filt_w1, filt_b1, filt_freq1, filt_w2, filt_b2,
     filt_freq2, filt_w3, hyena_bias, q_norm_gain, w_uq, kv_norm_gain, w_ukv, hyena_out_gain, attn_out_gain,
     w_out, post_mix_gain, pre_ffn_gain, w_up, ffn_conv_w, ffn_conv_b, w_down, post_ffn_gain) = prm
    batch = h.shape[0]
    p = seq // 2
    x2d = h.reshape(batch * seq, D_MODEL)
    row = lambda a: a[l][None, :].astype(F32)

    wt = jnp.swapaxes(w_in[l], 0, 1)
    w_tail = jnp.pad(wt[COL_CQ:].astype(BF16), ((0, PROJ_WIDTH - wt.shape[0]), (0, 0)))
    proj = _inproj(x2d, row(pre_mix_gain), wt[:COL_CQ].astype(BF16), w_tail)

    cmat, sfwd, sinv = _dft_mats(p)
    t, z = _position_features(seq)
    zp = jnp.pad(z, ((0, 0), (0, V7X_LANES - FILTER_EMB_DIM)))
    w1p = jnp.pad(filt_w1[l], ((0, V7X_LANES - FILTER_EMB_DIM), (0, 0)))
    max_decay = math.log(DECAY_TARGET) / FAST_DECAY_PCT
    min_decay = math.log(DECAY_TARGET) / SLOW_DECAY_PCT
    deltas = jnp.abs(jnp.linspace(min_decay, max_decay, HYENA_WIDTH, dtype=F32))[None, :]
    kspec = _filter_spectra(zp, t, w1p, row(filt_b1), row(filt_freq1), filt_w2[l], row(filt_b2),
                            row(filt_freq2), filt_w3[l], deltas, cmat, sfwd, p)
    conv_w = hyena_conv_w[l].reshape(CONV_WIDTH, 3, HYENA_WIDTH).transpose(1, 0, 2)
    conv_b = hyena_conv_b[l].reshape(3, 1, HYENA_WIDTH)
    y_hyena = _hyena(proj, conv_w, conv_b, kspec, row(hyena_bias), cmat, sfwd, sinv, batch, seq, p)

    pos = jnp.arange(seq, dtype=F32)
    inv_freq = 1.0 / (ROPE_THETA ** (jnp.arange(0, QK_ROPE_DIM, 2, dtype=F32) / QK_ROPE_DIM))
    ang = pos[:, None] * inv_freq[None, :]
    ang = jnp.concatenate([ang, ang], axis=-1)
    lpad = ((0, 0), (0, V7X_LANES - QK_ROPE_DIM))
    cos_t = jnp.pad(jnp.cos(ang), lpad)
    sin_t = jnp.pad(jnp.sin(ang), lpad)
    dqk = QK_NOPE_DIM + QK_ROPE_DIM
    wq = w_uq[l].reshape(Q_LORA_RANK, N_HEADS, dqk)
    wa = jnp.pad(wq, ((0, 0), (0, 0), (0, QK_PAD_DIM - dqk))).reshape(Q_LORA_RANK, N_HEADS * QK_PAD_DIM)
    q, k, kro, v = _qkv(proj, cos_t, sin_t, row(q_norm_gain), row(kv_norm_gain), wa.astype(BF16),
                   w_ukv[l].astype(BF16), batch, seq)
    y_attn = _attn(q, k, kro, v).reshape(batch * seq, ATTN_WIDTH)

    h2d, hn, w_down_bf = _outproj(y_hyena, y_attn, row(hyena_out_gain), row(attn_out_gain), w_out[l].astype(BF16),
                                  x2d, row(post_mix_gain), row(pre_ffn_gain), w_down[l])

    act = _ffn_up(hn, w_up[l], ffn_conv_w[l], row(ffn_conv_b), batch, seq)
    out = _ffn_down(act, w_down_bf, h2d, row(post_ffn_gain))
    return out.reshape(batch, seq, D_MODEL)


def kernel(x, pre_mix_gain, w_in, hyena_conv_w, hyena_conv_b, filt_w1, filt_b1, filt_freq1, filt_w2, filt_b2,
           filt_freq2, filt_w3, hyena_bias, q_norm_gain, w_uq, kv_norm_gain, w_ukv, hyena_out_gain,
           attn_out_gain, w_out, post_mix_gain, pre_ffn_gain, w_up, ffn_conv_w, ffn_conv_b, w_down,
           post_ffn_gain):
    prm = (pre_mix_gain, w_in, hyena_conv_w, hyena_conv_b, filt_w1, filt_b1, filt_freq1, filt_w2, filt_b2,
           filt_freq2, filt_w3, hyena_bias, q_norm_gain, w_uq, kv_norm_gain, w_ukv, hyena_out_gain,
           attn_out_gain, w_out, post_mix_gain, pre_ffn_gain, w_up, ffn_conv_w, ffn_conv_b, w_down,
           post_ffn_gain)
    seq = x.shape[1]
    h = x
    for l in range(w_in.shape[0]):
        h = _layer(h, l, seq, prm)
    return h
```

```python
import functools
import math

import jax
import jax.numpy as jnp
from jax import lax
from jax.experimental import pallas as pl
from jax.experimental.pallas import tpu as pltpu

F32 = jnp.float32
BF16 = jnp.bfloat16

D_MODEL = 2048
HYENA_WIDTH = 1024
CONV_WIDTH = 3
FILTER_EMB_DIM = 33
FILTER_HIDDEN = 64
DECAY_TARGET = 1e-2
FAST_DECAY_PCT = 0.3
SLOW_DECAY_PCT = 1.5
DECAY_SHIFT = 0.05
N_HEADS = 8
QK_NOPE_DIM = 128
QK_ROPE_DIM = 64
V_HEAD_DIM = 128
Q_LORA_RANK = 512
KV_LORA_RANK = 256
ROPE_THETA = 10000.0
ATTN_WIDTH = N_HEADS * V_HEAD_DIM
D_FF = 5632
NORM_EPS = 1e-6

V7X_VMEM_BYTES = 64 * 1024 * 1024
V7X_LANES = 128
V7X_MXU_DIM = 256
V7X_SUBLANES = 8
HALO_BF16 = 2 * V7X_SUBLANES
VMEM_RESERVE_BYTES = 4 * 1024 * 1024
VMEM_TEMP_BYTES = 6 * 1024 * 1024

QK_PAD_DIM = V7X_MXU_DIM
PROJ_WIDTH = 4096
COL_CQ = 3 * HYENA_WIDTH
COL_CKV = COL_CQ + Q_LORA_RANK
COL_KPE = COL_CKV + KV_LORA_RANK


def _params(semantics, block_bytes):
    limit = min(int(block_bytes) + VMEM_TEMP_BYTES, V7X_VMEM_BYTES - VMEM_RESERVE_BYTES)
    return pltpu.CompilerParams(dimension_semantics=semantics, vmem_limit_bytes=limit)


def _rms(x, gain):
    return x * lax.rsqrt(jnp.mean(x * x, axis=-1, keepdims=True) + NORM_EPS) * gain


def _bdot(a, b):
    return jnp.dot(a, b, preferred_element_type=F32)


def _dot3(a, b):
    a_hi, b_hi = a.astype(BF16), b.astype(BF16)
    a_lo, b_lo = (a - a_hi.astype(F32)).astype(BF16), (b - b_hi.astype(F32)).astype(BF16)
    return _bdot(a_hi, b_hi) + (_bdot(a_hi, b_lo) + _bdot(a_lo, b_hi))


def _chunk_bounds(r0, rows, seq):
    return max(r0 - HALO_BF16, 0), min(r0 + rows + HALO_BF16, seq)


def _seq_conv3_chunk(xe, w, b, r0, rows, seq):
    lo, hi = _chunk_bounds(r0, rows, seq)
    n = hi - lo
    prev = pltpu.roll(xe, 1, 0)
    nxt = pltpu.roll(xe, n - 1, 0)
    if lo == 0 or hi == seq:
        row = lax.broadcasted_iota(jnp.int32, xe.shape, 0)
        if lo == 0:
            prev = jnp.where(row == 0, 0.0, prev)
        if hi == seq:
            nxt = jnp.where(row == n - 1, 0.0, nxt)
    return (prev * w[0:1] + xe * w[1:2] + nxt * w[2:3] + b)[r0 - lo:r0 - lo + rows]


def _inproj_body(x_ref, g_ref, wm_ref, wt_ref, o_ref, *, n_main, rows):
    j = pl.program_id(1)

    def project(w_ref):
        w = w_ref[...]
        for r0 in range(0, x_ref.shape[0], rows):
            xn = _rms(x_ref[r0:r0 + rows, :], g_ref[...]).astype(BF16)
            o_ref[r0:r0 + rows, :] = lax.dot_general(xn, w, (((1,), (1,)), ((), ())),
                                                     preferred_element_type=F32).astype(o_ref.dtype)

    @pl.when(j < n_main)
    def _():
        project(wm_ref)

    @pl.when(j == n_main)
    def _():
        project(wt_ref)


def _inproj(x2d, gain, w_main, w_tail):
    m, d = x2d.shape
    tm, tn = 1024, w_tail.shape[0]
    n_main = w_main.shape[0] // tn
    vmem = 2 * tm * d * 4 + tm * d * 2 + 4 * d * tn * 2 + 2 * tm * tn * 2 + tm * tn * 4
    return pl.pallas_call(
        functools.partial(_inproj_body, n_main=n_main, rows=256),
        grid=(m // tm, n_main + 1),
        in_specs=[pl.BlockSpec((tm, d), lambda i, j: (i, 0)),
                  pl.BlockSpec((1, d), lambda i, j: (0, 0)),
                  pl.BlockSpec((tn, d), lambda i, j: (jnp.minimum(j, n_main - 1), 0)),
                  pl.BlockSpec((tn, d), lambda i, j: (0, 0))],
        out_specs=pl.BlockSpec((tm, tn), lambda i, j: (i, j)),
        out_shape=jax.ShapeDtypeStruct((m, (n_main + 1) * tn), BF16),
        compiler_params=_params(("parallel", "arbitrary"), vmem),
        name="inproj",
    )(x2d, gain, w_main, w_tail)


def _dft_mats(p):
    r = 32
    assert p == r * r
    idx = jnp.arange(p, dtype=jnp.int32)
    sub = jnp.arange(r, dtype=jnp.int32)
    ang_hi = ((idx[:, None] * (r * sub)[None, :]) % (2 * p)).astype(F32) * (math.pi / p)
    ang_lo = ((idx[:, None] * sub[None, :]) % (2 * p)).astype(F32) * (math.pi / p)
    ch, sh = jnp.cos(ang_hi)[:, :, None], jnp.sin(ang_hi)[:, :, None]
    cl, sl = jnp.cos(ang_lo)[:, None, :], jnp.sin(ang_lo)[:, None, :]
    c = (ch * cl - sh * sl).reshape(p, p)
    s = -(sh * cl + ch * sl).reshape(p, p)
    alt = jnp.where(idx % 2 == 0, 1.0, -1.0).astype(F32)
    s_fwd = jnp.where(idx[:, None] == 0, alt[None, :], s)
    s_inv = jnp.where(idx[None, :] == 0, alt[:, None], s)
    return c.astype(BF16), s_fwd.astype(BF16), s_inv.astype(BF16)


def _filter_body(z_ref, t_ref, w1_ref, b1_ref, f1_ref, w2_ref, b2_ref, f2_ref, w3f_ref, w3b_ref,
                 dl_ref, c_ref, s_ref, o_ref, hid_ref, *, p):
    @pl.when(pl.program_id(0) == 0)
    def _():
        h1 = jnp.sin(f1_ref[...] * (_dot3(z_ref[...], w1_ref[...]) + b1_ref[...]))
        hid_ref[...] = jnp.sin(f2_ref[...] * (_dot3(h1, w2_ref[...]) + b2_ref[...]))

    h = hid_ref[...]
    win = jnp.exp(-t_ref[...] * dl_ref[...]) + DECAY_SHIFT
    hf = _dot3(h, w3f_ref[...]) * win
    hb = _dot3(h, w3b_ref[...]) * win
    lrow = lax.broadcasted_iota(jnp.int32, hb.shape, 0)
    hb = jnp.where(lrow == 0, 0.0, hb)

    cmat = c_ref[...]
    smat = s_ref[...]

    def fwd(x):
        xb = x.astype(BF16)
        return _bdot(cmat, xb), _bdot(smat, xb)

    row = lax.broadcasted_iota(jnp.int32, (p, hf.shape[1]), 0)
    row0 = row == 0
    sigma = jnp.where(row % 2 == 1, -1.0, 1.0)

    def conj(a):
        return a[0], jnp.where(row0, a[1], -a[1])

    af0, af1 = fwd(hf[:p]), fwd(hf[p:])
    ab0, ab1 = fwd(hb[:p]), fwd(hb[p:])
    cb0 = conj(ab0)
    k0 = (af0[0] + cb0[0], af0[1] + cb0[1])
    k1 = (af1[0] + sigma * af0[0], af1[1] + sigma * af0[1])
    km1 = conj((ab1[0] + sigma * ab0[0], ab1[1] + sigma * ab0[1]))
    scale = jnp.where(row0, 0.5 / p, 1.0 / p)
    for i, a in enumerate((k0, k1, km1)):
        o_ref[2 * i] = a[0] * scale
        o_ref[2 * i + 1] = a[1] * scale


def _filter_spectra(z, t, w1, b1, f1, w2, b2, f2, w3, deltas, cmat, smat, p):
    l = z.shape[0]
    c = HYENA_WIDTH
    tc = 256
    nb = c // tc
    full = lambda a: pl.BlockSpec(a.shape, lambda j: (0,) * a.ndim)
    return pl.pallas_call(
        functools.partial(_filter_body, p=p),
        grid=(nb,),
        in_specs=[full(z), full(t), full(w1), full(b1), full(f1), full(w2), full(b2), full(f2),
                  pl.BlockSpec((FILTER_HIDDEN, tc), lambda j: (0, j)),
                  pl.BlockSpec((FILTER_HIDDEN, tc), lambda j: (0, j + nb)),
                  pl.BlockSpec((1, tc), lambda j: (0, j)),
                  full(cmat), full(smat)],
        out_specs=pl.BlockSpec((6, p, tc), lambda j: (0, 0, j)),
        out_shape=jax.ShapeDtypeStruct((6, p, c), F32),
        scratch_shapes=[pltpu.VMEM((l, FILTER_HIDDEN), F32)],
        compiler_params=_params(("arbitrary",), 2 * 6 * p * tc * 4 + 2 * 2 * p * p * 2 + 10 * l * tc * 4),
        name="filt",
    )(z, t, w1, b1, f1, w2, b2, f2, w3, w3, deltas, cmat, smat)


def _hyena_body(x0_ref, x1_ref, v_ref, cw_ref, cb_ref, ks_ref, hb_ref, c_ref, sf_ref, si_ref, o_ref, *, p):
    cmat = c_ref[...]
    sfwd = sf_ref[...]
    sinv = si_ref[...]
    seq = 2 * p
    row0 = lax.broadcasted_iota(jnp.int32, (p, o_ref.shape[1]), 0) == 0

    def cmul(i, uu):
        kre, kim = ks_ref[2 * i], ks_ref[2 * i + 1]
        ii = kim * uu[1]
        return (kre * uu[0] - jnp.where(row0, 0.0, ii),
                jnp.where(row0, ii, kre * uu[1] + kim * uu[0]))

    def inv(a, b):
        return _bdot(cmat, (a[0] + b[0]).astype(BF16)) + _bdot(sinv, (a[1] + b[1]).astype(BF16))

    def conv(x_ref, i, r0):
        lo, hi = _chunk_bounds(r0, p, seq)
        return _seq_conv3_chunk(x_ref[lo:hi, :].astype(F32), cw_ref[i], cb_ref[i], r0, p, seq)

    us, spec = [], []
    for j in range(2):
        u = conv(x1_ref, 1, j * p) * conv(v_ref, 2, j * p)
        ub = u.astype(BF16)
        us.append(u)
        spec.append((_bdot(cmat, ub), _bdot(sfwd, ub)))
    for j, (ia, ib) in enumerate(((0, 2), (1, 0))):
        y = inv(cmul(ia, spec[0]), cmul(ib, spec[1]))
        x0 = conv(x0_ref, 0, j * p)
        o_ref[j * p:(j + 1) * p, :] = (x0 * (y + us[j] * hb_ref[...])).astype(o_ref.dtype)


def _hyena(proj, conv_w, conv_b, kspec, hbias, cmat, sfwd, sinv, batch, seq, p):
    c = HYENA_WIDTH
    tc = 256
    nb = c // tc
    xspec = lambda off: pl.BlockSpec((seq, tc), lambda j, b: (b, j + off * nb))
    full = lambda a: pl.BlockSpec(a.shape, lambda j, b: (0,) * a.ndim)
    vmem = (6 * seq * tc * 2 + 2 * 6 * p * tc * 4 + 6 * p * p * 2 + 2 * seq * tc * 2
            + 6 * seq * tc * 4)
    return pl.pallas_call(
        functools.partial(_hyena_body, p=p),
        grid=(nb, batch),
        in_specs=[xspec(0), xspec(1), xspec(2),
                  pl.BlockSpec((3, CONV_WIDTH, tc), lambda j, b: (0, 0, j)),
                  pl.BlockSpec((3, 1, tc), lambda j, b: (0, 0, j)),
                  pl.BlockSpec((6, p, tc), lambda j, b: (0, 0, j)),
                  pl.BlockSpec((1, tc), lambda j, b: (0, j)),
                  full(cmat), full(sfwd), full(sinv)],
        out_specs=pl.BlockSpec((seq, tc), lambda j, b: (b, j)),
        out_shape=jax.ShapeDtypeStruct((batch * seq, c), BF16),
        compiler_params=_params(("parallel", "parallel"), vmem),
        name="hyena",
    )(proj, proj, proj, conv_w, conv_b, kspec, hbias, cmat, sfwd, sinv)


def _rotate_half_lanes(x):
    half = QK_ROPE_DIM // 2
    lane = lax.broadcasted_iota(jnp.int32, x.shape, 1)
    return jnp.where(lane < half, -pltpu.roll(x, V7X_LANES - half, 1), pltpu.roll(x, half, 1))


def _qkv_body(cq_ref, ckv_ref, kpe_ref, cos_ref, sin_ref, gq_ref, gkv_ref, wa_ref, wkv_ref,
              q_ref, k_ref, kro_ref, v_ref, *, rows):
    scale = (QK_NOPE_DIM + QK_ROPE_DIM) ** -0.5 * math.log2(math.e)
    n = cq_ref.shape[0] // rows

    def project(c):
        rs = slice(c * rows, (c + 1) * rows)
        cqn = _rms(cq_ref[rs, :].astype(F32), gq_ref[...]).astype(BF16)
        ckvn = _rms(ckv_ref[rs, :].astype(F32), gkv_ref[...]).astype(BF16)
        return (_bdot(cqn, wa_ref[...]),
                _bdot(ckvn, wkv_ref[...]))

    nxt = project(0)
    for c in range(n):
        qa, kv = nxt
        if c + 1 < n:
            nxt = project(c + 1)
        rs = slice(c * rows, (c + 1) * rows)
        cos = cos_ref[rs, :]
        sin = sin_ref[rs, :]
        kpe = kpe_ref[rs, :].astype(F32)
        kro_ref[0, rs, :] = (kpe * cos + _rotate_half_lanes(kpe) * sin).astype(BF16)
        for h in range(N_HEADS):
            a = h * QK_PAD_DIM
            qpe = qa[:, a + 128:a + 256]
            q_ref[0, h, rs, 0:128] = (qa[:, a:a + 128] * scale).astype(BF16)
            q_ref[0, h, rs, 128:256] = ((qpe * cos + _rotate_half_lanes(qpe) * sin) * scale).astype(BF16)
            k_ref[0, h, rs, :] = kv[:, a:a + 128].astype(BF16)
            v_ref[0, h, rs, :] = kv[:, a + 128:a + 256].astype(BF16)


def _qkv(proj, cos_t, sin_t, gq, gkv, wa, wkv, batch, seq):
    tm = 1024
    ns = seq // tm
    full = lambda a: pl.BlockSpec(a.shape, lambda b, i: (0,) * a.ndim)
    col = lambda width, off: pl.BlockSpec((tm, width), lambda b, i: (b * ns + i, off // width))
    hd = lambda w: pl.BlockSpec((1, N_HEADS, tm, w), lambda b, i: (b, 0, i, 0))
    vmem = (2 * tm * (Q_LORA_RANK + KV_LORA_RANK + V7X_LANES) * 2 + 4 * tm * V7X_LANES * 4 + 2 * (wa.size + wkv.size) * 2
            + 2 * N_HEADS * tm * (2 * QK_PAD_DIM + V_HEAD_DIM) * 2 + 4 * tm * N_HEADS * QK_PAD_DIM * 4)
    return pl.pallas_call(
        functools.partial(_qkv_body, rows=tm // 4),
        grid=(batch, ns),
        in_specs=[col(Q_LORA_RANK, COL_CQ), col(KV_LORA_RANK, COL_CKV), col(V7X_LANES, COL_KPE),
                  pl.BlockSpec((tm, V7X_LANES), lambda b, i: (i, 0)),
                  pl.BlockSpec((tm, V7X_LANES), lambda b, i: (i, 0)),
                  full(gq), full(gkv), full(wa), full(wkv)],
        out_specs=[hd(QK_PAD_DIM), hd(QK_NOPE_DIM), pl.BlockSpec((1, tm, V7X_LANES), lambda b, i: (b, i, 0)),
                   hd(V_HEAD_DIM)],
        out_shape=[jax.ShapeDtypeStruct((batch, N_HEADS, seq, QK_PAD_DIM), BF16),
                   jax.ShapeDtypeStruct((batch, N_HEADS, seq, QK_NOPE_DIM), BF16),
                   jax.ShapeDtypeStruct((batch, seq, V7X_LANES), BF16),
                   jax.ShapeDtypeStruct((batch, N_HEADS, seq, V_HEAD_DIM), BF16)],
        compiler_params=_params(("parallel", "parallel"), vmem),
        name="qkv",
    )(proj, proj, proj, cos_t, sin_t, gq, gkv, wa, wkv)


def _attn_body(q_ref, k_ref, kro_ref, v_ref, o_ref, *, tq):
    heads, seq = q_ref.shape[1], q_ref.shape[2]
    work = [(h, c) for h in range(heads) for c in range(seq // tq)]
    keys = [jnp.concatenate([k_ref[0, h], kro_ref[0]], axis=1) for h in range(heads)]

    def scores(h, c):
        return lax.dot_general(q_ref[0, h, c * tq:(c + 1) * tq, :], keys[h], (((1,), (1,)), ((), ())),
                               preferred_element_type=F32)

    s_next = scores(*work[0])
    for i, (h, c) in enumerate(work):
        s = s_next
        if i + 1 < len(work):
            s_next = scores(*work[i + 1])
        m = jnp.max(s, axis=-1, keepdims=True)
        e = jnp.exp2(s - m)
        l = jnp.sum(e, axis=-1, keepdims=True)
        o = _bdot(e.astype(BF16), v_ref[0, h])
        o_ref[0, c * tq:(c + 1) * tq, h * V_HEAD_DIM:(h + 1) * V_HEAD_DIM] = (o / l).astype(o_ref.dtype)


def _attn(q, k, kro, v):
    batch, heads, seq, _ = q.shape
    tq = 256
    hps = 2
    hspec = lambda w: pl.BlockSpec((1, hps, seq, w), lambda b, h: (b, h, 0, 0))
    vmem = 2 * hps * seq * (2 * QK_PAD_DIM + 2 * V_HEAD_DIM) * 2 + 12 * tq * seq * 4
    return pl.pallas_call(
        functools.partial(_attn_body, tq=tq),
        grid=(batch, heads // hps),
        in_specs=[hspec(QK_PAD_DIM), hspec(QK_NOPE_DIM), pl.BlockSpec((1, seq, V7X_LANES), lambda b, h: (b, 0, 0)),
                  hspec(V_HEAD_DIM)],
        out_specs=pl.BlockSpec((1, seq, hps * V_HEAD_DIM), lambda b, h: (b, 0, h)),
        out_shape=jax.ShapeDtypeStruct((batch, seq, heads * V_HEAD_DIM), BF16),
        compiler_params=_params(("parallel", "parallel"), vmem),
        name="attn",
    )(q, k, kro, v)


def _outproj_body(yh_ref, ya_ref, gh_ref, ga_ref, w_ref, x_ref, gpm_ref, gpf_ref, wd_ref, h_ref, hn_ref, wdb_ref,
                  *, rows):
    wdb_ref[...] = wd_ref[...].astype(BF16)
    n = x_ref.shape[0] // rows

    def mix(c):
        rs = slice(c * rows, (c + 1) * rows)
        a = _rms(yh_ref[rs, :].astype(F32), gh_ref[...]).astype(BF16)
        b = _rms(ya_ref[rs, :].astype(F32), ga_ref[...]).astype(BF16)
        return _bdot(a, w_ref[0:HYENA_WIDTH, :]) + _bdot(b, w_ref[HYENA_WIDTH:, :])

    nxt = mix(0)
    for c in range(n):
        mixed = nxt
        if c + 1 < n:
            nxt = mix(c + 1)
        rs = slice(c * rows, (c + 1) * rows)
        h = x_ref[rs, :] + _rms(mixed, gpm_ref[...])
        h_ref[rs, :] = h
        hn_ref[rs, :] = _rms(h, gpf_ref[...]).astype(BF16)


def _outproj(yh, ya, gh, ga, w_out, x2d, gpm, gpf, w_down):
    m, d = x2d.shape
    tm = 512
    full = lambda a: pl.BlockSpec(a.shape, lambda i: (0,) * a.ndim)
    rows = lambda w: pl.BlockSpec((tm, w), lambda i: (i, 0))
    wd_rows = pl.BlockSpec((w_down.shape[0] // (m // tm), d), lambda i: (i, 0))
    vmem = 2 * 2 * tm * HYENA_WIDTH * 2 + 2 * d * d * 2 + 2 * tm * d * (4 + 4 + 2) + 4 * tm * d * 4
    return pl.pallas_call(
        functools.partial(_outproj_body, rows=tm // 2),
        grid=(m // tm,),
        in_specs=[rows(HYENA_WIDTH), rows(ATTN_WIDTH), full(gh), full(ga), full(w_out), rows(d),
                  full(gpm), full(gpf), wd_rows],
        out_specs=[rows(d), rows(d), wd_rows],
        out_shape=[jax.ShapeDtypeStruct((m, d), F32), jax.ShapeDtypeStruct((m, d), BF16),
                   jax.ShapeDtypeStruct(w_down.shape, BF16)],
        compiler_params=_params(("parallel",), vmem),
        name="outproj",
    )(yh, ya, gh, ga, w_out, x2d, gpm, gpf, w_down)


def _ffn_up_body(hn_ref, wg_ref, wu_ref, cw_ref, cb_ref, o_ref, wgb_ref, wub_ref, *, rows):
    @pl.when(pl.program_id(1) == 0)
    def _():
        wgb_ref[...] = wg_ref[...].astype(BF16)
        wub_ref[...] = wu_ref[...].astype(BF16)

    seq = hn_ref.shape[0]
    wg, wu = wgb_ref[...], wub_ref[...]
    for r0 in range(0, seq, rows):
        lo, hi = _chunk_bounds(r0, rows, seq)
        g = _seq_conv3_chunk(_bdot(hn_ref[lo:hi, :], wg), cw_ref[...], cb_ref[...], r0, rows, seq)
        u = _bdot(hn_ref[r0:r0 + rows, :], wu)
        gelu = 0.5 * g * (1.0 + jnp.tanh(math.sqrt(2.0 / math.pi) * (g + 0.044715 * (g * g * g))))
        o_ref[r0:r0 + rows, :] = (gelu * u).astype(o_ref.dtype)


def _ffn_up(hn, w_up, conv_w, conv_b, batch, seq):
    d = hn.shape[1]
    tn = 512
    nb = D_FF // tn
    wbf = pltpu.VMEM((d, tn), BF16)
    vmem = 2 * seq * d * 2 + 2 * 2 * d * tn * 4 + 2 * d * tn * 2 + 2 * seq * tn * 2 + 8 * seq * tn * 4
    return pl.pallas_call(
        functools.partial(_ffn_up_body, rows=1024),
        grid=(nb, batch),
        in_specs=[pl.BlockSpec((seq, d), lambda j, b: (b, 0)),
                  pl.BlockSpec((d, tn), lambda j, b: (0, j)),
                  pl.BlockSpec((d, tn), lambda j, b: (0, j + nb)),
                  pl.BlockSpec((CONV_WIDTH, tn), lambda j, b: (0, j)),
                  pl.BlockSpec((1, tn), lambda j, b: (0, j))],
        out_specs=pl.BlockSpec((seq, tn), lambda j, b: (b, j)),
        out_shape=jax.ShapeDtypeStruct((batch * seq, D_FF), BF16),
        scratch_shapes=[wbf, wbf],
        compiler_params=_params(("parallel", "arbitrary"), vmem),
        name="ffn_up",
    )(hn, w_up, w_up, conv_w, conv_b)


def _ffn_down_body(a_ref, w_ref, h_ref, g_ref, o_ref, *, rows):
    for r0 in range(0, a_ref.shape[0], rows):
        rs = slice(r0, r0 + rows)
        o_ref[rs, :] = h_ref[rs, :] + _rms(_bdot(a_ref[rs, :], w_ref[...]), g_ref[...])


def _ffn_down(act, w_down, h, gain):
    m, d = h.shape
    k = act.shape[1]
    tm = 512
    vmem = k * d * 2 + 2 * tm * k * 2 + 2 * 2 * tm * d * 4 + tm * d * 4
    return pl.pallas_call(
        functools.partial(_ffn_down_body, rows=tm // 2),
        grid=(m // tm,),
        in_specs=[pl.BlockSpec((tm, k), lambda i: (i, 0)),
                  pl.BlockSpec((k, d), lambda i: (0, 0), pipeline_mode=pl.Buffered(1)),
                  pl.BlockSpec((tm, d), lambda i: (i, 0)),
                  pl.BlockSpec((1, d), lambda i: (0, 0))],
        out_specs=pl.BlockSpec((tm, d), lambda i: (i, 0)),
        out_shape=jax.ShapeDtypeStruct((m, d), F32),
        compiler_params=_params(("arbitrary",), vmem),
        name="ffn_down",
    )(act, w_down, h, gain)


def _position_features(l):
    t = jnp.linspace(0.0, 1.0, l, dtype=F32)[:, None]
    bands = (FILTER_EMB_DIM - 1) // 2
    w = 2.0 * math.pi * jnp.arange(l, dtype=F32) / l
    f = jnp.linspace(1e-4, bands - 1, bands, dtype=F32)
    ang = w[:, None] * f[None, :]
    return t, jnp.concatenate([t, jnp.cos(ang), -jnp.sin(ang)], axis=-1)


def _layer(h, l, seq, prm):
    (pre_mix_gain, w_in, hyena_conv_w, hyena_conv_b, filt_w1, filt_b1, filt_freq1, filt_w2, filt_b2,
     filt_freq2, filt_w3, hyena_bias, q_norm_gain, w_uq, kv_norm_gain, w_ukv, hyena_out_gain, attn_out_gain,
     w_out, post_mix_gain, pre_ffn_gain, w_up, ffn_conv_w, ffn_conv_b, w_down, post_ffn_gain) = prm
    batch = h.shape[0]
    p = seq // 2
    x2d = h.reshape(batch * seq, D_MODEL)
    row = lambda a: a[l][None, :].astype(F32)

    wt = jnp.swapaxes(w_in[l], 0, 1)
    w_tail = jnp.pad(wt[COL_CQ:].astype(BF16), ((0, PROJ_WIDTH - wt.shape[0]), (0, 0)))
    proj = _inproj(x2d, row(pre_mix_gain), wt[:COL_CQ].astype(BF16), w_tail)

    cmat, sfwd, sinv = _dft_mats(p)
    t, z = _position_features(seq)
    zp = jnp.pad(z, ((0, 0), (0, V7X_LANES - FILTER_EMB_DIM)))
    w1p = jnp.pad(filt_w1[l], ((0, V7X_LANES - FILTER_EMB_DIM), (0, 0)))
    max_decay = math.log(DECAY_TARGET) / FAST_DECAY_PCT
    min_decay = math.log(DECAY_TARGET) / SLOW_DECAY_PCT
    deltas = jnp.abs(jnp.linspace(min_decay, max_decay, HYENA_WIDTH, dtype=F32))[None, :]
    kspec = _filter_spectra(zp, t, w1p, row(filt_b1), row(filt_freq1), filt_w2[l], row(filt_b2),
                            row(filt_freq2), filt_w3[l], deltas, cmat, sfwd, p)
    conv_w = hyena_conv_w[l].reshape(CONV_WIDTH, 3, HYENA_WIDTH).transpose(1, 0, 2)
    conv_b = hyena_conv_b[l].reshape(3, 1, HYENA_WIDTH)
    y_hyena = _hyena(proj, conv_w, conv_b, kspec, row(hyena_bias), cmat, sfwd, sinv, batch, seq, p)

    pos = jnp.arange(seq, dtype=F32)
    inv_freq = 1.0 / (ROPE_THETA ** (jnp.arange(0, QK_ROPE_DIM, 2, dtype=F32) / QK_ROPE_DIM))
    ang = pos[:, None] * inv_freq[None, :]
    ang = jnp.concatenate([ang, ang], axis=-1)
    lpad = ((0, 0), (0, V7X_LANES - QK_ROPE_DIM))
    cos_t = jnp.pad(jnp.cos(ang), lpad)
    sin_t = jnp.pad(jnp.sin(ang), lpad)
    dqk = QK_NOPE_DIM + QK_ROPE_DIM
    wq = w_uq[l].reshape(Q_LORA_RANK, N_HEADS, dqk)
    wa = jnp.pad(wq, ((0, 0), (0, 0), (0, QK_PAD_DIM - dqk))).reshape(Q_LORA_RANK, N_HEADS * QK_PAD_DIM)
    q, k, kro, v = _qkv(proj, cos_t, sin_t, row(q_norm_gain), row(kv_norm_gain), wa.astype(BF16),
                   w_ukv[l].astype(BF16), batch, seq)
    y_attn = _attn(q, k, kro, v).reshape(batch * seq, ATTN_WIDTH)

    h2d, hn, w_down_bf = _outproj(y_hyena, y_attn, row(hyena_out_gain), row(attn_out_gain), w_out[l].astype(BF16),
                                  x2d, row(post_mix_gain), row(pre_ffn_gain), w_down[l])

    act = _ffn_up(hn, w_up[l], ffn_conv_w[l], row(ffn_conv_b), batch, seq)
    out = _ffn_down(act, w_down_bf, h2d, row(post_ffn_gain))
    return out.reshape(batch, seq, D_MODEL)


def kernel(x, pre_mix_gain, w_in, hyena_conv_w, hyena_conv_b, filt_w1, filt_b1, filt_freq1, filt_w2, filt_b2,
           filt_freq2, filt_w3, hyena_bias, q_norm_gain, w_uq, kv_norm_gain, w_ukv, hyena_out_gain,
           attn_out_gain, w_out, post_mix_gain, pre_ffn_gain, w_up, ffn_conv_w, ffn_conv_b, w_down,
           post_ffn_gain):
    prm = (pre_mix_gain, w_in, hyena_conv_w, hyena_conv_b, filt_w1, filt_b1, filt_freq1, filt_w2, filt_b2,
           filt_freq2, filt_w3, hyena_bias, q_norm_gain, w_uq, kv_norm_gain, w_ukv, hyena_out_gain,
           attn_out_gain, w_out, post_mix_gain, pre_ffn_gain, w_up, ffn_conv_w, ffn_conv_b, w_down,
           post_ffn_gain)
    seq = x.shape[1]
    h = x
    for l in range(w_in.shape[0]):
        h = _layer(h, l, seq, prm)
    return h
```
